```python
import math
import jax, jax.numpy as jnp
from jax import lax
import numpy as np

D_MODEL = 1024
BATCH = 16
SEQ = 256
DEPTH = 2
DEC_BATCH = 8
DEC_SEQ = 1024
PAST_LEN = 512

GRID_W = 64
EPS = 1e-6
N_A_LAYERS = (DEPTH + 1) // 2
N_C_LAYERS = DEPTH // 2

MLA_HEADS = D_MODEL // 128
Q_RANK = D_MODEL // 4
KV_RANK = D_MODEL // 8
QK_NOPE = 64
QK_ROPE = 32
V_HEAD = 64
MLA_WIDTH = MLA_HEADS * V_HEAD
ATTN_SCALE = (QK_NOPE + QK_ROPE) ** -0.5
ROPE_THETA = 10000.0
Q_BLOCK = 128

CONV_WIDTH = D_MODEL // 2
CONV_K = 31

A_SPLITS = [Q_RANK, Q_RANK + KV_RANK, Q_RANK + KV_RANK + QK_ROPE,
            Q_RANK + KV_RANK + QK_ROPE + 2 * CONV_WIDTH]
A_MIX = MLA_WIDTH + CONV_WIDTH
A_IN = A_SPLITS[-1] + A_MIX

SSD_INNER = 2 * D_MODEL
SSD_HEAD_DIM = 64
SSD_HEADS = SSD_INNER // SSD_HEAD_DIM
SSD_GROUPS = 4
SSD_HPG = SSD_HEADS // SSD_GROUPS
SSD_STATE = 128
SSD_CONV_K = 5
SSD_CHUNK = 128
SSD_CONV_CH = SSD_INNER + 2 * SSD_GROUPS * SSD_STATE
C_IN = SSD_INNER + SSD_CONV_CH + 2 * SSD_HEADS

kernel_name = 'hybrid_mla_conformer_ssd_diffusion_step'


def rmsnorm(x, g):
    xf = x.astype(jnp.float32)
    y = xf * lax.rsqrt(jnp.mean(xf * xf, axis=-1, keepdims=True) + EPS)
    return (y * g.astype(jnp.float32)).astype(x.dtype)


def layernorm(x, g, b):
    xf = x.astype(jnp.float32)
    mu = jnp.mean(xf, axis=-1, keepdims=True)
    xc = xf - mu
    y = xc * lax.rsqrt(jnp.mean(xc * xc, axis=-1, keepdims=True) + EPS)
    return (y * g.astype(jnp.float32) + b.astype(jnp.float32)).astype(x.dtype)


def modulation(cond, w_mod, b_mod):
    m = jax.nn.silu(cond) @ w_mod + b_mod
    shift, scale, gate = jnp.split(m, 3, axis=-1)
    return shift[..., None, :], scale[..., None, :], gate[..., None, :]


def dwconv(x, w, b):
    k = w.shape[0]
    y = lax.conv_general_dilated(x, w[:, None, :].astype(x.dtype), window_strides=(1,),
                                 padding=[(k // 2, k // 2)],
                                 dimension_numbers=('NWC', 'WIO', 'NWC'),
                                 feature_group_count=x.shape[-1])
    return y + b


def axial_rope(length):
    rows = length // GRID_W
    row = jnp.repeat(jnp.arange(rows, dtype=jnp.float32), GRID_W)
    col = jnp.tile(jnp.arange(GRID_W, dtype=jnp.float32), rows)
    n_freq = QK_ROPE // 4
    inv = jnp.power(ROPE_THETA, -jnp.arange(n_freq, dtype=jnp.float32) / n_freq)
    ang = jnp.concatenate([row[:, None] * inv, col[:, None] * inv], axis=-1)
    return jnp.cos(ang), jnp.sin(ang)


def apply_rope(x, cos, sin):
    xf = x.astype(jnp.float32)
    x1, x2 = xf[..., 0::2], xf[..., 1::2]
    out = jnp.stack([x1 * cos - x2 * sin, x1 * sin + x2 * cos], axis=-1).reshape(x.shape)
    return out.astype(x.dtype)


def mla_attend(q_nope, q_rope, k_nope, k_rope, v):
    b, lq, h, _ = q_nope.shape
    nblk = lq // Q_BLOCK

    def to_blocks(t):
        return t.reshape(b, nblk, Q_BLOCK, *t.shape[2:]).swapaxes(0, 1)

    def one_block(qs):
        qn, qr = qs
        s = (jnp.einsum('bqhd,bkhd->bhqk', qn, k_nope, preferred_element_type=jnp.float32)
             + jnp.einsum('bqhr,bkr->bhqk', qr, k_rope, preferred_element_type=jnp.float32)) * ATTN_SCALE
        p = jax.nn.softmax(s, axis=-1).astype(v.dtype)
        return jnp.einsum('bhqk,bkhd->bqhd', p, v)

    out = lax.map(one_block, (to_blocks(q_nope), to_blocks(q_rope)))
    return out.swapaxes(0, 1).reshape(b, lq, h * V_HEAD)


def layer_a(h, w_in, g_q, g_kv, w_uq, w_uk, w_uv, conv_w, conv_b, ln_g, ln_b, w_out,
            ctx_ckv=None, ctx_krope=None):
    b, L, _ = h.shape
    u = h @ w_in
    q_lat, kv_lat, k_rope, glu, gate = jnp.split(u, A_SPLITS, axis=-1)
    ckv = rmsnorm(kv_lat, g_kv)
    q = (rmsnorm(q_lat, g_q) @ w_uq).reshape(b, L, MLA_HEADS, QK_NOPE + QK_ROPE)
    q_nope, q_rope = q[..., :QK_NOPE], q[..., QK_NOPE:]
    if ctx_ckv is None:
        ckv_all, krope_all = ckv, k_rope
    else:
        cos, sin = axial_rope(L)
        q_rope = apply_rope(q_rope, cos[:, None, :], sin[:, None, :])
        krope_all = jnp.concatenate([ctx_krope, apply_rope(k_rope, cos, sin)], axis=1)
        ckv_all = jnp.concatenate([ctx_ckv, ckv], axis=1)
    lk = ckv_all.shape[1]
    k_nope = (ckv_all @ w_uk).reshape(b, lk, MLA_HEADS, QK_NOPE)
    v = (ckv_all @ w_uv).reshape(b, lk, MLA_HEADS, V_HEAD)
    attn = mla_attend(q_nope, q_rope, k_nope, krope_all, v)
    ga, gb = jnp.split(glu, 2, axis=-1)
    cv = ga * jax.nn.sigmoid(gb)
    cv = jax.nn.silu(layernorm(dwconv(cv, conv_w, conv_b), ln_g, ln_b))
    mix = jnp.concatenate([attn, cv], axis=-1) * jax.nn.silu(gate)
    return mix @ w_out, ckv, k_rope


def ssd_scan(x, dt, a_neg, bm, cm, init):
    b, L, G, E, P = x.shape
    N = bm.shape[-1]
    nc, Q = L // SSD_CHUNK, SSD_CHUNK
    f32 = jnp.float32
    xf = x.astype(f32).reshape(b, nc, Q, G, E, P)
    bf = bm.astype(f32).reshape(b, nc, Q, G, N)
    cf = cm.astype(f32).reshape(b, nc, Q, G, N)
    dtc = dt.reshape(b, nc, Q, G, E)
    a_cum = jnp.cumsum(dtc * a_neg.reshape(G, E), axis=2)
    xdt = xf * dtc[..., None]
    seg = a_cum[:, :, :, None] - a_cum[:, :, None, :]
    causal = jnp.tril(jnp.ones((Q, Q), dtype=bool))[:, :, None, None]
    decay = jnp.exp(jnp.where(causal, seg, -jnp.inf))
    cb = jnp.einsum('bclgn,bcsgn->bclsg', cf, bf)
    y_diag = jnp.einsum('bclsg,bclsge,bcsgep->bclgep', cb, decay, xdt)
    decay_end = jnp.exp(a_cum[:, :, -1:] - a_cum)
    chunk_states = jnp.einsum('bcsgn,bcsge,bcsgep->bcgepn', bf, decay_end, xdt)
    chunk_decay = jnp.exp(a_cum[:, :, -1])

    def step(s, inp):
        st, dec = inp
        return s * dec[..., None, None] + st, s

    s0 = init.astype(f32).reshape(b, G, E, P, N)
    final, prev = lax.scan(step, s0, (chunk_states.swapaxes(0, 1), chunk_decay.swapaxes(0, 1)))
    prev = prev.swapaxes(0, 1)
    y_off = jnp.einsum('bclgn,bcgepn,bclge->bclgep', cf, prev, jnp.exp(a_cum))
    y = (y_diag + y_off).reshape(b, L, G, E, P)
    return y.astype(x.dtype), final.reshape(b, G * E, P, N)


def layer_c(h, w_in, conv_w, conv_b, dt_bias, a_log, d_skip, g_norm, w_out, init_state=None):
    b, L, _ = h.shape
    u = h @ w_in
    z, xbc, dt = jnp.split(u, [SSD_INNER, SSD_INNER + SSD_CONV_CH], axis=-1)
    xbc = jax.nn.silu(dwconv(xbc, conv_w, conv_b))
    xs, bm, cm = jnp.split(xbc, [SSD_INNER, SSD_INNER + SSD_GROUPS * SSD_STATE], axis=-1)
    xs = xs.reshape(b, L, SSD_GROUPS, SSD_HPG, SSD_HEAD_DIM)
    bm = bm.reshape(b, L, SSD_GROUPS, SSD_STATE)
    cm = cm.reshape(b, L, SSD_GROUPS, SSD_STATE)
    dt = jax.nn.softplus(dt.astype(jnp.float32).reshape(b, L, 2, SSD_HEADS) + dt_bias.astype(jnp.float32))
    a_neg = -jnp.exp(a_log.astype(jnp.float32))
    if init_state is None:
        init_state = jnp.zeros((b, 2, SSD_HEADS, SSD_HEAD_DIM, SSD_STATE), jnp.float32)
    y_f, s_f = ssd_scan(xs, dt[:, :, 0], a_neg[0], bm, cm, init_state[:, 0])
    y_b, s_b = ssd_scan(jnp.flip(xs, 1), jnp.flip(dt[:, :, 1], 1), a_neg[1],
                        jnp.flip(bm, 1), jnp.flip(cm, 1), init_state[:, 1])
    y = y_f + jnp.flip(y_b, 1) + d_skip.reshape(SSD_GROUPS, SSD_HPG, 1) * xs
    y = rmsnorm(y.reshape(b, L, SSD_INNER) * jax.nn.silu(z), g_norm)
    return y @ w_out, jnp.stack([s_f, s_b], axis=1)


def setup_inputs(seed: int = 0) -> dict:
    key = jax.random.key(seed)
    ks = iter(jax.random.split(key, 40))

    def nrm(shape, s):
        return s * jax.random.normal(next(ks), shape, jnp.float32)

    def gain(shape):
        return 1.0 + nrm(shape, 0.05)

    D = D_MODEL
    dt0 = jnp.exp(jax.random.uniform(next(ks), (N_C_LAYERS, 2, SSD_HEADS), jnp.float32,
                                     math.log(1e-3), math.log(1e-1)))
    dt_bias = dt0 + jnp.log(-jnp.expm1(-dt0))
    a_log = jnp.log(jax.random.uniform(next(ks), (N_C_LAYERS, 2, SSD_HEADS), jnp.float32, 1.0, 16.0))
    return {
        'x_prompt': nrm((BATCH, SEQ, D), 1.0),
        'x_sample': nrm((DEC_BATCH, DEC_SEQ, D), 1.0),
        'cache_ckv': nrm((DEC_BATCH, N_A_LAYERS, PAST_LEN, KV_RANK), 1.0),
        'cache_krope': nrm((DEC_BATCH, N_A_LAYERS, PAST_LEN, QK_ROPE), 1.0),
        'state_ssd': nrm((DEC_BATCH, N_C_LAYERS, 2, SSD_HEADS, SSD_HEAD_DIM, SSD_STATE), 0.5),
        'c': nrm((DEC_BATCH, D), 1.0),
        'c_ctx': nrm((D,), 1.0),
        'w_mod': nrm((DEPTH, D, 3 * D), 0.5 * D ** -0.5),
        'b_mod': nrm((DEPTH, 3 * D), 0.01),
        'g_pre': gain((DEPTH, D)),
        'g_final': gain((D,)),
        'a_w_in': nrm((N_A_LAYERS, D, A_IN), D ** -0.5),
        'a_g_q': gain((N_A_LAYERS, Q_RANK)),
        'a_g_kv': gain((N_A_LAYERS, KV_RANK)),
        'a_w_uq': nrm((N_A_LAYERS, Q_RANK, MLA_HEADS * (QK_NOPE + QK_ROPE)), Q_RANK ** -0.5),
        'a_w_uk': nrm((N_A_LAYERS, KV_RANK, MLA_HEADS * QK_NOPE), KV_RANK ** -0.5),
        'a_w_uv': nrm((N_A_LAYERS, KV_RANK, MLA_HEADS * V_HEAD), KV_RANK ** -0.5),
        'a_conv_w': nrm((N_A_LAYERS, CONV_K, CONV_WIDTH), CONV_K ** -0.5),
        'a_conv_b': nrm((N_A_LAYERS, CONV_WIDTH), 0.01),
        'a_ln_g': gain((N_A_LAYERS, CONV_WIDTH)),
        'a_ln_b': nrm((N_A_LAYERS, CONV_WIDTH), 0.01),
        'a_w_out': nrm((N_A_LAYERS, A_MIX, D), A_MIX ** -0.5),
        'c_w_in': nrm((N_C_LAYERS, D, C_IN), D ** -0.5),
        'c_conv_w': nrm((N_C_LAYERS, SSD_CONV_K, SSD_CONV_CH), SSD_CONV_K ** -0.5),
        'c_conv_b': nrm((N_C_LAYERS, SSD_CONV_CH), 0.01),
        'c_dt_bias': dt_bias,
        'c_a_log': a_log,
        'c_d': gain((N_C_LAYERS, SSD_HEADS)),
        'c_g_norm': gain((N_C_LAYERS, SSD_INNER)),
        'c_w_out': nrm((N_C_LAYERS, SSD_INNER, D), SSD_INNER ** -0.5),
    }


def reference(x_prompt, x_sample, cache_ckv, cache_krope, state_ssd, c, c_ctx,
              w_mod, b_mod, g_pre, g_final,
              a_w_in, a_g_q, a_g_kv, a_w_uq, a_w_uk, a_w_uv, a_conv_w, a_conv_b, a_ln_g, a_ln_b, a_w_out,
              c_w_in, c_conv_w, c_conv_b, c_dt_bias, c_a_log, c_d, c_g_norm, c_w_out):
    xp, xs = x_prompt, x_sample
    new_ckv, new_krope, new_ssd = [], [], []
    for layer in range(DEPTH):
        i = layer // 2
        sh_p, sc_p, gt_p = modulation(c_ctx, w_mod[layer], b_mod[layer])
        sh_s, sc_s, gt_s = modulation(c, w_mod[layer], b_mod[layer])
        hp = rmsnorm(xp, g_pre[layer]) * (1.0 + sc_p) + sh_p
        hs = rmsnorm(xs, g_pre[layer]) * (1.0 + sc_s) + sh_s
        if layer % 2 == 0:
            wa = (a_w_in[i], a_g_q[i], a_g_kv[i], a_w_uq[i], a_w_uk[i], a_w_uv[i],
                  a_conv_w[i], a_conv_b[i], a_ln_g[i], a_ln_b[i], a_w_out[i])
            out_p, ckv_p, krope_p = layer_a(hp, *wa)
            out_s, _, _ = layer_a(hs, *wa, ctx_ckv=cache_ckv[:, i], ctx_krope=cache_krope[:, i])
            new_ckv.append(ckv_p)
            new_krope.append(krope_p)
        else:
            wc = (c_w_in[i], c_conv_w[i], c_conv_b[i], c_dt_bias[i], c_a_log[i], c_d[i], c_g_norm[i], c_w_out[i])
            out_p, st_p = layer_c(hp, *wc)
            out_s, _ = layer_c(hs, *wc, init_state=state_ssd[:, i])
            new_ssd.append(st_p)
        xp = xp + gt_p * out_p
        xs = xs + gt_s * out_s
    y_prompt = rmsnorm(xp, g_final)
    y_sample = rmsnorm(xs, g_final)
    new_cache_ckv = jnp.stack(new_ckv, axis=1)
    new_cache_krope = jnp.stack(new_krope, axis=1)
    new_state_ssd = jnp.stack(new_ssd, axis=1)
    return (y_prompt, y_sample, new_cache_ckv, new_cache_krope, new_state_ssd)
```

```python
import functools
import math

import jax
import jax.numpy as jnp
from jax import lax
from jax.experimental import pallas as pl
from jax.experimental.pallas import tpu as pltpu

F32 = jnp.float32
BF16 = jnp.bfloat16

D_MODEL = 1024
DEPTH = 2
GRID_W = 64
EPS = 1e-6

MLA_HEADS = 8
Q_RANK = 256
KV_RANK = 128
QK_NOPE = 64
QK_ROPE = 32
V_HEAD = 64
MLA_WIDTH = MLA_HEADS * V_HEAD
ATTN_SCALE = (QK_NOPE + QK_ROPE) ** -0.5
ROPE_THETA = 10000.0
HEAD_SLAB = 128

CONV_WIDTH = 512
CONV_K = 31
A_GLU0 = 640
A_GATE0 = A_GLU0 + 2 * CONV_WIDTH
A_COLS = A_GATE0 + MLA_WIDTH + CONV_WIDTH

SSD_INNER = 2048
SSD_HEAD_DIM = 64
SSD_HEADS = 32
SSD_GROUPS = 4
SSD_STATE = 128
SSD_CONV_K = 5
SSD_CHUNK = 128
SSD_CONV_CH = SSD_INNER + 2 * SSD_GROUPS * SSD_STATE
C_MAIN = SSD_INNER + SSD_CONV_CH
C_TILE = 1024
ROW_TILE = 1024
A_ROW_TILE = 512
CONV_ROWS = 32

VMEM_LIMIT = 56 * 1024 * 1024
HI = lax.Precision.HIGHEST
NT = (((1,), (1,)), ((), ()))
TN = (((0,), (0,)), ((), ()))


def _dot(a, b):
    return jnp.dot(a, b, preferred_element_type=F32)


def _sigmoid(x):
    return 1.0 / (1.0 + jnp.exp(-x))


def _silu(x):
    return x * _sigmoid(x)


def _rms(x, g):
    return x * lax.rsqrt(jnp.mean(x * x, axis=-1, keepdims=True) + EPS) * g


def _params(n_axes):
    return pltpu.CompilerParams(dimension_semantics=("arbitrary",) * n_axes,
                                vmem_limit_bytes=VMEM_LIMIT)


def _full(shape):
    nd = len(shape)
    return pl.BlockSpec(shape, lambda *_: (0,) * nd)


def _mod_kernel(cond_ref, w_ref, b_ref, o_ref):
    s = _silu(cond_ref[...])
    o_ref[0] = jnp.dot(s, w_ref[0], precision=HI, preferred_element_type=F32) + b_ref[0]


def _mod_call(cond, w_mod, b_mod):
    nrow = cond.shape[0]
    tn = 1024
    return pl.pallas_call(
        _mod_kernel,
        grid=(DEPTH, 3 * D_MODEL // tn),
        in_specs=[pl.BlockSpec((nrow, D_MODEL), lambda l, j: (0, 0)),
                  pl.BlockSpec((1, D_MODEL, tn), lambda l, j: (l, 0, j)),
                  pl.BlockSpec((1, 1, tn), lambda l, j: (l, 0, j))],
        out_specs=pl.BlockSpec((1, nrow, tn), lambda l, j: (l, 0, j)),
        out_shape=jax.ShapeDtypeStruct((DEPTH, nrow, 3 * D_MODEL), F32),
        compiler_params=_params(2),
        name="modulation",
    )(cond, w_mod, b_mod.reshape(DEPTH, 1, 3 * D_MODEL))


def _a_in_kernel(*refs, rope):
    if rope:
        (x_ref, mod_ref, gpre_ref, w_ref, gq_ref, gkv_ref, w1_ref, w2_ref, cos_ref, sin_ref,
         q_ref, kx_ref, cv_ref, sg_ref) = refs
    else:
        (x_ref, mod_ref, gpre_ref, w_ref, gq_ref, gkv_ref, w1_ref,
         q_ref, kx_ref, cv_ref, sg_ref, ckv_ref, kr_ref) = refs
    x = x_ref[...]
    h = _rms(x, gpre_ref[...]) * (1.0 + mod_ref[0, 1:2, :]) + mod_ref[0, 0:1, :]
    hb = h.astype(BF16)

    u0 = _dot(hb, w_ref[:, 0:A_GLU0])
    qn = _rms(u0[:, 0:Q_RANK], gq_ref[...]).astype(BF16)
    ckv = _rms(u0[:, Q_RANK:Q_RANK + KV_RANK], gkv_ref[...])
    kr = u0[:, 384:512]
    qf = _dot(qn, w1_ref[...])
    if rope:
        cos = cos_ref[...]
        sin = sin_ref[...]
        qs = _dot(qn, w2_ref[...])
        for hd in range(MLA_HEADS):
            sl = slice(hd * HEAD_SLAB, (hd + 1) * HEAD_SLAB)
            q_ref[:, sl] = (qf[:, sl] * cos + qs[:, sl] * sin).astype(BF16)
        kr = kr * cos + u0[:, 512:640] * sin
    else:
        q_ref[...] = qf.astype(BF16)
        ckv_ref[...] = ckv
        kr_ref[...] = kr[:, 0:QK_ROPE]
    kx_ref[:, 0:KV_RANK] = ckv.astype(BF16)
    kx_ref[:, KV_RANK:2 * KV_RANK] = kr.astype(BF16)

    glu = _dot(hb, w_ref[:, A_GLU0:A_GATE0])
    cv_ref[...] = glu[:, 0:CONV_WIDTH] * _sigmoid(glu[:, CONV_WIDTH:])
    sg_ref[...] = _silu(_dot(hb, w_ref[:, A_GATE0:A_COLS]))


def _a_in_call(x2, mods, mod_row, seq_len, g_pre, w_in, g_q, g_kv, w1, w2, cos_t, sin_t):
    rows = x2.shape[0]
    tr = A_ROW_TILE
    rope = w2 is not None
    per_seq = max(seq_len // tr, 1)
    if mod_row is None:
        mod_map = lambda i: (i // per_seq, 0, 0)
    else:
        mod_map = lambda i: (mod_row, 0, 0)
    row = lambda i: (i, 0)
    in_specs = [pl.BlockSpec((tr, D_MODEL), row),
                pl.BlockSpec((1, 3, D_MODEL), mod_map),
                _full(g_pre.shape), _full(w_in.shape), _full(g_q.shape), _full(g_kv.shape),
                _full(w1.shape)]
    args = [x2, mods, g_pre, w_in, g_q, g_kv, w1]
    out_specs = [pl.BlockSpec((tr, MLA_HEADS * HEAD_SLAB), row),
                 pl.BlockSpec((tr, 2 * KV_RANK), row),
                 pl.BlockSpec((tr, CONV_WIDTH), row),
                 pl.BlockSpec((tr, MLA_WIDTH + CONV_WIDTH), row)]
    out_shape = [jax.ShapeDtypeStruct((rows, MLA_HEADS * HEAD_SLAB), BF16),
                 jax.ShapeDtypeStruct((rows, 2 * KV_RANK), BF16),
                 jax.ShapeDtypeStruct((rows, CONV_WIDTH), F32),
                 jax.ShapeDtypeStruct((rows, MLA_WIDTH + CONV_WIDTH), F32)]
    if rope:
        in_specs += [_full(w2.shape),
                     pl.BlockSpec((tr, HEAD_SLAB), lambda i: (i % per_seq, 0)),
                     pl.BlockSpec((tr, HEAD_SLAB), lambda i: (i % per_seq, 0))]
        args += [w2, cos_t, sin_t]
    else:
        out_specs += [pl.BlockSpec((tr, KV_RANK), row), pl.BlockSpec((tr, QK_ROPE), row)]
        out_shape += [jax.ShapeDtypeStruct((rows, KV_RANK), F32),
                      jax.ShapeDtypeStruct((rows, QK_ROPE), F32)]
    return pl.pallas_call(
        functools.partial(_a_in_kernel, rope=rope),
        grid=(rows // tr,),
        in_specs=in_specs, out_specs=out_specs, out_shape=out_shape,
        compiler_params=_params(1),
        name="a_in_rope" if rope else "a_in",
    )(*args)


def _attn_kernel(*refs, n_ctx, seq_len):
    if n_ctx:
        q_ref, kxn_ref, kxc_ref, wk_ref, wvt_ref, o_ref, k_scr, vt_scr, ot_scr = refs
    else:
        q_ref, kxn_ref, wk_ref, wvt_ref, o_ref, k_scr, vt_scr, ot_scr = refs

    @pl.when(pl.program_id(1) == 0)
    def _():
        def fill(kx, lo, n):
            k_scr[lo:lo + n, :] = _dot(kx, wk_ref[...]).astype(BF16)
            vt_scr[:, lo:lo + n] = lax.dot_general(
                wvt_ref[...], kx[:, 0:KV_RANK], NT, preferred_element_type=F32).astype(BF16)
        if n_ctx:
            fill(kxc_ref[0], 0, n_ctx)
        fill(kxn_ref[0], n_ctx, seq_len)

    for hd in range(MLA_HEADS):
        sl = slice(hd * HEAD_SLAB, (hd + 1) * HEAD_SLAB)
        st = lax.dot_general(k_scr[:, sl], q_ref[0, :, sl], NT, preferred_element_type=F32)
        m = jnp.max(st, axis=0, keepdims=True)
        p = jnp.exp((st - m) * ATTN_SCALE)
        den = jnp.sum(p, axis=0, keepdims=True)
        ot = _dot(vt_scr[hd * V_HEAD:(hd + 1) * V_HEAD, :], p.astype(BF16))
        ot_scr[hd * V_HEAD:(hd + 1) * V_HEAD, :] = ot / den
    o_ref[0] = ot_scr[...].T


def _attn_call(q3, kx_new, kx_ctx, w_k, w_vt):
    nb, seq_len, _ = q3.shape
    n_ctx = 0 if kx_ctx is None else kx_ctx.shape[1]
    tq = min(seq_len, 256)
    lk = n_ctx + seq_len
    in_specs = [pl.BlockSpec((1, tq, MLA_HEADS * HEAD_SLAB), lambda b, i: (b, i, 0)),
                pl.BlockSpec((1, seq_len, 2 * KV_RANK), lambda b, i: (b, 0, 0))]
    args = [q3, kx_new]
    if n_ctx:
        in_specs.append(pl.BlockSpec((1, n_ctx, 2 * KV_RANK), lambda b, i: (b, 0, 0)))
        args.append(kx_ctx)
    in_specs += [_full(w_k.shape), _full(w_vt.shape)]
    args += [w_k, w_vt]
    return pl.pallas_call(
        functools.partial(_attn_kernel, n_ctx=n_ctx, seq_len=seq_len),
        grid=(nb, seq_len // tq),
        in_specs=in_specs,
        out_specs=pl.BlockSpec((1, tq, MLA_WIDTH), lambda b, i: (b, i, 0)),
        out_shape=jax.ShapeDtypeStruct((nb, seq_len, MLA_WIDTH), F32),
        scratch_shapes=[pltpu.VMEM((lk, MLA_HEADS * HEAD_SLAB), BF16),
                        pltpu.VMEM((MLA_WIDTH, lk), BF16),
                        pltpu.VMEM((MLA_WIDTH, tq), F32)],
        compiler_params=_params(2),
        name="attn_ctx" if n_ctx else "attn",
    )(*args)


def _a_out_kernel(attn_ref, cv_ref, sg_ref, x_ref, mod_ref, cw_ref, cb_ref, lng_ref, lnb_ref, wout_ref,
                  o_ref, pad_scr, win_scr, cvn_scr, *, seq_len):
    halo = 16
    zeros = jnp.zeros((halo, CONV_WIDTH), F32)
    pad_scr[0:halo, :] = zeros
    pad_scr[halo + seq_len:2 * halo + seq_len, :] = zeros
    pad_scr[halo:halo + seq_len, :] = cv_ref[0]

    def chunk(r, carry):
        base = pl.multiple_of(r * CONV_ROWS, CONV_ROWS)
        win_scr[...] = pad_scr[pl.ds(base, CONV_ROWS + 2 * halo), :]
        acc = jnp.zeros((CONV_ROWS, CONV_WIDTH), F32)
        for k in range(CONV_K):
            lo = halo - CONV_K // 2 + k
            acc = acc + win_scr[lo:lo + CONV_ROWS, :] * cw_ref[k:k + 1, :]
        acc = acc + cb_ref[...]
        mu = jnp.mean(acc, axis=-1, keepdims=True)
        xc = acc - mu
        y = xc * lax.rsqrt(jnp.mean(xc * xc, axis=-1, keepdims=True) + EPS) * lng_ref[...] + lnb_ref[...]
        cvn_scr[pl.ds(base, CONV_ROWS), :] = _silu(y)
        return carry

    lax.fori_loop(0, seq_len // CONV_ROWS, chunk, 0)
    sg = sg_ref[0]
    mix = jnp.concatenate([attn_ref[0] * sg[:, 0:MLA_WIDTH], cvn_scr[...] * sg[:, MLA_WIDTH:]], axis=1)
    out = _dot(mix.astype(BF16), wout_ref[...])
    o_ref[0] = x_ref[0] + mod_ref[0, 2:3, :] * out


def _a_out_call(attn3, cv3, sg3, x3, mods, mod_row, conv_w, conv_b, ln_g, ln_b, w_out):
    nb, seq_len, _ = x3.shape
    blk = lambda c: pl.BlockSpec((1, seq_len, c), lambda b: (b, 0, 0))
    mod_map = (lambda b: (b, 0, 0)) if mod_row is None else (lambda b: (mod_row, 0, 0))
    return pl.pallas_call(
        functools.partial(_a_out_kernel, seq_len=seq_len),
        grid=(nb,),
        in_specs=[blk(MLA_WIDTH), blk(CONV_WIDTH), blk(MLA_WIDTH + CONV_WIDTH), blk(D_MODEL),
                  pl.BlockSpec((1, 3, D_MODEL), mod_map),
                  _full(conv_w.shape), _full(conv_b.shape), _full(ln_g.shape), _full(ln_b.shape),
                  _full(w_out.shape)],
        out_specs=blk(D_MODEL),
        out_shape=jax.ShapeDtypeStruct((nb, seq_len, D_MODEL), F32),
        scratch_shapes=[pltpu.VMEM((seq_len + 32, CONV_WIDTH), F32),
                        pltpu.VMEM((CONV_ROWS + 32, CONV_WIDTH), F32),
                        pltpu.VMEM((seq_len, CONV_WIDTH), F32)],
        compiler_params=_params(1),
        name="a_out",
    )(attn3, cv3, sg3, x3, mods, conv_w, conv_b, ln_g, ln_b, w_out)


def _softplus(x):
    return jnp.maximum(x, 0.0) + jnp.log1p(jnp.exp(-jnp.abs(x)))


def _c_in_kernel(x_ref, mod_ref, gpre_ref, w_ref, wdt_ref, dtb_ref, cw_ref, cb_ref,
                 u_ref, dt_ref, h_scr, pad_scr, win_scr, *, seq_len, n_seq):
    j = pl.program_id(1)
    gap = 8
    stride = seq_len + gap

    @pl.when(j == 0)
    def _():
        h = _rms(x_ref[...], gpre_ref[...]) * (1.0 + mod_ref[0, 1:2, :]) + mod_ref[0, 0:1, :]
        h_scr[...] = h.astype(BF16)
        dt_ref[...] = _softplus(_dot(h_scr[...], wdt_ref[...]) + dtb_ref[...])

    u = _dot(h_scr[...], w_ref[...])

    @pl.when(j < SSD_INNER // C_TILE)
    def _():
        u_ref[...] = u

    @pl.when(j >= SSD_INNER // C_TILE)
    def _():
        zeros = jnp.zeros((gap, C_TILE), F32)
        pad_scr[0:gap, :] = zeros
        for s in range(n_seq):
            lo = gap + s * stride
            pad_scr[lo:lo + seq_len, :] = u[s * seq_len:(s + 1) * seq_len, :]
            pad_scr[lo + seq_len:lo + stride, :] = zeros

        def chunk(r, carry):
            base = pl.multiple_of(r * CONV_ROWS, CONV_ROWS)
            for s in range(n_seq):
                win_scr[...] = pad_scr[pl.ds(base + s * stride, CONV_ROWS + 2 * gap), :]
                acc = jnp.zeros((CONV_ROWS, C_TILE), F32)
                for k in range(SSD_CONV_K):
                    lo = gap - SSD_CONV_K // 2 + k
                    acc = acc + win_scr[lo:lo + CONV_ROWS, :] * cw_ref[k:k + 1, :]
                u_ref[pl.ds(base + s * seq_len, CONV_ROWS), :] = _silu(acc + cb_ref[...])
            return carry

        lax.fori_loop(0, seq_len // CONV_ROWS, chunk, 0)


def _c_in_call(x2, mods, mod_row, seq_len, g_pre, w_main, w_dt, dt_b, conv_w, conv_b):
    rows = x2.shape[0]
    tr = ROW_TILE
    n_seq = tr // seq_len
    n_z = SSD_INNER // C_TILE
    mod_map = (lambda i, j: (i, 0, 0)) if mod_row is None else (lambda i, j: (mod_row, 0, 0))
    conv_map = lambda i, j: (0, jnp.maximum(j - n_z, 0))
    return pl.pallas_call(
        functools.partial(_c_in_kernel, seq_len=seq_len, n_seq=n_seq),
        grid=(rows // tr, C_MAIN // C_TILE),
        in_specs=[pl.BlockSpec((tr, D_MODEL), lambda i, j: (i, 0)),
                  pl.BlockSpec((1, 3, D_MODEL), mod_map),
                  _full(g_pre.shape),
                  pl.BlockSpec((D_MODEL, C_TILE), lambda i, j: (0, j)),
                  _full(w_dt.shape), _full(dt_b.shape),
                  pl.BlockSpec((SSD_CONV_K, C_TILE), conv_map),
                  pl.BlockSpec((1, C_TILE), conv_map)],
        out_specs=[pl.BlockSpec((tr, C_TILE), lambda i, j: (i, j)),
                   pl.BlockSpec((tr, 2 * HEAD_SLAB), lambda i, j: (i, 0))],
        out_shape=[jax.ShapeDtypeStruct((rows, C_MAIN), F32),
                   jax.ShapeDtypeStruct((rows, 2 * HEAD_SLAB), F32)],
        scratch_shapes=[pltpu.VMEM((tr, D_MODEL), BF16),
                        pltpu.VMEM((8 + n_seq * (seq_len + 8), C_TILE), F32),
                        pltpu.VMEM((CONV_ROWS + 16, C_TILE), F32)],
        compiler_params=_params(2),
        name="c_in",
    )(x2, mods, g_pre, w_main, w_dt, dt_b, conv_w, conv_b)


def _pair_expand(v, e0):
    rows = v.shape[0]
    lane_lo = lax.broadcasted_iota(jnp.int32, (rows, 2 * SSD_HEAD_DIM), 1) < SSD_HEAD_DIM
    a = jnp.broadcast_to(v[:, e0:e0 + 1], (rows, 2 * SSD_HEAD_DIM))
    b = jnp.broadcast_to(v[:, e0 + 1:e0 + 2], (rows, 2 * SSD_HEAD_DIM))
    return jnp.where(lane_lo, a, b)


def _ssd_kernel(*refs, n_chunks, has_init, emit_state):
    refs = list(refs)
    xs_ref, b_ref, c_ref, dt_ref, z_ref, x_ref = refs[:6]
    refs = refs[6:]
    st0_ref = refs.pop(0) if has_init else None
    aneg_ref, dskip_ref, gnorm_ref, wout_ref, mod_ref, gfin_ref, y_ref = refs[:7]
    refs = refs[7:]
    stout_ref = refs.pop(0) if emit_state else None
    st_scr, ybuf_scr, ycur_scr, xw_scr = refs

    q = SSD_CHUNK
    s = pl.program_id(1)
    bwd = s >= n_chunks
    c = jnp.where(bwd, 2 * n_chunks - 1 - s, s)

    def init_state(d):
        if has_init:
            st_scr[...] = st0_ref[0, d].T
        else:
            st_scr[...] = jnp.zeros(st_scr.shape, F32)

    @pl.when(s == 0)
    def _():
        init_state(0)

    @pl.when(s == n_chunks)
    def _():
        init_state(1)

    dt = dt_ref[0]
    a = dt * aneg_ref[0]
    ri = lax.broadcasted_iota(jnp.int32, (q, q), 0)
    ci = lax.broadcasted_iota(jnp.int32, (q, q), 1)
    keep = jnp.where(bwd, ci, ri) >= jnp.where(bwd, ri, ci)
    cum = jnp.dot(keep.astype(F32), a, precision=HI, preferred_element_type=F32)
    total = jnp.sum(a, axis=0, keepdims=True)
    cum_t = cum.T
    dt_t = dt.T
    ecum = jnp.exp(cum)
    wgt = jnp.exp(total - cum) * dt
    cdec = jnp.exp(total)
    lane_lo = lax.broadcasted_iota(jnp.int32, (q, 2 * SSD_HEAD_DIM), 1) < SSD_HEAD_DIM
    neg_inf = jnp.float32(-jnp.inf)

    hpg = SSD_HEADS // SSD_GROUPS
    gw = hpg * SSD_HEAD_DIM
    for g in range(SSD_GROUPS):
        bg = b_ref[0, :, g * SSD_STATE:(g + 1) * SSD_STATE]
        cg = c_ref[0, :, g * SSD_STATE:(g + 1) * SSD_STATE].astype(BF16)
        cb = lax.dot_general(cg, bg.astype(BF16), NT, preferred_element_type=F32)
        stg = st_scr[:, g * gw:(g + 1) * gw]
        yoff = _dot(cg, stg.astype(BF16))
        for jp in range(hpg // 2):
            e0 = g * hpg + 2 * jp
            ls = slice(e0 * SSD_HEAD_DIM, (e0 + 2) * SSD_HEAD_DIM)
            ms = []
            for e in (e0, e0 + 1):
                seg = cum[:, e:e + 1] - cum_t[e:e + 1, :]
                dec = jnp.exp(jnp.where(keep, seg, neg_inf))
                ms.append((cb * dec * dt_t[e:e + 1, :]).astype(BF16))
            xp = xs_ref[0, :, ls]
            xpb = xp.astype(BF16)
            zero = jnp.zeros_like(xpb)
            rhs = jnp.concatenate([jnp.where(lane_lo, xpb, zero), jnp.where(lane_lo, zero, xpb)], axis=0)
            yd = _dot(jnp.concatenate(ms, axis=1), rhs)
            yo = yoff[:, 2 * jp * SSD_HEAD_DIM:(2 * jp + 2) * SSD_HEAD_DIM]
            ycur_scr[:, ls] = yd + yo * _pair_expand(ecum, e0)
            xw_scr[:, ls] = (xp * _pair_expand(wgt, e0)).astype(BF16)
        new = lax.dot_general(bg.T.astype(BF16), xw_scr[:, g * gw:(g + 1) * gw],
                              (((1,), (0,)), ((), ())), preferred_element_type=F32)
        dec_g = jnp.concatenate([_pair_expand(cdec, g * hpg + 2 * jp) for jp in range(hpg // 2)], axis=1)
        st_scr[:, g * gw:(g + 1) * gw] = stg * dec_g + new

    @pl.when(jnp.logical_not(bwd))
    def _():
        ybuf_scr[c] = ycur_scr[...]

    if emit_state:
        @pl.when(s == n_chunks - 1)
        def _():
            stout_ref[0, 0] = st_scr[...].T

        @pl.when(s == 2 * n_chunks - 1)
        def _():
            stout_ref[0, 1] = st_scr[...].T

    @pl.when(bwd)
    def _():
        ytot = ybuf_scr[c] + ycur_scr[...] + dskip_ref[...] * xs_ref[0]
        gated = ytot * _silu(z_ref[0])
        yn = _rms(gated, gnorm_ref[...]).astype(BF16)
        xn = x_ref[0] + mod_ref[0, 2:3, :] * _dot(yn, wout_ref[...])
        y_ref[0] = _rms(xn, gfin_ref[...])


def _ssd_call(u3, dt3, x3, st0, a_neg, d_skip, g_norm, w_out, mods, mod_row, g_final, emit_state):
    nb, seq_len, _ = x3.shape
    q = SSD_CHUNK
    nc = seq_len // q
    has_init = st0 is not None

    def chunk(s):
        return jnp.where(s >= nc, 2 * nc - 1 - s, s)

    def late(s):
        return jnp.where(s >= nc, 2 * nc - 1 - s, nc - 1)

    xcol = SSD_INNER // SSD_INNER
    bcol = 2 * SSD_INNER // (SSD_GROUPS * SSD_STATE)
    mod_map = (lambda b, s: (b, 0, 0)) if mod_row is None else (lambda b, s: (mod_row, 0, 0))
    in_specs = [pl.BlockSpec((1, q, SSD_INNER), lambda b, s: (b, chunk(s), xcol)),
                pl.BlockSpec((1, q, SSD_GROUPS * SSD_STATE), lambda b, s: (b, chunk(s), bcol)),
                pl.BlockSpec((1, q, SSD_GROUPS * SSD_STATE), lambda b, s: (b, chunk(s), bcol + 1)),
                pl.BlockSpec((1, q, HEAD_SLAB), lambda b, s: (b, chunk(s), (s >= nc).astype(jnp.int32))),
                pl.BlockSpec((1, q, SSD_INNER), lambda b, s: (b, late(s), 0)),
                pl.BlockSpec((1, q, D_MODEL), lambda b, s: (b, late(s), 0))]
    args = [u3, u3, u3, dt3, u3, x3]
    if has_init:
        in_specs.append(pl.BlockSpec((1, 2, SSD_INNER, SSD_STATE), lambda b, s: (b, 0, 0, 0)))
        args.append(st0)
    in_specs += [pl.BlockSpec((1, 1, HEAD_SLAB), lambda b, s: ((s >= nc).astype(jnp.int32), 0, 0)),
                 _full(d_skip.shape), _full(g_norm.shape), _full(w_out.shape),
                 pl.BlockSpec((1, 3, D_MODEL), mod_map), _full(g_final.shape)]
    args += [a_neg, d_skip, g_norm, w_out, mods, g_final]
    out_specs = [pl.BlockSpec((1, q, D_MODEL), lambda b, s: (b, late(s), 0))]
    out_shape = [jax.ShapeDtypeStruct((nb, seq_len, D_MODEL), F32)]
    if emit_state:
        out_specs.append(pl.BlockSpec((1, 2, SSD_INNER, SSD_STATE), lambda b, s: (b, 0, 0, 0)))
        out_shape.append(jax.ShapeDtypeStruct((nb, 2, SSD_INNER, SSD_STATE), F32))
    return pl.pallas_call(
        functools.partial(_ssd_kernel, n_chunks=nc, has_init=has_init, emit_state=emit_state),
        grid=(nb, 2 * nc),
        in_specs=in_specs, out_specs=out_specs, out_shape=out_shape,
        scratch_shapes=[pltpu.VMEM((SSD_STATE, SSD_INNER), F32),
                        pltpu.VMEM((nc, q, SSD_INNER), F32),
                        pltpu.VMEM((q, SSD_INNER), F32),
                        pltpu.VMEM((q, SSD_INNER), BF16)],
        compiler_params=_params(2),
        name="ssd_init" if has_init else "ssd",
    )(*args)


def _rope_tables(length):
    rows = length // GRID_W
    row = jnp.repeat(jnp.arange(rows, dtype=F32), GRID_W)
    col = jnp.tile(jnp.arange(GRID_W, dtype=F32), rows)
    n_freq = QK_ROPE // 4
    inv = jnp.power(ROPE_THETA, -jnp.arange(n_freq, dtype=F32) / n_freq)
    ang = jnp.concatenate([row[:, None] * inv, col[:, None] * inv], axis=-1)
    cos, sin = jnp.cos(ang), jnp.sin(ang)
    cos2 = jnp.repeat(cos, 2, axis=-1)
    sin2 = jnp.stack([-sin, sin], axis=-1).reshape(length, QK_ROPE)
    pad = HEAD_SLAB - QK_ROPE
    return (jnp.concatenate([cos2, jnp.ones((length, pad), F32)], axis=-1),
            jnp.concatenate([sin2, jnp.zeros((length, pad), F32)], axis=-1))


def _prep_layer_a(w_in, w_uq, w_uk, w_uv):
    d = w_in.shape[0]
    swap = jnp.arange(QK_ROPE) ^ 1
    kr = w_in[:, 384:416]
    z96 = jnp.zeros((d, HEAD_SLAB - QK_ROPE), F32)
    w_r = jnp.concatenate([w_in[:, 0:384], kr, z96, kr[:, swap], z96, w_in[:, 416:]], axis=1).astype(BF16)
    uq = w_uq.reshape(Q_RANK, MLA_HEADS, QK_NOPE + QK_ROPE)
    nope, rp = uq[..., :QK_NOPE], uq[..., QK_NOPE:]
    z32 = jnp.zeros((Q_RANK, MLA_HEADS, 32), F32)
    w1 = jnp.concatenate([rp, z32, nope], axis=-1).reshape(Q_RANK, MLA_HEADS * HEAD_SLAB).astype(BF16)
    w2 = jnp.concatenate([rp[..., swap], z32, jnp.zeros_like(nope)], axis=-1)
    w2 = w2.reshape(Q_RANK, MLA_HEADS * HEAD_SLAB).astype(BF16)
    uk = w_uk.reshape(KV_RANK, MLA_HEADS, QK_NOPE)
    top = jnp.concatenate([jnp.zeros((KV_RANK, MLA_HEADS, 64), F32), uk], axis=-1)
    eye = jnp.broadcast_to(jnp.eye(QK_ROPE, dtype=F32)[:, None, :], (QK_ROPE, MLA_HEADS, QK_ROPE))
    mid = jnp.concatenate([eye, jnp.zeros((QK_ROPE, MLA_HEADS, HEAD_SLAB - QK_ROPE), F32)], axis=-1)
    bot = jnp.zeros((KV_RANK - QK_ROPE, MLA_HEADS, HEAD_SLAB), F32)
    w_k = jnp.concatenate([top, mid, bot], axis=0).reshape(2 * KV_RANK, MLA_HEADS * HEAD_SLAB).astype(BF16)
    return w_r, w1, w2, w_k, w_uv.T.astype(BF16)


def kernel(x_prompt, x_sample, cache_ckv, cache_krope, state_ssd, c, c_ctx, w_mod, b_mod, g_pre, g_final,
           a_w_in, a_g_q, a_g_kv, a_w_uq, a_w_uk, a_w_uv, a_conv_w, a_conv_b, a_ln_g, a_ln_b, a_w_out,
           c_w_in, c_conv_w, c_conv_b, c_dt_bias, c_a_log, c_d, c_g_norm, c_w_out):
    nbp, lp, d = x_prompt.shape
    nbs, ls, _ = x_sample.shape
    ctx_row = nbs
    cond = jnp.concatenate([c, c_ctx[None, :], jnp.zeros((16 - nbs - 1, d), F32)], axis=0)
    mods_all = _mod_call(cond, w_mod, b_mod).reshape(DEPTH, 16, 3, d)
    row2 = lambda v: v.reshape(1, -1)

    mods = mods_all[0]
    w_r, w1, w2, w_k, w_vt = _prep_layer_a(a_w_in[0], a_w_uq[0], a_w_uk[0], a_w_uv[0])
    cos_t, sin_t = _rope_tables(ls)
    gpre = row2(g_pre[0])
    gq, gkv = row2(a_g_q[0]), row2(a_g_kv[0])
    a_wout = a_w_out[0].astype(BF16)
    conv_args = (a_conv_w[0], row2(a_conv_b[0]), row2(a_ln_g[0]), row2(a_ln_b[0]), a_wout)

    qp, kxp, cvp, sgp, ckv_p, kr_p = _a_in_call(x_prompt.reshape(nbp * lp, d), mods, ctx_row, lp, gpre,
                                                w_r, gq, gkv, w1, None, None, None)
    attn_p = _attn_call(qp.reshape(nbp, lp, -1), kxp.reshape(nbp, lp, -1), None, w_k, w_vt)
    xp1 = _a_out_call(attn_p, cvp.reshape(nbp, lp, -1), sgp.reshape(nbp, lp, -1), x_prompt, mods, ctx_row,
                      *conv_args)

    qs, kxs, cvs, sgs = _a_in_call(x_sample.reshape(nbs * ls, d), mods, None, ls, gpre,
                                   w_r, gq, gkv, w1, w2, cos_t, sin_t)
    n_ctx = cache_ckv.shape[2]
    kx_ctx = jnp.concatenate([cache_ckv[:, 0], cache_krope[:, 0],
                              jnp.zeros((nbs, n_ctx, KV_RANK - QK_ROPE), F32)], axis=-1).astype(BF16)
    attn_s = _attn_call(qs.reshape(nbs, ls, -1), kxs.reshape(nbs, ls, -1), kx_ctx, w_k, w_vt)
    xs1 = _a_out_call(attn_s, cvs.reshape(nbs, ls, -1), sgs.reshape(nbs, ls, -1), x_sample, mods, None,
                      *conv_args)

    mods = mods_all[1]
    gpre = row2(g_pre[1])
    w_in = c_w_in[0]
    w_main = w_in[:, 0:C_MAIN].astype(BF16)
    dtw = w_in[:, C_MAIN:]
    zpad = jnp.zeros((d, HEAD_SLAB - SSD_HEADS), F32)
    w_dt = jnp.concatenate([dtw[:, 0:SSD_HEADS], zpad, dtw[:, SSD_HEADS:], zpad], axis=1).astype(BF16)
    pad32 = jnp.zeros((2, HEAD_SLAB - SSD_HEADS), F32)
    dt_b = jnp.concatenate([c_dt_bias[0], pad32], axis=1).reshape(1, 2 * HEAD_SLAB)
    a_neg = jnp.concatenate([-jnp.exp(c_a_log[0]), pad32], axis=1).reshape(2, 1, HEAD_SLAB)
    d_skip = jnp.repeat(c_d[0], SSD_HEAD_DIM).reshape(1, SSD_INNER)
    gnorm = row2(c_g_norm[0])
    c_wout = c_w_out[0].astype(BF16)
    gfin = row2(g_final)
    cw, cbias = c_conv_w[0], row2(c_conv_b[0])

    up, dtp = _c_in_call(xp1.reshape(nbp * lp, d), mods, ctx_row, lp, gpre, w_main, w_dt, dt_b, cw, cbias)
    y_prompt, st_p = _ssd_call(up.reshape(nbp, lp, -1), dtp.reshape(nbp, lp, -1), xp1, None, a_neg, d_skip,
                               gnorm, c_wout, mods, ctx_row, gfin, True)
    us, dts = _c_in_call(xs1.reshape(nbs * ls, d), mods, None, ls, gpre, w_main, w_dt, dt_b, cw, cbias)
    st0 = state_ssd[:, 0].reshape(nbs, 2, SSD_INNER, SSD_STATE)
    (y_sample,) = _ssd_call(us.reshape(nbs, ls, -1), dts.reshape(nbs, ls, -1), xs1, st0, a_neg, d_skip,
                            gnorm, c_wout, mods, None, gfin, False)

    new_cache_ckv = ckv_p.reshape(nbp, 1, lp, KV_RANK)
    new_cache_krope = kr_p.reshape(nbp, 1, lp, QK_ROPE)
    new_state_ssd = st_p.reshape(nbp, 1, 2, SSD_HEADS, SSD_HEAD_DIM, SSD_STATE)
    return (y_prompt, y_sample, new_cache_ckv, new_cache_krope, new_state_ssd)
```

```python
import functools
import math

import jax
import jax.numpy as jnp
from jax import lax
from jax.experimental import pallas as pl
from jax.experimental.pallas import tpu as pltpu

F32 = jnp.float32
BF16 = jnp.bfloat16

LANES = 128
SUBLANES = 8

D_MODEL = 1024
DEPTH = 2
GRID_W = 64
EPS = 1e-6

MLA_HEADS = 8
Q_RANK = 256
KV_RANK = 128
QK_NOPE = 64
QK_ROPE = 32
V_HEAD = 64
MLA_WIDTH = MLA_HEADS * V_HEAD
ATTN_SCALE = (QK_NOPE + QK_ROPE) ** -0.5
ROPE_THETA = 10000.0
HEAD_SLAB = LANES

CONV_WIDTH = 512
CONV_K = 31
A_GLU0 = 640
A_GATE0 = A_GLU0 + 2 * CONV_WIDTH
A_COLS = A_GATE0 + MLA_WIDTH + CONV_WIDTH

SSD_INNER = 2048
SSD_HEAD_DIM = 64
SSD_HEADS = 32
SSD_GROUPS = 4
SSD_STATE = 128
SSD_CONV_K = 5
SSD_CHUNK = 128
SSD_CONV_CH = SSD_INNER + 2 * SSD_GROUPS * SSD_STATE
C_MAIN = SSD_INNER + SSD_CONV_CH
C_TILE = 1024
ROW_TILE = 1024
A_ROW_TILE = 512

CONV_STRIDE = 4
CONV_BLOCK = SUBLANES * CONV_STRIDE

VMEM_LIMIT = 56 * 1024 * 1024
HI = lax.Precision.HIGHEST
NT = (((1,), (1,)), ((), ()))
LOG2E = math.log2(math.e)


def _dot(a, b):
    return jnp.dot(a, b, preferred_element_type=F32)


def _sigmoid(x):
    return 1.0 / (1.0 + jnp.exp(-x))


def _silu(x):
    return x * _sigmoid(x)


def _rms(x, g):
    return x * lax.rsqrt(jnp.mean(x * x, axis=-1, keepdims=True) + EPS) * g


def _params(n_axes):
    return pltpu.CompilerParams(dimension_semantics=("arbitrary",) * n_axes,
                                vmem_limit_bytes=VMEM_LIMIT)


def _full(shape):
    nd = len(shape)
    return pl.BlockSpec(shape, lambda *_: (0,) * nd)


def _mod_kernel(cond_ref, w_ref, b_ref, o_ref):
    s = _silu(cond_ref[...])
    o_ref[0] = jnp.dot(s, w_ref[0], precision=HI, preferred_element_type=F32) + b_ref[0]


def _mod_call(cond, w_mod, b_mod):
    nrow = cond.shape[0]
    tn = 1024
    return pl.pallas_call(
        _mod_kernel,
        grid=(DEPTH, 3 * D_MODEL // tn),
        in_specs=[pl.BlockSpec((nrow, D_MODEL), lambda l, j: (0, 0)),
                  pl.BlockSpec((1, D_MODEL, tn), lambda l, j: (l, 0, j)),
                  pl.BlockSpec((1, 1, tn), lambda l, j: (l, 0, j))],
        out_specs=pl.BlockSpec((1, nrow, tn), lambda l, j: (l, 0, j)),
        out_shape=jax.ShapeDtypeStruct((DEPTH, nrow, 3 * D_MODEL), F32),
        compiler_params=_params(2),
        name="modulation",
    )(cond, w_mod, b_mod.reshape(DEPTH, 1, 3 * D_MODEL))


def _a_in_kernel(*refs, rope):
    if rope:
        (x_ref, mod_ref, gpre_ref, w_ref, gq_ref, gkv_ref, w1_ref, w2_ref, cos_ref, sin_ref,
         q_ref, kx_ref, cv_ref, sg_ref) = refs
    else:
        (x_ref, mod_ref, gpre_ref, w_ref, gq_ref, gkv_ref, w1_ref,
         q_ref, kx_ref, cv_ref, sg_ref, ckv_ref, kr_ref) = refs
    x = x_ref[...]
    h = _rms(x, gpre_ref[...]) * (1.0 + mod_ref[0, 1:2, :]) + mod_ref[0, 0:1, :]
    hb = h.astype(BF16)

    u0 = _dot(hb, w_ref[:, 0:A_GLU0])
    qn = _rms(u0[:, 0:Q_RANK], gq_ref[...]).astype(BF16)
    ckv = _rms(u0[:, Q_RANK:Q_RANK + KV_RANK], gkv_ref[...])
    kr = u0[:, 384:512]
    qf = _dot(qn, w1_ref[...])
    qscale = ATTN_SCALE * LOG2E
    if rope:
        cos = cos_ref[...]
        sin = sin_ref[...]
        qs = _dot(qn, w2_ref[...])
        for hd in range(MLA_HEADS):
            sl = slice(hd * HEAD_SLAB, (hd + 1) * HEAD_SLAB)
            q_ref[:, sl] = ((qf[:, sl] * cos + qs[:, sl] * sin) * qscale).astype(BF16)
        kr = kr * cos + u0[:, 512:640] * sin
    else:
        q_ref[...] = (qf * qscale).astype(BF16)
        ckv_ref[...] = ckv
        kr_ref[...] = kr[:, 0:QK_ROPE]
    kx_ref[:, 0:KV_RANK] = ckv.astype(BF16)
    kx_ref[:, KV_RANK:2 * KV_RANK] = kr.astype(BF16)

    glu = _dot(hb, w_ref[:, A_GLU0:A_GATE0])
    cv_ref[...] = glu[:, 0:CONV_WIDTH] * _sigmoid(glu[:, CONV_WIDTH:])
    sg_ref[...] = _silu(_dot(hb, w_ref[:, A_GATE0:A_COLS]))


def _a_in_call(x2, mods, mod_row, seq_len, g_pre, w_in, g_q, g_kv, w1, w2, cos_t, sin_t):
    rows = x2.shape[0]
    tr = A_ROW_TILE
    rope = w2 is not None
    per_seq = max(seq_len // tr, 1)
    if mod_row is None:
        mod_map = lambda i: (i // per_seq, 0, 0)
    else:
        mod_map = lambda i: (mod_row, 0, 0)
    row = lambda i: (i, 0)
    in_specs = [pl.BlockSpec((tr, D_MODEL), row),
                pl.BlockSpec((1, 3, D_MODEL), mod_map),
                _full(g_pre.shape), _full(w_in.shape), _full(g_q.shape), _full(g_kv.shape),
                _full(w1.shape)]
    args = [x2, mods, g_pre, w_in, g_q, g_kv, w1]
    out_specs = [pl.BlockSpec((tr, MLA_HEADS * HEAD_SLAB), row),
                 pl.BlockSpec((tr, 2 * KV_RANK), row),
                 pl.BlockSpec((tr, CONV_WIDTH), row),
                 pl.BlockSpec((tr, MLA_WIDTH + CONV_WIDTH), row)]
    out_shape = [jax.ShapeDtypeStruct((rows, MLA_HEADS * HEAD_SLAB), BF16),
                 jax.ShapeDtypeStruct((rows, 2 * KV_RANK), BF16),
                 jax.ShapeDtypeStruct((rows, CONV_WIDTH), F32),
                 jax.ShapeDtypeStruct((rows, MLA_WIDTH + CONV_WIDTH), F32)]
    if rope:
        in_specs += [_full(w2.shape),
                     pl.BlockSpec((tr, HEAD_SLAB), lambda i: (i % per_seq, 0)),
                     pl.BlockSpec((tr, HEAD_SLAB), lambda i: (i % per_seq, 0))]
        args += [w2, cos_t, sin_t]
    else:
        out_specs += [pl.BlockSpec((tr, KV_RANK), row), pl.BlockSpec((tr, QK_ROPE), row)]
        out_shape += [jax.ShapeDtypeStruct((rows, KV_RANK), F32),
                      jax.ShapeDtypeStruct((rows, QK_ROPE), F32)]
    return pl.pallas_call(
        functools.partial(_a_in_kernel, rope=rope),
        grid=(rows // tr,),
        in_specs=in_specs, out_specs=out_specs, out_shape=out_shape,
        compiler_params=_params(1),
        name="a_in_rope" if rope else "a_in",
    )(*args)


def _attn_kernel(*refs, n_ctx, seq_len):
    if n_ctx:
        q_ref, kxn_ref, kxc_ref, wk_ref, wvt_ref, o_ref, k_scr, vt_scr, ot_scr = refs
    else:
        q_ref, kxn_ref, wk_ref, wvt_ref, o_ref, k_scr, vt_scr, ot_scr = refs

    @pl.when(pl.program_id(1) == 0)
    def _():
        def fill(kx, lo, n):
            k_scr[lo:lo + n, :] = _dot(kx, wk_ref[...]).astype(BF16)
            vt_scr[:, lo:lo + n] = lax.dot_general(
                wvt_ref[...], kx[:, 0:KV_RANK], NT, preferred_element_type=F32).astype(BF16)
        if n_ctx:
            fill(kxc_ref[0], 0, n_ctx)
        fill(kxn_ref[0], n_ctx, seq_len)

    for hd in range(MLA_HEADS):
        sl = slice(hd * HEAD_SLAB, (hd + 1) * HEAD_SLAB)
        vrows = slice(hd * V_HEAD, (hd + 1) * V_HEAD)
        st = lax.dot_general(k_scr[:, sl], q_ref[0, :, sl], NT, preferred_element_type=F32)
        p = jnp.exp2(st - jnp.max(st, axis=0, keepdims=True))
        den = jnp.sum(p, axis=0, keepdims=True)
        ot_scr[vrows, :] = _dot(vt_scr[vrows, :], p.astype(BF16)) / den
    o_ref[0] = ot_scr[...].T


def _attn_call(q3, kx_new, kx_ctx, w_k, w_vt):
    nb, seq_len, _ = q3.shape
    n_ctx = 0 if kx_ctx is None else kx_ctx.shape[1]
    tq = min(seq_len, 256)
    lk = n_ctx + seq_len
    in_specs = [pl.BlockSpec((1, tq, MLA_HEADS * HEAD_SLAB), lambda b, i: (b, i, 0)),
                pl.BlockSpec((1, seq_len, 2 * KV_RANK), lambda b, i: (b, 0, 0))]
    args = [q3, kx_new]
    if n_ctx:
        in_specs.append(pl.BlockSpec((1, n_ctx, 2 * KV_RANK), lambda b, i: (b, 0, 0)))
        args.append(kx_ctx)
    in_specs += [_full(w_k.shape), _full(w_vt.shape)]
    args += [w_k, w_vt]
    return pl.pallas_call(
        functools.partial(_attn_kernel, n_ctx=n_ctx, seq_len=seq_len),
        grid=(nb, seq_len // tq),
        in_specs=in_specs,
        out_specs=pl.BlockSpec((1, tq, MLA_WIDTH), lambda b, i: (b, i, 0)),
        out_shape=jax.ShapeDtypeStruct((nb, seq_len, MLA_WIDTH), F32),
        scratch_shapes=[pltpu.VMEM((lk, MLA_HEADS * HEAD_SLAB), BF16),
                        pltpu.VMEM((MLA_WIDTH, lk), BF16),
                        pltpu.VMEM((MLA_WIDTH, tq), F32)],
        compiler_params=_params(2),
        name="attn_ctx" if n_ctx else "attn",
    )(*args)


def _a_out_kernel(attn_ref, cv_ref, sg_ref, x_ref, mod_ref, cw_ref, cb_ref, lng_ref, lnb_ref, wout_ref,
                  o_ref, pad_scr, cvn_scr, *, seq_len):
    halo = 16
    n_slab = CONV_WIDTH // LANES
    zeros = jnp.zeros((halo, LANES), F32)
    for sl in range(n_slab):
        pad_scr[sl, 0:halo, :] = zeros
        pad_scr[sl, halo + seq_len:2 * halo + seq_len, :] = zeros
        pad_scr[sl, halo:halo + seq_len, :] = cv_ref[0, :, sl * LANES:(sl + 1) * LANES]

    def block(r, carry):
        base = r * CONV_BLOCK
        for sl in range(n_slab):
            lanes = slice(sl * LANES, (sl + 1) * LANES)
            accs = [None] * CONV_STRIDE
            for k in range(CONV_K):
                w = jnp.broadcast_to(cw_ref[k:k + 1, lanes], (SUBLANES, LANES))
                for ph in range(CONV_STRIDE):
                    start = base + (halo - CONV_K // 2 + ph + k)
                    v = pad_scr[sl, pl.ds(start, SUBLANES, stride=CONV_STRIDE), :] * w
                    accs[ph] = v if accs[ph] is None else accs[ph] + v
            for ph in range(CONV_STRIDE):
                cvn_scr[sl, pl.ds(base + ph, SUBLANES, stride=CONV_STRIDE), :] = accs[ph]
        return carry

    lax.fori_loop(0, seq_len // CONV_BLOCK, block, 0)
    conv = jnp.concatenate([cvn_scr[sl] for sl in range(n_slab)], axis=1) + cb_ref[...]
    mu = jnp.mean(conv, axis=-1, keepdims=True)
    cen = conv - mu
    ln = cen * lax.rsqrt(jnp.mean(cen * cen, axis=-1, keepdims=True) + EPS) * lng_ref[...] + lnb_ref[...]
    sg = sg_ref[0]
    mix = jnp.concatenate([attn_ref[0] * sg[:, 0:MLA_WIDTH], _silu(ln) * sg[:, MLA_WIDTH:]], axis=1)
    out = _dot(mix.astype(BF16), wout_ref[...])
    o_ref[0] = x_ref[0] + mod_ref[0, 2:3, :] * out


def _a_out_call(attn3, cv3, sg3, x3, mods, mod_row, conv_w, conv_b, ln_g, ln_b, w_out):
    nb, seq_len, _ = x3.shape
    blk = lambda c: pl.BlockSpec((1, seq_len, c), lambda b: (b, 0, 0))
    mod_map = (lambda b: (b, 0, 0)) if mod_row is None else (lambda b: (mod_row, 0, 0))
    n_slab = CONV_WIDTH // LANES
    return pl.pallas_call(
        functools.partial(_a_out_kernel, seq_len=seq_len),
        grid=(nb,),
        in_specs=[blk(MLA_WIDTH), blk(CONV_WIDTH), blk(MLA_WIDTH + CONV_WIDTH), blk(D_MODEL),
                  pl.BlockSpec((1, 3, D_MODEL), mod_map),
                  _full(conv_w.shape), _full(conv_b.shape), _full(ln_g.shape), _full(ln_b.shape),
                  _full(w_out.shape)],
        out_specs=blk(D_MODEL),
        out_shape=jax.ShapeDtypeStruct((nb, seq_len, D_MODEL), F32),
        scratch_shapes=[pltpu.VMEM((n_slab, seq_len + 32, LANES), F32),
                        pltpu.VMEM((n_slab, seq_len, LANES), F32)],
        compiler_params=_params(1),
        name="a_out",
    )(attn3, cv3, sg3, x3, mods, conv_w, conv_b, ln_g, ln_b, w_out)


def _softplus(x):
    return jnp.maximum(x, 0.0) + jnp.log1p(jnp.exp(-jnp.abs(x)))


def _c_in_kernel(x_ref, mod_ref, gpre_ref, w_ref, wdt_ref, dtb_ref, cw_ref, cb_ref,
                 u_ref, dt_ref, h_scr, pad_scr, *, seq_len, n_seq):
    j = pl.program_id(1)
    gap = SUBLANES
    pitch = seq_len + gap
    n_slab = C_TILE // LANES
    n_z = SSD_INNER // C_TILE

    @pl.when(j == 0)
    def _():
        h = _rms(x_ref[...], gpre_ref[...]) * (1.0 + mod_ref[0, 1:2, :]) + mod_ref[0, 0:1, :]
        h_scr[...] = h.astype(BF16)
        dt_ref[...] = _softplus(_dot(h_scr[...], wdt_ref[...]) + dtb_ref[...])

    u = _dot(h_scr[...], w_ref[...])

    @pl.when(j < n_z)
    def _():
        for sl in range(n_slab):
            u_ref[sl] = u[:, sl * LANES:(sl + 1) * LANES]

    @pl.when(j >= n_z)
    def _():
        zeros = jnp.zeros((gap, LANES), F32)
        for sl in range(n_slab):
            pad_scr[sl, 0:gap, :] = zeros
            for s in range(n_seq):
                lo = gap + s * pitch
                pad_scr[sl, lo:lo + seq_len, :] = u[s * seq_len:(s + 1) * seq_len, sl * LANES:(sl + 1) * LANES]
                pad_scr[sl, lo + seq_len:lo + pitch, :] = zeros

        def block(r, carry):
            base = r * CONV_BLOCK
            for sl in range(n_slab):
                lanes = slice(sl * LANES, (sl + 1) * LANES)
                ws = [jnp.broadcast_to(cw_ref[k:k + 1, lanes], (SUBLANES, LANES)) for k in range(SSD_CONV_K)]
                bias = jnp.broadcast_to(cb_ref[:, lanes], (SUBLANES, LANES))
                for s in range(n_seq):
                    for ph in range(CONV_STRIDE):
                        acc = bias
                        for k in range(SSD_CONV_K):
                            start = base + (gap + s * pitch - SSD_CONV_K // 2 + ph + k)
                            acc = acc + pad_scr[sl, pl.ds(start, SUBLANES, stride=CONV_STRIDE), :] * ws[k]
                        u_ref[sl, pl.ds(base + (s * seq_len + ph), SUBLANES, stride=CONV_STRIDE), :] = _silu(acc)
            return carry

        lax.fori_loop(0, seq_len // CONV_BLOCK, block, 0)


def _c_in_call(x2, mods, mod_row, seq_len, g_pre, w_main, w_dt, dt_b, conv_w, conv_b):
    rows = x2.shape[0]
    tr = ROW_TILE
    n_seq = tr // seq_len
    n_z = SSD_INNER // C_TILE
    n_slab = C_TILE // LANES
    mod_map = (lambda i, j: (i, 0, 0)) if mod_row is None else (lambda i, j: (mod_row, 0, 0))
    conv_map = lambda i, j: (0, jnp.maximum(j - n_z, 0))
    return pl.pallas_call(
        functools.partial(_c_in_kernel, seq_len=seq_len, n_seq=n_seq),
        grid=(rows // tr, C_MAIN // C_TILE),
        in_specs=[pl.BlockSpec((tr, D_MODEL), lambda i, j: (i, 0)),
                  pl.BlockSpec((1, 3, D_MODEL), mod_map),
                  _full(g_pre.shape),
                  pl.BlockSpec((D_MODEL, C_TILE), lambda i, j: (0, j)),
                  _full(w_dt.shape), _full(dt_b.shape),
                  pl.BlockSpec((SSD_CONV_K, C_TILE), conv_map),
                  pl.BlockSpec((1, C_TILE), conv_map)],
        out_specs=[pl.BlockSpec((n_slab, tr, LANES), lambda i, j: (j, i, 0)),
                   pl.BlockSpec((tr, 2 * HEAD_SLAB), lambda i, j: (i, 0))],
        out_shape=[jax.ShapeDtypeStruct((C_MAIN // LANES, rows, LANES), F32),
                   jax.ShapeDtypeStruct((rows, 2 * HEAD_SLAB), F32)],
        scratch_shapes=[pltpu.VMEM((tr, D_MODEL), BF16),
                        pltpu.VMEM((n_slab, SUBLANES + n_seq * (seq_len + SUBLANES), LANES), F32)],
        compiler_params=_params(2),
        name="c_in",
    )(x2, mods, g_pre, w_main, w_dt, dt_b, conv_w, conv_b)


def _split3(a):
    hi = a.astype(BF16)
    r1 = a - hi.astype(F32)
    mid = r1.astype(BF16)
    lo = (r1 - mid.astype(F32)).astype(BF16)
    return hi, mid, lo


def _ssd_kernel(*refs, n_chunks, has_init, emit_state):
    refs = list(refs)
    xs_ref, z_ref, b_ref, c_ref, dt_ref, x_ref = refs[:6]
    refs = refs[6:]
    st0_ref = refs.pop(0) if has_init else None
    aneg_ref, dskip_ref, gnorm_ref, wout_ref, mod_ref, gfin_ref, y_ref = refs[:7]
    refs = refs[7:]
    stout_ref = refs.pop(0) if emit_state else None
    st_scr, ybuf_scr, ycur_scr, xw_scr, xt_scr = refs

    q = SSD_CHUNK
    s = pl.program_id(1)
    bwd = s >= n_chunks
    c = jnp.where(bwd, 2 * n_chunks - 1 - s, s)
    n_slab = SSD_INNER // LANES

    @pl.when(jnp.logical_not(bwd))
    def _():
        for p in range(n_slab):
            xt_scr[c, p * LANES:(p + 1) * LANES, :] = xs_ref[p].T

    def init_state(d):
        if has_init:
            st_scr[...] = st0_ref[0, d]
        else:
            st_scr[...] = jnp.zeros(st_scr.shape, F32)

    @pl.when(s == 0)
    def _():
        init_state(0)

    @pl.when(s == n_chunks)
    def _():
        init_state(1)

    dt = dt_ref[...]
    a = dt * aneg_ref[0]
    ri = lax.broadcasted_iota(jnp.int32, (q, q), 0)
    ci = lax.broadcasted_iota(jnp.int32, (q, q), 1)
    keep_t = jnp.where(bwd, ri, ci) >= jnp.where(bwd, ci, ri)
    keep_n = jnp.where(bwd, ci, ri) >= jnp.where(bwd, ri, ci)
    pieces = jnp.concatenate(_split3(a), axis=1)
    c3 = _dot(jnp.where(keep_n, 1.0, 0.0).astype(BF16), pieces)
    cum = c3[:, 0:LANES] + c3[:, LANES:2 * LANES] + c3[:, 2 * LANES:3 * LANES]
    cum_t = cum.T
    dt_t = dt.T
    total_t = jnp.sum(a.T, axis=1, keepdims=True)
    ecum_t = jnp.exp(cum_t)
    wgt_t = jnp.exp(total_t - cum_t) * dt_t
    cdec = jnp.exp(total_t)
    neg_inf = jnp.float32(-jnp.inf)

    hpg = SSD_HEADS // SSD_GROUPS
    gw = hpg * SSD_HEAD_DIM
    for g in range(SSD_GROUPS):
        grows = slice(g * gw, (g + 1) * gw)
        bg = b_ref[g].astype(BF16)
        cg = c_ref[g].astype(BF16)
        cb_t = lax.dot_general(bg, cg, NT, preferred_element_type=F32)
        stg = st_scr[grows, :]
        yoff_t = lax.dot_general(stg.astype(BF16), cg, NT, preferred_element_type=F32)
        decs = []
        for eh in range(hpg):
            e = g * hpg + eh
            rows = slice(e * SSD_HEAD_DIM, (e + 1) * SSD_HEAD_DIM)
            seg = cum_t[e:e + 1, :] - cum[:, e:e + 1]
            m_t = (cb_t * jnp.exp(jnp.where(keep_t, seg, neg_inf))).astype(BF16)
            xe = xt_scr[c, rows, :]
            yd = _dot((xe * dt_t[e:e + 1, :]).astype(BF16), m_t)
            ycur_scr[rows, :] = yd + yoff_t[eh * SSD_HEAD_DIM:(eh + 1) * SSD_HEAD_DIM, :] * ecum_t[e:e + 1, :]
            xw_scr[rows, :] = (xe * wgt_t[e:e + 1, :]).astype(BF16)
            decs.append(jnp.broadcast_to(cdec[e:e + 1, :], (SSD_HEAD_DIM, SSD_STATE)))
        new = _dot(xw_scr[grows, :], bg)
        st_scr[grows, :] = stg * jnp.concatenate(decs, axis=0) + new

    @pl.when(jnp.logical_not(bwd))
    def _():
        ybuf_scr[c] = ycur_scr[...]

    if emit_state:
        @pl.when(s == n_chunks - 1)
        def _():
            stout_ref[0, 0] = st_scr[...]

        @pl.when(s == 2 * n_chunks - 1)
        def _():
            stout_ref[0, 1] = st_scr[...]

    @pl.when(bwd)
    def _():
        ytot = ybuf_scr[c] + ycur_scr[...] + dskip_ref[...] * xt_scr[c]
        zt = jnp.concatenate([z_ref[p].T for p in range(n_slab)], axis=0)
        gated = ytot * _silu(zt)
        inv = lax.rsqrt(jnp.mean(gated * gated, axis=0, keepdims=True) + EPS)
        yn = (gated * inv * gnorm_ref[...]).T.astype(BF16)
        xn = x_ref[...] + mod_ref[0, 2:3, :] * _dot(yn, wout_ref[...])
        y_ref[...] = _rms(xn, gfin_ref[...])


def _ssd_call(u3, dt2, x2, seq_len, st0, a_neg, d_skip, g_norm, w_out, mods, mod_row, g_final, emit_state):
    rows = x2.shape[0]
    q = SSD_CHUNK
    nc = seq_len // q
    nb = rows // seq_len
    has_init = st0 is not None

    def chunk(b, s):
        return b * nc + jnp.where(s >= nc, 2 * nc - 1 - s, s)

    def late(b, s):
        return b * nc + jnp.where(s >= nc, 2 * nc - 1 - s, nc - 1)

    def early(b, s):
        return b * nc + jnp.minimum(s, nc - 1)

    direction = lambda s: (s >= nc).astype(jnp.int32)
    n_slab = SSD_INNER // LANES
    mod_map = (lambda b, s: (b, 0, 0)) if mod_row is None else (lambda b, s: (mod_row, 0, 0))
    in_specs = [pl.BlockSpec((n_slab, q, LANES), lambda b, s: (1, early(b, s), 0)),
                pl.BlockSpec((n_slab, q, LANES), lambda b, s: (0, late(b, s), 0)),
                pl.BlockSpec((SSD_GROUPS, q, LANES), lambda b, s: (2 * n_slab // SSD_GROUPS, chunk(b, s), 0)),
                pl.BlockSpec((SSD_GROUPS, q, LANES), lambda b, s: (2 * n_slab // SSD_GROUPS + 1, chunk(b, s), 0)),
                pl.BlockSpec((q, HEAD_SLAB), lambda b, s: (chunk(b, s), direction(s))),
                pl.BlockSpec((q, D_MODEL), lambda b, s: (late(b, s), 0))]
    args = [u3, u3, u3, u3, dt2, x2]
    if has_init:
        in_specs.append(pl.BlockSpec((1, 2, SSD_INNER, SSD_STATE), lambda b, s: (b, 0, 0, 0)))
        args.append(st0)
    in_specs += [pl.BlockSpec((1, 1, HEAD_SLAB), lambda b, s: (direction(s), 0, 0)),
                 _full(d_skip.shape), _full(g_norm.shape), _full(w_out.shape),
                 pl.BlockSpec((1, 3, D_MODEL), mod_map), _full(g_final.shape)]
    args += [a_neg, d_skip, g_norm, w_out, mods, g_final]
    out_specs = [pl.BlockSpec((q, D_MODEL), lambda b, s: (late(b, s), 0))]
    out_shape = [jax.ShapeDtypeStruct((rows, D_MODEL), F32)]
    if emit_state:
        out_specs.append(pl.BlockSpec((1, 2, SSD_INNER, SSD_STATE), lambda b, s: (b, 0, 0, 0)))
        out_shape.append(jax.ShapeDtypeStruct((nb, 2, SSD_INNER, SSD_STATE), F32))
    return pl.pallas_call(
        functools.partial(_ssd_kernel, n_chunks=nc, has_init=has_init, emit_state=emit_state),
        grid=(nb, 2 * nc),
        in_specs=in_specs, out_specs=out_specs, out_shape=out_shape,
        scratch_shapes=[pltpu.VMEM((SSD_INNER, SSD_STATE), F32),
                        pltpu.VMEM((nc, SSD_INNER, q), F32),
                        pltpu.VMEM((SSD_INNER, q), F32),
                        pltpu.VMEM((SSD_INNER, q), BF16),
                        pltpu.VMEM((nc, SSD_INNER, q), F32)],
        compiler_params=_params(2),
        name="ssd_init" if has_init else "ssd",
    )(*args)


def _rope_tables(length):
    rows = length // GRID_W
    row = jnp.repeat(jnp.arange(rows, dtype=F32), GRID_W)
    col = jnp.tile(jnp.arange(GRID_W, dtype=F32), rows)
    n_freq = QK_ROPE // 4
    inv = jnp.power(ROPE_THETA, -jnp.arange(n_freq, dtype=F32) / n_freq)
    ang = jnp.concatenate([row[:, None] * inv, col[:, None] * inv], axis=-1)
    cos, sin = jnp.cos(ang), jnp.sin(ang)
    cos2 = jnp.repeat(cos, 2, axis=-1)
    sin2 = jnp.stack([-sin, sin], axis=-1).reshape(length, QK_ROPE)
    pad = HEAD_SLAB - QK_ROPE
    return (jnp.concatenate([cos2, jnp.ones((length, pad), F32)], axis=-1),
            jnp.concatenate([sin2, jnp.zeros((length, pad), F32)], axis=-1))


def _prep_layer_a(w_in, w_uq, w_uk, w_uv):
    d = w_in.shape[0]
    swap = jnp.arange(QK_ROPE) ^ 1
    kr = w_in[:, 384:416]
    z96 = jnp.zeros((d, HEAD_SLAB - QK_ROPE), F32)
    w_r = jnp.concatenate([w_in[:, 0:384], kr, z96, kr[:, swap], z96, w_in[:, 416:]], axis=1).astype(BF16)
    uq = w_uq.reshape(Q_RANK, MLA_HEADS, QK_NOPE + QK_ROPE)
    nope, rp = uq[..., :QK_NOPE], uq[..., QK_NOPE:]
    z32 = jnp.zeros((Q_RANK, MLA_HEADS, 32), F32)
    w1 = jnp.concatenate([rp, z32, nope], axis=-1).reshape(Q_RANK, MLA_HEADS * HEAD_SLAB).astype(BF16)
    w2 = jnp.concatenate([rp[..., swap], z32, jnp.zeros_like(nope)], axis=-1)
    w2 = w2.reshape(Q_RANK, MLA_HEADS * HEAD_SLAB).astype(BF16)
    uk = w_uk.reshape(KV_RANK, MLA_HEADS, QK_NOPE)
    top = jnp.concatenate([jnp.zeros((KV_RANK, MLA_HEADS, 64), F32), uk], axis=-1)
    eye = jnp.broadcast_to(jnp.eye(QK_ROPE, dtype=F32)[:, None, :], (QK_ROPE, MLA_HEADS, QK_ROPE))
    mid = jnp.concatenate([eye, jnp.zeros((QK_ROPE, MLA_HEADS, HEAD_SLAB - QK_ROPE), F32)], axis=-1)
    bot = jnp.zeros((KV_RANK - QK_ROPE, MLA_HEADS, HEAD_SLAB), F32)
    w_k = jnp.concatenate([top, mid, bot], axis=0).reshape(2 * KV_RANK, MLA_HEADS * HEAD_SLAB).astype(BF16)
    return w_r, w1, w2, w_k, w_uv.T.astype(BF16)


def kernel(x_prompt, x_sample, cache_ckv, cache_krope, state_ssd, c, c_ctx, w_mod, b_mod, g_pre, g_final,
           a_w_in, a_g_q, a_g_kv, a_w_uq, a_w_uk, a_w_uv, a_conv_w, a_conv_b, a_ln_g, a_ln_b, a_w_out,
           c_w_in, c_conv_w, c_conv_b, c_dt_bias, c_a_log, c_d, c_g_norm, c_w_out):
    nbp, lp, d = x_prompt.shape
    nbs, ls, _ = x_sample.shape
    ctx_row = nbs
    cond = jnp.concatenate([c, c_ctx[None, :], jnp.zeros((16 - nbs - 1, d), F32)], axis=0)
    mods_all = _mod_call(cond, w_mod, b_mod).reshape(DEPTH, 16, 3, d)
    row2 = lambda v: v.reshape(1, -1)

    mods = mods_all[0]
    w_r, w1, w2, w_k, w_vt = _prep_layer_a(a_w_in[0], a_w_uq[0], a_w_uk[0], a_w_uv[0])
    cos_t, sin_t = _rope_tables(ls)
    gpre = row2(g_pre[0])
    gq, gkv = row2(a_g_q[0]), row2(a_g_kv[0])
    a_wout = a_w_out[0].astype(BF16)
    conv_args = (a_conv_w[0], row2(a_conv_b[0]), row2(a_ln_g[0]), row2(a_ln_b[0]), a_wout)

    qp, kxp, cvp, sgp, ckv_p, kr_p = _a_in_call(x_prompt.reshape(nbp * lp, d), mods, ctx_row, lp, gpre,
                                                w_r, gq, gkv, w1, None, None, None)
    attn_p = _attn_call(qp.reshape(nbp, lp, -1), kxp.reshape(nbp, lp, -1), None, w_k, w_vt)
    xp1 = _a_out_call(attn_p, cvp.reshape(nbp, lp, -1), sgp.reshape(nbp, lp, -1), x_prompt, mods, ctx_row,
                      *conv_args)

    qs, kxs, cvs, sgs = _a_in_call(x_sample.reshape(nbs * ls, d), mods, None, ls, gpre,
                                   w_r, gq, gkv, w1, w2, cos_t, sin_t)
    n_ctx = cache_ckv.shape[2]
    kx_ctx = jnp.concatenate([cache_ckv[:, 0], cache_krope[:, 0],
                              jnp.zeros((nbs, n_ctx, KV_RANK - QK_ROPE), F32)], axis=-1).astype(BF16)
    attn_s = _attn_call(qs.reshape(nbs, ls, -1), kxs.reshape(nbs, ls, -1), kx_ctx, w_k, w_vt)
    xs1 = _a_out_call(attn_s, cvs.reshape(nbs, ls, -1), sgs.reshape(nbs, ls, -1), x_sample, mods, None,
                      *conv_args)

    mods = mods_all[1]
    gpre = row2(g_pre[1])
    w_in = c_w_in[0]
    w_main = w_in[:, 0:C_MAIN].astype(BF16)
    dtw = w_in[:, C_MAIN:]
    zpad = jnp.zeros((d, HEAD_SLAB - SSD_HEADS), F32)
    w_dt = jnp.concatenate([dtw[:, 0:SSD_HEADS], zpad, dtw[:, SSD_HEADS:], zpad], axis=1).astype(BF16)
    pad32 = jnp.zeros((2, HEAD_SLAB - SSD_HEADS), F32)
    dt_b = jnp.concatenate([c_dt_bias[0], pad32], axis=1).reshape(1, 2 * HEAD_SLAB)
    a_neg = jnp.concatenate([-jnp.exp(c_a_log[0]), pad32], axis=1).reshape(2, 1, HEAD_SLAB)
    lane_rep = lambda v: jnp.broadcast_to(v.reshape(-1, 1), (v.size, LANES))
    d_skip = lane_rep(jnp.repeat(c_d[0], SSD_HEAD_DIM))
    gnorm = lane_rep(c_g_norm[0])
    c_wout = c_w_out[0].astype(BF16)
    gfin = row2(g_final)
    cw, cbias = c_conv_w[0], row2(c_conv_b[0])

    xp1f = xp1.reshape(nbp * lp, d)
    up, dtp = _c_in_call(xp1f, mods, ctx_row, lp, gpre, w_main, w_dt, dt_b, cw, cbias)
    y_prompt, st_p = _ssd_call(up, dtp, xp1f, lp, None, a_neg, d_skip, gnorm, c_wout, mods, ctx_row, gfin, True)
    xs1f = xs1.reshape(nbs * ls, d)
    us, dts = _c_in_call(xs1f, mods, None, ls, gpre, w_main, w_dt, dt_b, cw, cbias)
    st0 = state_ssd[:, 0].reshape(nbs, 2, SSD_INNER, SSD_STATE)
    (y_sample,) = _ssd_call(us, dts, xs1f, ls, st0, a_neg, d_skip, gnorm, c_wout, mods, None, gfin, False)

    new_cache_ckv = ckv_p.reshape(nbp, 1, lp, KV_RANK)
    new_cache_krope = kr_p.reshape(nbp, 1, lp, QK_ROPE)
    new_state_ssd = st_p.reshape(nbp, 1, 2, SSD_HEADS, SSD_HEAD_DIM, SSD_STATE)
    return (y_prompt.reshape(nbp, lp, d), y_sample.reshape(nbs, ls, d), new_cache_ckv, new_cache_krope,
            new_state_ssd)
```

```python
import functools
import math

import jax
import jax.numpy as jnp
from jax import lax
from jax.experimental import pallas as pl
from jax.experimental.pallas import tpu as pltpu

F32 = jnp.float32
BF16 = jnp.bfloat16

LANES = 128
SUBLANES = 8

D_MODEL = 1024
DEPTH = 2
GRID_W = 64
EPS = 1e-6

MLA_HEADS = 8
Q_RANK = 256
KV_RANK = 128
QK_NOPE = 64
QK_ROPE = 32
V_HEAD = 64
MLA_WIDTH = MLA_HEADS * V_HEAD
ATTN_SCALE = (QK_NOPE + QK_ROPE) ** -0.5
ROPE_THETA = 10000.0
HEAD_SLAB = LANES

CONV_WIDTH = 512
CONV_K = 31
A_GLU0 = 640
A_GATE0 = A_GLU0 + 2 * CONV_WIDTH
A_COLS = A_GATE0 + MLA_WIDTH + CONV_WIDTH

SSD_INNER = 2048
SSD_HEAD_DIM = 64
SSD_HEADS = 32
SSD_GROUPS = 4
SSD_STATE = 128
SSD_CONV_K = 5
SSD_CHUNK = 128
FIN_CHUNK = 512
SSD_CONV_CH = SSD_INNER + 2 * SSD_GROUPS * SSD_STATE
C_MAIN = SSD_INNER + SSD_CONV_CH
C_TILE = 1024
ROW_TILE = 1024
A_ROW_TILE = 512

CONV_STRIDE = 4
CONV_BLOCK = SUBLANES * CONV_STRIDE

VMEM_LIMIT = 56 * 1024 * 1024
HI = lax.Precision.HIGHEST
NT = (((1,), (1,)), ((), ()))
LOG2E = math.log2(math.e)


def _dot(a, b):
    return jnp.dot(a, b, preferred_element_type=F32)


def _sigmoid(x):
    return 1.0 / (1.0 + jnp.exp(-x))


def _silu(x):
    return x * _sigmoid(x)


def _rms(x, g):
    return x * lax.rsqrt(jnp.mean(x * x, axis=-1, keepdims=True) + EPS) * g


def _params(n_axes):
    return pltpu.CompilerParams(dimension_semantics=("arbitrary",) * n_axes,
                                vmem_limit_bytes=VMEM_LIMIT)


def _full(shape):
    nd = len(shape)
    return pl.BlockSpec(shape, lambda *_: (0,) * nd)


def _mod_kernel(cond_ref, w_ref, b_ref, o_ref):
    s = _silu(cond_ref[...])
    o_ref[0] = jnp.dot(s, w_ref[0], precision=HI, preferred_element_type=F32) + b_ref[0]


def _mod_call(cond, w_mod, b_mod):
    nrow = cond.shape[0]
    tn = 1024
    return pl.pallas_call(
        _mod_kernel,
        grid=(DEPTH, 3 * D_MODEL // tn),
        in_specs=[pl.BlockSpec((nrow, D_MODEL), lambda l, j: (0, 0)),
                  pl.BlockSpec((1, D_MODEL, tn), lambda l, j: (l, 0, j)),
                  pl.BlockSpec((1, 1, tn), lambda l, j: (l, 0, j))],
        out_specs=pl.BlockSpec((1, nrow, tn), lambda l, j: (l, 0, j)),
        out_shape=jax.ShapeDtypeStruct((DEPTH, nrow, 3 * D_MODEL), F32),
        compiler_params=_params(2),
        name="modulation",
    )(cond, w_mod, b_mod.reshape(DEPTH, 1, 3 * D_MODEL))


def _a_in_kernel(*refs, rope):
    if rope:
        (x_ref, mod_ref, gpre_ref, w_ref, gq_ref, gkv_ref, w1_ref, w2_ref, cos_ref, sin_ref,
         q_ref, kx_ref, cv_ref, sg_ref) = refs
    else:
        (x_ref, mod_ref, gpre_ref, w_ref, gq_ref, gkv_ref, w1_ref,
         q_ref, kx_ref, cv_ref, sg_ref, ckv_ref, kr_ref) = refs
    x = x_ref[...]
    h = _rms(x, gpre_ref[...]) * (1.0 + mod_ref[0, 1:2, :]) + mod_ref[0, 0:1, :]
    hb = h.astype(BF16)

    u0 = _dot(hb, w_ref[:, 0:A_GLU0])
    qn = _rms(u0[:, 0:Q_RANK], gq_ref[...]).astype(BF16)
    ckv = _rms(u0[:, Q_RANK:Q_RANK + KV_RANK], gkv_ref[...])
    kr = u0[:, 384:512]
    qf = _dot(qn, w1_ref[...])
    qscale = ATTN_SCALE * LOG2E
    if rope:
        cos = cos_ref[...]
        sin = sin_ref[...]
        qs = _dot(qn, w2_ref[...])
        for hd in range(MLA_HEADS):
            sl = slice(hd * HEAD_SLAB, (hd + 1) * HEAD_SLAB)
            q_ref[:, sl] = ((qf[:, sl] * cos + qs[:, sl] * sin) * qscale).astype(BF16)
        kr = kr * cos + u0[:, 512:640] * sin
    else:
        q_ref[...] = (qf * qscale).astype(BF16)
        ckv_ref[...] = ckv
        kr_ref[...] = kr[:, 0:QK_ROPE]
    kx_ref[:, 0:KV_RANK] = ckv.astype(BF16)
    kx_ref[:, KV_RANK:2 * KV_RANK] = kr.astype(BF16)

    glu = _dot(hb, w_ref[:, A_GLU0:A_GATE0])
    cv_ref[...] = glu[:, 0:CONV_WIDTH] * _sigmoid(glu[:, CONV_WIDTH:])
    sg_ref[...] = _silu(_dot(hb, w_ref[:, A_GATE0:A_COLS]))


def _a_in_call(x2, mods, mod_row, seq_len, g_pre, w_in, g_q, g_kv, w1, w2, cos_t, sin_t):
    rows = x2.shape[0]
    tr = A_ROW_TILE
    rope = w2 is not None
    per_seq = max(seq_len // tr, 1)
    if mod_row is None:
        mod_map = lambda i: (i // per_seq, 0, 0)
    else:
        mod_map = lambda i: (mod_row, 0, 0)
    row = lambda i: (i, 0)
    in_specs = [pl.BlockSpec((tr, D_MODEL), row),
                pl.BlockSpec((1, 3, D_MODEL), mod_map),
                _full(g_pre.shape), _full(w_in.shape), _full(g_q.shape), _full(g_kv.shape),
                _full(w1.shape)]
    args = [x2, mods, g_pre, w_in, g_q, g_kv, w1]
    out_specs = [pl.BlockSpec((tr, MLA_HEADS * HEAD_SLAB), row),
                 pl.BlockSpec((tr, 2 * KV_RANK), row),
                 pl.BlockSpec((tr, CONV_WIDTH), row),
                 pl.BlockSpec((tr, MLA_WIDTH + CONV_WIDTH), row)]
    out_shape = [jax.ShapeDtypeStruct((rows, MLA_HEADS * HEAD_SLAB), BF16),
                 jax.ShapeDtypeStruct((rows, 2 * KV_RANK), BF16),
                 jax.ShapeDtypeStruct((rows, CONV_WIDTH), F32),
                 jax.ShapeDtypeStruct((rows, MLA_WIDTH + CONV_WIDTH), F32)]
    if rope:
        in_specs += [_full(w2.shape),
                     pl.BlockSpec((tr, HEAD_SLAB), lambda i: (i % per_seq, 0)),
                     pl.BlockSpec((tr, HEAD_SLAB), lambda i: (i % per_seq, 0))]
        args += [w2, cos_t, sin_t]
    else:
        out_specs += [pl.BlockSpec((tr, KV_RANK), row), pl.BlockSpec((tr, QK_ROPE), row)]
        out_shape += [jax.ShapeDtypeStruct((rows, KV_RANK), F32),
                      jax.ShapeDtypeStruct((rows, QK_ROPE), F32)]
    return pl.pallas_call(
        functools.partial(_a_in_kernel, rope=rope),
        grid=(rows // tr,),
        in_specs=in_specs, out_specs=out_specs, out_shape=out_shape,
        compiler_params=_params(1),
        name="a_in_rope" if rope else "a_in",
    )(*args)


def _attn_kernel(*refs, n_ctx, seq_len):
    if n_ctx:
        q_ref, kxn_ref, kxc_ref, wk_ref, wvt_ref, o_ref, k_scr, vt_scr, ot_scr, s_scr = refs
    else:
        q_ref, kxn_ref, wk_ref, wvt_ref, o_ref, k_scr, vt_scr, ot_scr, s_scr = refs

    @pl.when(pl.program_id(1) == 0)
    def _():
        def fill(kx, lo, n):
            k_scr[lo:lo + n, :] = _dot(kx, wk_ref[...]).astype(BF16)
            vt_scr[:, lo:lo + n] = lax.dot_general(
                wvt_ref[...], kx[:, 0:KV_RANK], NT, preferred_element_type=F32).astype(BF16)
        if n_ctx:
            fill(kxc_ref[0], 0, n_ctx)
        fill(kxn_ref[0], n_ctx, seq_len)

    def scores(hd):
        sl = slice(hd * HEAD_SLAB, (hd + 1) * HEAD_SLAB)
        s_scr[hd % 2] = lax.dot_general(k_scr[:, sl], q_ref[0, :, sl], NT, preferred_element_type=F32)

    scores(0)
    for hd in range(MLA_HEADS):
        vrows = slice(hd * V_HEAD, (hd + 1) * V_HEAD)
        if hd + 1 < MLA_HEADS:
            scores(hd + 1)
        st = s_scr[hd % 2]
        p = jnp.exp2(st - jnp.max(st, axis=0, keepdims=True))
        den = jnp.sum(p, axis=0, keepdims=True)
        ot_scr[vrows, :] = _dot(vt_scr[vrows, :], p.astype(BF16)) / den
    o_ref[0] = ot_scr[...].T


def _attn_call(q3, kx_new, kx_ctx, w_k, w_vt):
    nb, seq_len, _ = q3.shape
    n_ctx = 0 if kx_ctx is None else kx_ctx.shape[1]
    tq = min(seq_len, 256)
    lk = n_ctx + seq_len
    in_specs = [pl.BlockSpec((1, tq, MLA_HEADS * HEAD_SLAB), lambda b, i: (b, i, 0)),
                pl.BlockSpec((1, seq_len, 2 * KV_RANK), lambda b, i: (b, 0, 0))]
    args = [q3, kx_new]
    if n_ctx:
        in_specs.append(pl.BlockSpec((1, n_ctx, 2 * KV_RANK), lambda b, i: (b, 0, 0)))
        args.append(kx_ctx)
    in_specs += [_full(w_k.shape), _full(w_vt.shape)]
    args += [w_k, w_vt]
    return pl.pallas_call(
        functools.partial(_attn_kernel, n_ctx=n_ctx, seq_len=seq_len),
        grid=(nb, seq_len // tq),
        in_specs=in_specs,
        out_specs=pl.BlockSpec((1, tq, MLA_WIDTH), lambda b, i: (b, i, 0)),
        out_shape=jax.ShapeDtypeStruct((nb, seq_len, MLA_WIDTH), F32),
        scratch_shapes=[pltpu.VMEM((lk, MLA_HEADS * HEAD_SLAB), BF16),
                        pltpu.VMEM((MLA_WIDTH, lk), BF16),
                        pltpu.VMEM((MLA_WIDTH, tq), F32),
                        pltpu.VMEM((2, lk, tq), F32)],
        compiler_params=_params(2),
        name="attn_ctx" if n_ctx else "attn",
    )(*args)


def _a_out_kernel(attn_ref, cv_ref, sg_ref, x_ref, mod_ref, cw_ref, cb_ref, lng_ref, lnb_ref, wout_ref,
                  o_ref, pad_scr, cvn_scr, *, seq_len):
    halo = 16
    n_slab = CONV_WIDTH // LANES
    zeros = jnp.zeros((halo, LANES), F32)
    for sl in range(n_slab):
        pad_scr[sl, 0:halo, :] = zeros
        pad_scr[sl, halo + seq_len:2 * halo + seq_len, :] = zeros
        pad_scr[sl, halo:halo + seq_len, :] = cv_ref[0, :, sl * LANES:(sl + 1) * LANES]

    def block(r, carry):
        base = r * CONV_BLOCK
        for sl in range(n_slab):
            lanes = slice(sl * LANES, (sl + 1) * LANES)
            accs = [None] * CONV_STRIDE
            for k in range(CONV_K):
                w = jnp.broadcast_to(cw_ref[k:k + 1, lanes], (SUBLANES, LANES))
                for ph in range(CONV_STRIDE):
                    start = base + (halo - CONV_K // 2 + ph + k)
                    v = pad_scr[sl, pl.ds(start, SUBLANES, stride=CONV_STRIDE), :] * w
                    accs[ph] = v if accs[ph] is None else accs[ph] + v
            for ph in range(CONV_STRIDE):
                cvn_scr[sl, pl.ds(base + ph, SUBLANES, stride=CONV_STRIDE), :] = accs[ph]
        return carry

    lax.fori_loop(0, seq_len // CONV_BLOCK, block, 0)
    conv = jnp.concatenate([cvn_scr[sl] for sl in range(n_slab)], axis=1) + cb_ref[...]
    mu = jnp.mean(conv, axis=-1, keepdims=True)
    cen = conv - mu
    ln = cen * lax.rsqrt(jnp.mean(cen * cen, axis=-1, keepdims=True) + EPS) * lng_ref[...] + lnb_ref[...]
    sg = sg_ref[0]
    mix = jnp.concatenate([attn_ref[0] * sg[:, 0:MLA_WIDTH], _silu(ln) * sg[:, MLA_WIDTH:]], axis=1)
    out = _dot(mix.astype(BF16), wout_ref[...])
    o_ref[0] = x_ref[0] + mod_ref[0, 2:3, :] * out


def _a_out_call(attn3, cv3, sg3, x3, mods, mod_row, conv_w, conv_b, ln_g, ln_b, w_out):
    nb, seq_len, _ = x3.shape
    blk = lambda c: pl.BlockSpec((1, seq_len, c), lambda b: (b, 0, 0))
    mod_map = (lambda b: (b, 0, 0)) if mod_row is None else (lambda b: (mod_row, 0, 0))
    n_slab = CONV_WIDTH // LANES
    return pl.pallas_call(
        functools.partial(_a_out_kernel, seq_len=seq_len),
        grid=(nb,),
        in_specs=[blk(MLA_WIDTH), blk(CONV_WIDTH), blk(MLA_WIDTH + CONV_WIDTH), blk(D_MODEL),
                  pl.BlockSpec((1, 3, D_MODEL), mod_map),
                  _full(conv_w.shape), _full(conv_b.shape), _full(ln_g.shape), _full(ln_b.shape),
                  _full(w_out.shape)],
        out_specs=blk(D_MODEL),
        out_shape=jax.ShapeDtypeStruct((nb, seq_len, D_MODEL), F32),
        scratch_shapes=[pltpu.VMEM((n_slab, seq_len + 32, LANES), F32),
                        pltpu.VMEM((n_slab, seq_len, LANES), F32)],
        compiler_params=_params(1),
        name="a_out",
    )(attn3, cv3, sg3, x3, mods, conv_w, conv_b, ln_g, ln_b, w_out)


def _softplus(x):
    return jnp.maximum(x, 0.0) + jnp.log1p(jnp.exp(-jnp.abs(x)))


def _c_in_kernel(x_ref, mod_ref, gpre_ref, w_ref, wdt_ref, dtb_ref, cw_ref, cb_ref,
                 u_ref, dt_ref, h_scr, pad_scr, *, seq_len, n_seq):
    j = pl.program_id(1)
    gap = SUBLANES
    pitch = seq_len + gap
    n_slab = C_TILE // LANES
    n_z = SSD_INNER // C_TILE

    @pl.when(j == 0)
    def _():
        h = _rms(x_ref[...], gpre_ref[...]) * (1.0 + mod_ref[0, 1:2, :]) + mod_ref[0, 0:1, :]
        h_scr[...] = h.astype(BF16)
        dt_ref[...] = _softplus(_dot(h_scr[...], wdt_ref[...]) + dtb_ref[...])

    u = _dot(h_scr[...], w_ref[...])

    @pl.when(j < n_z)
    def _():
        for sl in range(n_slab):
            u_ref[sl] = u[:, sl * LANES:(sl + 1) * LANES]

    @pl.when(j >= n_z)
    def _():
        zeros = jnp.zeros((gap, LANES), F32)
        for sl in range(n_slab):
            pad_scr[sl, 0:gap, :] = zeros
            for s in range(n_seq):
                lo = gap + s * pitch
                pad_scr[sl, lo:lo + seq_len, :] = u[s * seq_len:(s + 1) * seq_len, sl * LANES:(sl + 1) * LANES]
                pad_scr[sl, lo + seq_len:lo + pitch, :] = zeros

        def block(r, carry):
            base = r * CONV_BLOCK
            for sl in range(n_slab):
                lanes = slice(sl * LANES, (sl + 1) * LANES)
                ws = [jnp.broadcast_to(cw_ref[k:k + 1, lanes], (SUBLANES, LANES)) for k in range(SSD_CONV_K)]
                bias = jnp.broadcast_to(cb_ref[:, lanes], (SUBLANES, LANES))
                for s in range(n_seq):
                    for ph in range(CONV_STRIDE):
                        acc = bias
                        for k in range(SSD_CONV_K):
                            start = base + (gap + s * pitch - SSD_CONV_K // 2 + ph + k)
                            acc = acc + pad_scr[sl, pl.ds(start, SUBLANES, stride=CONV_STRIDE), :] * ws[k]
                        u_ref[sl, pl.ds(base + (s * seq_len + ph), SUBLANES, stride=CONV_STRIDE), :] = _silu(acc)
            return carry

        lax.fori_loop(0, seq_len // CONV_BLOCK, block, 0)


def _c_in_call(x2, mods, mod_row, seq_len, g_pre, w_main, w_dt, dt_b, conv_w, conv_b):
    rows = x2.shape[0]
    tr = ROW_TILE
    n_seq = tr // seq_len
    n_z = SSD_INNER // C_TILE
    n_slab = C_TILE // LANES
    mod_map = (lambda i, j: (i, 0, 0)) if mod_row is None else (lambda i, j: (mod_row, 0, 0))
    conv_map = lambda i, j: (0, jnp.maximum(j - n_z, 0))
    return pl.pallas_call(
        functools.partial(_c_in_kernel, seq_len=seq_len, n_seq=n_seq),
        grid=(rows // tr, C_MAIN // C_TILE),
        in_specs=[pl.BlockSpec((tr, D_MODEL), lambda i, j: (i, 0)),
                  pl.BlockSpec((1, 3, D_MODEL), mod_map),
                  _full(g_pre.shape),
                  pl.BlockSpec((D_MODEL, C_TILE), lambda i, j: (0, j)),
                  _full(w_dt.shape), _full(dt_b.shape),
                  pl.BlockSpec((SSD_CONV_K, C_TILE), conv_map),
                  pl.BlockSpec((1, C_TILE), conv_map)],
        out_specs=[pl.BlockSpec((n_slab, tr, LANES), lambda i, j: (j, i, 0)),
                   pl.BlockSpec((tr, 2 * HEAD_SLAB), lambda i, j: (i, 0))],
        out_shape=[jax.ShapeDtypeStruct((C_MAIN // LANES, rows, LANES), F32),
                   jax.ShapeDtypeStruct((rows, 2 * HEAD_SLAB), F32)],
        scratch_shapes=[pltpu.VMEM((tr, D_MODEL), BF16),
                        pltpu.VMEM((n_slab, SUBLANES + n_seq * (seq_len + SUBLANES), LANES), F32)],
        compiler_params=_params(2),
        name="c_in",
    )(x2, mods, g_pre, w_main, w_dt, dt_b, conv_w, conv_b)


def _split3(a):
    hi = a.astype(BF16)
    r1 = a - hi.astype(F32)
    mid = r1.astype(BF16)
    lo = (r1 - mid.astype(F32)).astype(BF16)
    return hi, mid, lo


def _ssd_kernel(*refs, n_chunks, has_init, emit_state):
    refs = list(refs)
    xs_ref, z_ref, b_ref, c_ref, dt_ref, x_ref = refs[:6]
    refs = refs[6:]
    st0_ref = refs.pop(0) if has_init else None
    aneg_ref, dskip_ref, gnorm_ref, wout_ref, mod_ref, gfin_ref, y_ref = refs[:7]
    refs = refs[7:]
    stout_ref = refs.pop(0) if emit_state else None
    st_scr, ybuf_scr, xw_scr, xt_scr, xtb_scr = refs

    q = SSD_CHUNK
    s = pl.program_id(1)
    bwd = s >= n_chunks
    c = jnp.where(bwd, 2 * n_chunks - 1 - s, s)
    slot = jnp.where(bwd, n_chunks, c)
    n_slab = SSD_INNER // LANES

    @pl.when(jnp.logical_not(bwd))
    def _():
        for p in range(n_slab):
            xp_t = xs_ref[p].T
            xt_scr[c, p * LANES:(p + 1) * LANES, :] = xp_t
            xtb_scr[c, p * LANES:(p + 1) * LANES, :] = xp_t.astype(BF16)

    def init_state(d):
        if has_init:
            st_scr[...] = st0_ref[0, d]
        else:
            st_scr[...] = jnp.zeros(st_scr.shape, F32)

    @pl.when(s == 0)
    def _():
        init_state(0)

    @pl.when(s == n_chunks)
    def _():
        init_state(1)

    dt = dt_ref[...]
    a = dt * aneg_ref[0]
    ri = lax.broadcasted_iota(jnp.int32, (q, q), 0)
    ci = lax.broadcasted_iota(jnp.int32, (q, q), 1)
    keep_t = jnp.where(bwd, ri, ci) >= jnp.where(bwd, ci, ri)
    keep_n = jnp.where(bwd, ci, ri) >= jnp.where(bwd, ri, ci)
    pieces = jnp.concatenate(_split3(a), axis=1)
    c3 = _dot(jnp.where(keep_n, 1.0, 0.0).astype(BF16), pieces)
    cum = c3[:, 0:LANES] + c3[:, LANES:2 * LANES] + c3[:, 2 * LANES:3 * LANES]
    cum_t = cum.T
    dt_t = dt.T
    total_t = jnp.sum(a.T, axis=1, keepdims=True)
    ecum_t = jnp.exp(cum_t)
    wgt_t = jnp.exp(total_t - cum_t) * dt_t
    cdec = jnp.exp(total_t)
    neg_inf = jnp.float32(-jnp.inf)

    def row_bf16(v, e, n):
        return jnp.broadcast_to(v[e:e + 1, :], (n, q)).astype(BF16)

    hpg = SSD_HEADS // SSD_GROUPS
    gw = hpg * SSD_HEAD_DIM
    for g in range(SSD_GROUPS):
        grows = slice(g * gw, (g + 1) * gw)
        bg = b_ref[g].astype(BF16)
        cg32 = c_ref[g]
        cb_t = lax.dot_general(bg, cg32.astype(BF16), NT, preferred_element_type=F32)
        cg_t = cg32.T.astype(BF16)
        stg = st_scr[grows, :]
        stg_b = stg.astype(BF16)
        decs = []
        for eh in range(hpg):
            e = g * hpg + eh
            rows = slice(e * SSD_HEAD_DIM, (e + 1) * SSD_HEAD_DIM)
            row = jnp.broadcast_to(cum_t[e:e + 1, :], (q, q))
            seg = row - row.T
            m_t = (cb_t * jnp.exp(jnp.where(keep_t, seg, neg_inf))).astype(BF16)
            xe = xtb_scr[c, rows, :]
            lhs = jnp.concatenate([xe * row_bf16(dt_t, e, SSD_HEAD_DIM),
                                   stg_b[eh * SSD_HEAD_DIM:(eh + 1) * SSD_HEAD_DIM, :]], axis=1)
            rhs = jnp.concatenate([m_t, cg_t * row_bf16(ecum_t, e, SSD_STATE)], axis=0)
            ybuf_scr[slot, rows, :] = _dot(lhs, rhs)
            xw_scr[rows, :] = xe * row_bf16(wgt_t, e, SSD_HEAD_DIM)
            decs.append(jnp.broadcast_to(cdec[e:e + 1, :], (SSD_HEAD_DIM, SSD_STATE)))
        new = _dot(xw_scr[grows, :], bg)
        st_scr[grows, :] = stg * jnp.concatenate(decs, axis=0) + new

    if emit_state:
        @pl.when(s == n_chunks - 1)
        def _():
            stout_ref[0, 0] = st_scr[...]

        @pl.when(s == 2 * n_chunks - 1)
        def _():
            stout_ref[0, 1] = st_scr[...]

    @pl.when(bwd)
    def _():
        acc = None
        ss = jnp.zeros((1, q), F32)
        slabs_per = FIN_CHUNK // LANES
        for k in range(SSD_INNER // FIN_CHUNK):
            rows = slice(k * FIN_CHUNK, (k + 1) * FIN_CHUNK)
            ytot = ybuf_scr[c, rows, :] + ybuf_scr[n_chunks, rows, :] + dskip_ref[rows, :] * xt_scr[c, rows, :]
            zt = jnp.concatenate([z_ref[p].T for p in range(k * slabs_per, (k + 1) * slabs_per)], axis=0)
            gated = ytot * _silu(zt)
            ss = ss + jnp.sum(gated * gated, axis=0, keepdims=True)
            part = _dot((gated * gnorm_ref[rows, :]).T.astype(BF16), wout_ref[rows, :])
            acc = part if acc is None else acc + part
        inv = lax.rsqrt(ss * (1.0 / SSD_INNER) + EPS)
        inv_col = jnp.broadcast_to(inv, (q, q)).T
        scale = jnp.concatenate([inv_col] * (D_MODEL // LANES), axis=1)
        xn = x_ref[...] + mod_ref[0, 2:3, :] * (acc * scale)
        y_ref[...] = _rms(xn, gfin_ref[...])


def _ssd_call(u3, dt2, x2, seq_len, st0, a_neg, d_skip, g_norm, w_out, mods, mod_row, g_final, emit_state):
    rows = x2.shape[0]
    q = SSD_CHUNK
    nc = seq_len // q
    nb = rows // seq_len
    has_init = st0 is not None

    def chunk(b, s):
        return b * nc + jnp.where(s >= nc, 2 * nc - 1 - s, s)

    def late(b, s):
        return b * nc + jnp.where(s >= nc, 2 * nc - 1 - s, nc - 1)

    def early(b, s):
        return b * nc + jnp.minimum(s, nc - 1)

    direction = lambda s: (s >= nc).astype(jnp.int32)
    n_slab = SSD_INNER // LANES
    mod_map = (lambda b, s: (b, 0, 0)) if mod_row is None else (lambda b, s: (mod_row, 0, 0))
    in_specs = [pl.BlockSpec((n_slab, q, LANES), lambda b, s: (1, early(b, s), 0)),
                pl.BlockSpec((n_slab, q, LANES), lambda b, s: (0, late(b, s), 0)),
                pl.BlockSpec((SSD_GROUPS, q, LANES), lambda b, s: (2 * n_slab // SSD_GROUPS, chunk(b, s), 0)),
                pl.BlockSpec((SSD_GROUPS, q, LANES), lambda b, s: (2 * n_slab // SSD_GROUPS + 1, chunk(b, s), 0)),
                pl.BlockSpec((q, HEAD_SLAB), lambda b, s: (chunk(b, s), direction(s))),
                pl.BlockSpec((q, D_MODEL), lambda b, s: (late(b, s), 0))]
    args = [u3, u3, u3, u3, dt2, x2]
    if has_init:
        in_specs.append(pl.BlockSpec((1, 2, SSD_INNER, SSD_STATE), lambda b, s: (b, 0, 0, 0)))
        args.append(st0)
    in_specs += [pl.BlockSpec((1, 1, HEAD_SLAB), lambda b, s: (direction(s), 0, 0)),
                 _full(d_skip.shape), _full(g_norm.shape), _full(w_out.shape),
                 pl.BlockSpec((1, 3, D_MODEL), mod_map), _full(g_final.shape)]
    args += [a_neg, d_skip, g_norm, w_out, mods, g_final]
    out_specs = [pl.BlockSpec((q, D_MODEL), lambda b, s: (late(b, s), 0))]
    out_shape = [jax.ShapeDtypeStruct((rows, D_MODEL), F32)]
    if emit_state:
        out_specs.append(pl.BlockSpec((1, 2, SSD_INNER, SSD_STATE), lambda b, s: (b, 0, 0, 0)))
        out_shape.append(jax.ShapeDtypeStruct((nb, 2, SSD_INNER, SSD_STATE), F32))
    return pl.pallas_call(
        functools.partial(_ssd_kernel, n_chunks=nc, has_init=has_init, emit_state=emit_state),
        grid=(nb, 2 * nc),
        in_specs=in_specs, out_specs=out_specs, out_shape=out_shape,
        scratch_shapes=[pltpu.VMEM((SSD_INNER, SSD_STATE), F32),
                        pltpu.VMEM((nc + 1, SSD_INNER, q), F32),
                        pltpu.VMEM((SSD_INNER, q), BF16),
                        pltpu.VMEM((nc, SSD_INNER, q), F32),
                        pltpu.VMEM((nc, SSD_INNER, q), BF16)],
        compiler_params=_params(2),
        name="ssd_init" if has_init else "ssd",
    )(*args)


def _rope_tables(length):
    rows = length // GRID_W
    row = jnp.repeat(jnp.arange(rows, dtype=F32), GRID_W)
    col = jnp.tile(jnp.arange(GRID_W, dtype=F32), rows)
    n_freq = QK_ROPE // 4
    inv = jnp.power(ROPE_THETA, -jnp.arange(n_freq, dtype=F32) / n_freq)
    ang = jnp.concatenate([row[:, None] * inv, col[:, None] * inv], axis=-1)
    cos, sin = jnp.cos(ang), jnp.sin(ang)
    cos2 = jnp.repeat(cos, 2, axis=-1)
    sin2 = jnp.stack([-sin, sin], axis=-1).reshape(length, QK_ROPE)
    pad = HEAD_SLAB - QK_ROPE
    return (jnp.concatenate([cos2, jnp.ones((length, pad), F32)], axis=-1),
            jnp.concatenate([sin2, jnp.zeros((length, pad), F32)], axis=-1))


def _prep_layer_a(w_in, w_uq, w_uk, w_uv):
    d = w_in.shape[0]
    swap = jnp.arange(QK_ROPE) ^ 1
    kr = w_in[:, 384:416]
    z96 = jnp.zeros((d, HEAD_SLAB - QK_ROPE), F32)
    w_r = jnp.concatenate([w_in[:, 0:384], kr, z96, kr[:, swap], z96, w_in[:, 416:]], axis=1).astype(BF16)
    uq = w_uq.reshape(Q_RANK, MLA_HEADS, QK_NOPE + QK_ROPE)
    nope, rp = uq[..., :QK_NOPE], uq[..., QK_NOPE:]
    z32 = jnp.zeros((Q_RANK, MLA_HEADS, 32), F32)
    w1 = jnp.concatenate([rp, z32, nope], axis=-1).reshape(Q_RANK, MLA_HEADS * HEAD_SLAB).astype(BF16)
    w2 = jnp.concatenate([rp[..., swap], z32, jnp.zeros_like(nope)], axis=-1)
    w2 = w2.reshape(Q_RANK, MLA_HEADS * HEAD_SLAB).astype(BF16)
    uk = w_uk.reshape(KV_RANK, MLA_HEADS, QK_NOPE)
    top = jnp.concatenate([jnp.zeros((KV_RANK, MLA_HEADS, 64), F32), uk], axis=-1)
    eye = jnp.broadcast_to(jnp.eye(QK_ROPE, dtype=F32)[:, None, :], (QK_ROPE, MLA_HEADS, QK_ROPE))
    mid = jnp.concatenate([eye, jnp.zeros((QK_ROPE, MLA_HEADS, HEAD_SLAB - QK_ROPE), F32)], axis=-1)
    bot = jnp.zeros((KV_RANK - QK_ROPE, MLA_HEADS, HEAD_SLAB), F32)
    w_k = jnp.concatenate([top, mid, bot], axis=0).reshape(2 * KV_RANK, MLA_HEADS * HEAD_SLAB).astype(BF16)
    return w_r, w1, w2, w_k, w_uv.T.astype(BF16)


def kernel(x_prompt, x_sample, cache_ckv, cache_krope, state_ssd, c, c_ctx, w_mod, b_mod, g_pre, g_final,
           a_w_in, a_g_q, a_g_kv, a_w_uq, a_w_uk, a_w_uv, a_conv_w, a_conv_b, a_ln_g, a_ln_b, a_w_out,
           c_w_in, c_conv_w, c_conv_b, c_dt_bias, c_a_log, c_d, c_g_norm, c_w_out):
    nbp, lp, d = x_prompt.shape
    nbs, ls, _ = x_sample.shape
    ctx_row = nbs
    cond = jnp.concatenate([c, c_ctx[None, :], jnp.zeros((16 - nbs - 1, d), F32)], axis=0)
    mods_all = _mod_call(cond, w_mod, b_mod).reshape(DEPTH, 16, 3, d)
    row2 = lambda v: v.reshape(1, -1)

    mods = mods_all[0]
    w_r, w1, w2, w_k, w_vt = _prep_layer_a(a_w_in[0], a_w_uq[0], a_w_uk[0], a_w_uv[0])
    cos_t, sin_t = _rope_tables(ls)
    gpre = row2(g_pre[0])
    gq, gkv = row2(a_g_q[0]), row2(a_g_kv[0])
    a_wout = a_w_out[0].astype(BF16)
    conv_args = (a_conv_w[0], row2(a_conv_b[0]), row2(a_ln_g[0]), row2(a_ln_b[0]), a_wout)

    qp, kxp, cvp, sgp, ckv_p, kr_p = _a_in_call(x_prompt.reshape(nbp * lp, d), mods, ctx_row, lp, gpre,
                                                w_r, gq, gkv, w1, None, None, None)
    attn_p = _attn_call(qp.reshape(nbp, lp, -1), kxp.reshape(nbp, lp, -1), None, w_k, w_vt)
    xp1 = _a_out_call(attn_p, cvp.reshape(nbp, lp, -1), sgp.reshape(nbp, lp, -1), x_prompt, mods, ctx_row,
                      *conv_args)

    qs, kxs, cvs, sgs = _a_in_call(x_sample.reshape(nbs * ls, d), mods, None, ls, gpre,
                                   w_r, gq, gkv, w1, w2, cos_t, sin_t)
    n_ctx = cache_ckv.shape[2]
    kx_ctx = jnp.concatenate([cache_ckv[:, 0], cache_krope[:, 0],
                              jnp.zeros((nbs, n_ctx, KV_RANK - QK_ROPE), F32)], axis=-1).astype(BF16)
    attn_s = _attn_call(qs.reshape(nbs, ls, -1), kxs.reshape(nbs, ls, -1), kx_ctx, w_k, w_vt)
    xs1 = _a_out_call(attn_s, cvs.reshape(nbs, ls, -1), sgs.reshape(nbs, ls, -1), x_sample, mods, None,
                      *conv_args)

    mods = mods_all[1]
    gpre = row2(g_pre[1])
    w_in = c_w_in[0]
    w_main = w_in[:, 0:C_MAIN].astype(BF16)
    dtw = w_in[:, C_MAIN:]
    zpad = jnp.zeros((d, HEAD_SLAB - SSD_HEADS), F32)
    w_dt = jnp.concatenate([dtw[:, 0:SSD_HEADS], zpad, dtw[:, SSD_HEADS:], zpad], axis=1).astype(BF16)
    pad32 = jnp.zeros((2, HEAD_SLAB - SSD_HEADS), F32)
    dt_b = jnp.concatenate([c_dt_bias[0], pad32], axis=1).reshape(1, 2 * HEAD_SLAB)
    a_neg = jnp.concatenate([-jnp.exp(c_a_log[0]), pad32], axis=1).reshape(2, 1, HEAD_SLAB)
    lane_rep = lambda v: jnp.broadcast_to(v.reshape(-1, 1), (v.size, LANES))
    d_skip = lane_rep(jnp.repeat(c_d[0], SSD_HEAD_DIM))
    gnorm = lane_rep(c_g_norm[0])
    c_wout = c_w_out[0].astype(BF16)
    gfin = row2(g_final)
    cw, cbias = c_conv_w[0], row2(c_conv_b[0])

    xp1f = xp1.reshape(nbp * lp, d)
    up, dtp = _c_in_call(xp1f, mods, ctx_row, lp, gpre, w_main, w_dt, dt_b, cw, cbias)
    y_prompt, st_p = _ssd_call(up, dtp, xp1f, lp, None, a_neg, d_skip, gnorm, c_wout, mods, ctx_row, gfin, True)
    xs1f = xs1.reshape(nbs * ls, d)
    us, dts = _c_in_call(xs1f, mods, None, ls, gpre, w_main, w_dt, dt_b, cw, cbias)
    st0 = state_ssd[:, 0].reshape(nbs, 2, SSD_INNER, SSD_STATE)
    (y_sample,) = _ssd_call(us, dts, xs1f, ls, st0, a_neg, d_skip, gnorm, c_wout, mods, None, gfin, False)

    new_cache_ckv = ckv_p.reshape(nbp, 1, lp, KV_RANK)
    new_cache_krope = kr_p.reshape(nbp, 1, lp, QK_ROPE)
    new_state_ssd = st_p.reshape(nbp, 1, 2, SSD_HEADS, SSD_HEAD_DIM, SSD_STATE)
    return (y_prompt.reshape(nbp, lp, d), y_sample.reshape(nbs, ls, d), new_cache_ckv, new_cache_krope,
            new_state_ssd)
```

```python
import functools
import math

import jax
import jax.numpy as jnp
from jax import lax
from jax.experimental import pallas as pl
from jax.experimental.pallas import tpu as pltpu

F32 = jnp.float32
BF16 = jnp.bfloat16

LANES = 128
SUBLANES = 8

D_MODEL = 1024
DEPTH = 2
GRID_W = 64
EPS = 1e-6

MLA_HEADS = 8
Q_RANK = 256
KV_RANK = 128
QK_NOPE = 64
QK_ROPE = 32
V_HEAD = 64
MLA_WIDTH = MLA_HEADS * V_HEAD
ATTN_SCALE = (QK_NOPE + QK_ROPE) ** -0.5
ROPE_THETA = 10000.0
HEAD_SLAB = LANES

CONV_WIDTH = 512
CONV_K = 31
A_GLU0 = 640
A_GATE0 = A_GLU0 + 2 * CONV_WIDTH
A_COLS = A_GATE0 + MLA_WIDTH + CONV_WIDTH

SSD_INNER = 2048
SSD_HEAD_DIM = 64
SSD_HEADS = 32
SSD_GROUPS = 4
SSD_STATE = 128
SSD_CONV_K = 5
SSD_CHUNK = 128
FIN_CHUNK = 512
SSD_CONV_CH = SSD_INNER + 2 * SSD_GROUPS * SSD_STATE
C_MAIN = SSD_INNER + SSD_CONV_CH
C_TILE = 1024
ROW_TILE = 1024
MM_SLABS = 2
A_ROW_TILE = 512

CONV_STRIDE = 4
CONV_BLOCK = SUBLANES * CONV_STRIDE

VMEM_LIMIT = 56 * 1024 * 1024
HI = lax.Precision.HIGHEST
NT = (((1,), (1,)), ((), ()))
LOG2E = math.log2(math.e)


def _dot(a, b):
    return jnp.dot(a, b, preferred_element_type=F32)


def _sigmoid(x):
    return 1.0 / (1.0 + jnp.exp(-x))


def _silu(x):
    return x * _sigmoid(x)


def _rms(x, g):
    return x * lax.rsqrt(jnp.mean(x * x, axis=-1, keepdims=True) + EPS) * g


def _params(n_axes):
    return pltpu.CompilerParams(dimension_semantics=("arbitrary",) * n_axes,
                                vmem_limit_bytes=VMEM_LIMIT)


def _full(shape):
    nd = len(shape)
    return pl.BlockSpec(shape, lambda *_: (0,) * nd)


def _mod_kernel(cond_ref, w_ref, b_ref, o_ref):
    s = _silu(cond_ref[...])
    o_ref[0] = jnp.dot(s, w_ref[0], precision=HI, preferred_element_type=F32) + b_ref[0]


def _mod_call(cond, w_mod, b_mod):
    nrow = cond.shape[0]
    tn = 1024
    return pl.pallas_call(
        _mod_kernel,
        grid=(DEPTH, 3 * D_MODEL // tn),
        in_specs=[pl.BlockSpec((nrow, D_MODEL), lambda l, j: (0, 0)),
                  pl.BlockSpec((1, D_MODEL, tn), lambda l, j: (l, 0, j)),
                  pl.BlockSpec((1, 1, tn), lambda l, j: (l, 0, j))],
        out_specs=pl.BlockSpec((1, nrow, tn), lambda l, j: (l, 0, j)),
        out_shape=jax.ShapeDtypeStruct((DEPTH, nrow, 3 * D_MODEL), F32),
        compiler_params=_params(2),
        name="modulation",
    )(cond, w_mod, b_mod.reshape(DEPTH, 1, 3 * D_MODEL))


def _a_in_kernel(*refs, rope):
    if rope:
        (x_ref, mod_ref, gpre_ref, w_ref, gq_ref, gkv_ref, w1_ref, w2_ref, cos_ref, sin_ref,
         q_ref, kx_ref, cv_ref, sg_ref) = refs
    else:
        (x_ref, mod_ref, gpre_ref, w_ref, gq_ref, gkv_ref, w1_ref,
         q_ref, kx_ref, cv_ref, sg_ref, ckv_ref, kr_ref) = refs
    x = x_ref[...]
    h = _rms(x, gpre_ref[...]) * (1.0 + mod_ref[0, 1:2, :]) + mod_ref[0, 0:1, :]
    hb = h.astype(BF16)

    u0 = _dot(hb, w_ref[:, 0:A_GLU0])
    qn = _rms(u0[:, 0:Q_RANK], gq_ref[...]).astype(BF16)
    ckv = _rms(u0[:, Q_RANK:Q_RANK + KV_RANK], gkv_ref[...])
    kr = u0[:, 384:512]
    qf = _dot(qn, w1_ref[...])
    qscale = ATTN_SCALE * LOG2E
    if rope:
        cos = cos_ref[...]
        sin = sin_ref[...]
        qs = _dot(qn, w2_ref[...])
        for hd in range(MLA_HEADS):
            sl = slice(hd * HEAD_SLAB, (hd + 1) * HEAD_SLAB)
            q_ref[:, sl] = ((qf[:, sl] * cos + qs[:, sl] * sin) * qscale).astype(BF16)
        kr = kr * cos + u0[:, 512:640] * sin
    else:
        q_ref[...] = (qf * qscale).astype(BF16)
        ckv_ref[...] = ckv
        kr_ref[...] = kr[:, 0:QK_ROPE]
    kx_ref[:, 0:KV_RANK] = ckv.astype(BF16)
    kx_ref[:, KV_RANK:2 * KV_RANK] = kr.astype(BF16)

    glu = _dot(hb, w_ref[:, A_GLU0:A_GATE0])
    cv_ref[...] = glu[:, 0:CONV_WIDTH] * _sigmoid(glu[:, CONV_WIDTH:])
    sg_ref[...] = _silu(_dot(hb, w_ref[:, A_GATE0:A_COLS]))


def _a_in_call(x2, mods, mod_row, seq_len, g_pre, w_in, g_q, g_kv, w1, w2, cos_t, sin_t):
    rows = x2.shape[0]
    tr = A_ROW_TILE
    rope = w2 is not None
    per_seq = max(seq_len // tr, 1)
    if mod_row is None:
        mod_map = lambda i: (i // per_seq, 0, 0)
    else:
        mod_map = lambda i: (mod_row, 0, 0)
    row = lambda i: (i, 0)
    in_specs = [pl.BlockSpec((tr, D_MODEL), row),
                pl.BlockSpec((1, 3, D_MODEL), mod_map),
                _full(g_pre.shape), _full(w_in.shape), _full(g_q.shape), _full(g_kv.shape),
                _full(w1.shape)]
    args = [x2, mods, g_pre, w_in, g_q, g_kv, w1]
    out_specs = [pl.BlockSpec((tr, MLA_HEADS * HEAD_SLAB), row),
                 pl.BlockSpec((tr, 2 * KV_RANK), row),
                 pl.BlockSpec((tr, CONV_WIDTH), row),
                 pl.BlockSpec((tr, MLA_WIDTH + CONV_WIDTH), row)]
    out_shape = [jax.ShapeDtypeStruct((rows, MLA_HEADS * HEAD_SLAB), BF16),
                 jax.ShapeDtypeStruct((rows, 2 * KV_RANK), BF16),
                 jax.ShapeDtypeStruct((rows, CONV_WIDTH), F32),
                 jax.ShapeDtypeStruct((rows, MLA_WIDTH + CONV_WIDTH), F32)]
    if rope:
        in_specs += [_full(w2.shape),
                     pl.BlockSpec((tr, HEAD_SLAB), lambda i: (i % per_seq, 0)),
                     pl.BlockSpec((tr, HEAD_SLAB), lambda i: (i % per_seq, 0))]
        args += [w2, cos_t, sin_t]
    else:
        out_specs += [pl.BlockSpec((tr, KV_RANK), row), pl.BlockSpec((tr, QK_ROPE), row)]
        out_shape += [jax.ShapeDtypeStruct((rows, KV_RANK), F32),
                      jax.ShapeDtypeStruct((rows, QK_ROPE), F32)]
    return pl.pallas_call(
        functools.partial(_a_in_kernel, rope=rope),
        grid=(rows // tr,),
        in_specs=in_specs, out_specs=out_specs, out_shape=out_shape,
        compiler_params=_params(1),
        name="a_in_rope" if rope else "a_in",
    )(*args)


def _attn_kernel(*refs, n_ctx, seq_len):
    if n_ctx:
        q_ref, kxn_ref, kxc_ref, wk_ref, wvt_ref, o_ref, k_scr, vt_scr, ot_scr, s_scr = refs
    else:
        q_ref, kxn_ref, wk_ref, wvt_ref, o_ref, k_scr, vt_scr, ot_scr, s_scr = refs

    @pl.when(pl.program_id(1) == 0)
    def _():
        def fill(kx, lo, n):
            k_scr[lo:lo + n, :] = _dot(kx, wk_ref[...]).astype(BF16)
            vt_scr[:, lo:lo + n] = lax.dot_general(
                wvt_ref[...], kx[:, 0:KV_RANK], NT, preferred_element_type=F32).astype(BF16)
        if n_ctx:
            fill(kxc_ref[0], 0, n_ctx)
        fill(kxn_ref[0], n_ctx, seq_len)

    def scores(hd):
        sl = slice(hd * HEAD_SLAB, (hd + 1) * HEAD_SLAB)
        s_scr[hd % 2] = lax.dot_general(k_scr[:, sl], q_ref[0, :, sl], NT, preferred_element_type=F32)

    scores(0)
    for hd in range(MLA_HEADS):
        vrows = slice(hd * V_HEAD, (hd + 1) * V_HEAD)
        if hd + 1 < MLA_HEADS:
            scores(hd + 1)
        st = s_scr[hd % 2]
        p = jnp.exp2(st - jnp.max(st, axis=0, keepdims=True))
        den = jnp.sum(p, axis=0, keepdims=True)
        ot_scr[vrows, :] = _dot(vt_scr[vrows, :], p.astype(BF16)) / den
    o_ref[0] = ot_scr[...].T


def _attn_call(q3, kx_new, kx_ctx, w_k, w_vt):
    nb, seq_len, _ = q3.shape
    n_ctx = 0 if kx_ctx is None else kx_ctx.shape[1]
    tq = min(seq_len, 256)
    lk = n_ctx + seq_len
    in_specs = [pl.BlockSpec((1, tq, MLA_HEADS * HEAD_SLAB), lambda b, i: (b, i, 0)),
                pl.BlockSpec((1, seq_len, 2 * KV_RANK), lambda b, i: (b, 0, 0))]
    args = [q3, kx_new]
    if n_ctx:
        in_specs.append(pl.BlockSpec((1, n_ctx, 2 * KV_RANK), lambda b, i: (b, 0, 0)))
        args.append(kx_ctx)
    in_specs += [_full(w_k.shape), _full(w_vt.shape)]
    args += [w_k, w_vt]
    return pl.pallas_call(
        functools.partial(_attn_kernel, n_ctx=n_ctx, seq_len=seq_len),
        grid=(nb, seq_len // tq),
        in_specs=in_specs,
        out_specs=pl.BlockSpec((1, tq, MLA_WIDTH), lambda b, i: (b, i, 0)),
        out_shape=jax.ShapeDtypeStruct((nb, seq_len, MLA_WIDTH), F32),
        scratch_shapes=[pltpu.VMEM((lk, MLA_HEADS * HEAD_SLAB), BF16),
                        pltpu.VMEM((MLA_WIDTH, lk), BF16),
                        pltpu.VMEM((MLA_WIDTH, tq), F32),
                        pltpu.VMEM((2, lk, tq), F32)],
        compiler_params=_params(2),
        name="attn_ctx" if n_ctx else "attn",
    )(*args)


def _a_out_kernel(attn_ref, cv_ref, sg_ref, x_ref, mod_ref, cw_ref, cb_ref, lng_ref, lnb_ref, wout_ref,
                  o_ref, pad_scr, cvn_scr, *, seq_len):
    halo = 16
    n_slab = CONV_WIDTH // LANES
    zeros = jnp.zeros((halo, LANES), F32)
    for sl in range(n_slab):
        pad_scr[sl, 0:halo, :] = zeros
        pad_scr[sl, halo + seq_len:2 * halo + seq_len, :] = zeros
        pad_scr[sl, halo:halo + seq_len, :] = cv_ref[0, :, sl * LANES:(sl + 1) * LANES]

    def block(r, carry):
        base = r * CONV_BLOCK
        for sl in range(n_slab):
            lanes = slice(sl * LANES, (sl + 1) * LANES)
            accs = [None] * CONV_STRIDE
            for k in range(CONV_K):
                w = jnp.broadcast_to(cw_ref[k:k + 1, lanes], (SUBLANES, LANES))
                for ph in range(CONV_STRIDE):
                    start = base + (halo - CONV_K // 2 + ph + k)
                    v = pad_scr[sl, pl.ds(start, SUBLANES, stride=CONV_STRIDE), :] * w
                    accs[ph] = v if accs[ph] is None else accs[ph] + v
            for ph in range(CONV_STRIDE):
                cvn_scr[sl, pl.ds(base + ph, SUBLANES, stride=CONV_STRIDE), :] = accs[ph]
        return carry

    lax.fori_loop(0, seq_len // CONV_BLOCK, block, 0)
    conv = jnp.concatenate([cvn_scr[sl] for sl in range(n_slab)], axis=1) + cb_ref[...]
    mu = jnp.mean(conv, axis=-1, keepdims=True)
    cen = conv - mu
    ln = cen * lax.rsqrt(jnp.mean(cen * cen, axis=-1, keepdims=True) + EPS) * lng_ref[...] + lnb_ref[...]
    sg = sg_ref[0]
    mix = jnp.concatenate([attn_ref[0] * sg[:, 0:MLA_WIDTH], _silu(ln) * sg[:, MLA_WIDTH:]], axis=1)
    out = _dot(mix.astype(BF16), wout_ref[...])
    o_ref[0] = x_ref[0] + mod_ref[0, 2:3, :] * out


def _a_out_call(attn3, cv3, sg3, x3, mods, mod_row, conv_w, conv_b, ln_g, ln_b, w_out):
    nb, seq_len, _ = x3.shape
    blk = lambda c: pl.BlockSpec((1, seq_len, c), lambda b: (b, 0, 0))
    mod_map = (lambda b: (b, 0, 0)) if mod_row is None else (lambda b: (mod_row, 0, 0))
    n_slab = CONV_WIDTH // LANES
    return pl.pallas_call(
        functools.partial(_a_out_kernel, seq_len=seq_len),
        grid=(nb,),
        in_specs=[blk(MLA_WIDTH), blk(CONV_WIDTH), blk(MLA_WIDTH + CONV_WIDTH), blk(D_MODEL),
                  pl.BlockSpec((1, 3, D_MODEL), mod_map),
                  _full(conv_w.shape), _full(conv_b.shape), _full(ln_g.shape), _full(ln_b.shape),
                  _full(w_out.shape)],
        out_specs=blk(D_MODEL),
        out_shape=jax.ShapeDtypeStruct((nb, seq_len, D_MODEL), F32),
        scratch_shapes=[pltpu.VMEM((n_slab, seq_len + 32, LANES), F32),
                        pltpu.VMEM((n_slab, seq_len, LANES), F32)],
        compiler_params=_params(1),
        name="a_out",
    )(attn3, cv3, sg3, x3, mods, conv_w, conv_b, ln_g, ln_b, w_out)


def _softplus(x):
    return jnp.maximum(x, 0.0) + jnp.log1p(jnp.exp(-jnp.abs(x)))


def _c_in_kernel(x_ref, mod_ref, gpre_ref, w_ref, wdt_ref, dtb_ref, cw_ref, cb_ref,
                 u_ref, dt_ref, h_scr, pad0_scr, pad1_scr, *, seq_len, n_seq):
    j = pl.program_id(1)
    gap = SUBLANES
    pitch = seq_len + gap
    n_slab = C_TILE // LANES
    n_z = SSD_INNER // C_TILE
    n_tiles = C_MAIN // C_TILE
    pads = (pad0_scr, pad1_scr)

    @pl.when(j == 0)
    def _():
        h = _rms(x_ref[...], gpre_ref[...]) * (1.0 + mod_ref[0, 1:2, :]) + mod_ref[0, 0:1, :]
        h_scr[...] = h.astype(BF16)
        dt_ref[...] = _softplus(_dot(h_scr[...], wdt_ref[...]) + dtb_ref[...])

    def project(slabs=range(C_TILE // LANES)):
        return _dot(h_scr[...], w_ref[:, slabs[0] * LANES:(slabs[-1] + 1) * LANES])

    def stage(u, buf, slabs=range(C_TILE // LANES)):
        zeros = jnp.zeros((gap, LANES), F32)
        pad = pads[buf]
        for i, sl in enumerate(slabs):
            pad[sl, 0:gap, :] = zeros
            for s in range(n_seq):
                lo = gap + s * pitch
                pad[sl, lo:lo + seq_len, :] = u[s * seq_len:(s + 1) * seq_len, i * LANES:(i + 1) * LANES]
                pad[sl, lo + seq_len:lo + pitch, :] = zeros

    def conv(buf, slabs=range(C_TILE // LANES)):
        pad = pads[buf]
        for sl in slabs:
            lanes = slice(sl * LANES, (sl + 1) * LANES)
            ws = [jnp.broadcast_to(cw_ref[k:k + 1, lanes], (SUBLANES, LANES)) for k in range(SSD_CONV_K)]
            bias = jnp.broadcast_to(cb_ref[:, lanes], (SUBLANES, LANES))
            for s in range(n_seq):
                for r in range(seq_len // CONV_BLOCK):
                    for ph in range(CONV_STRIDE):
                        acc = bias
                        for k in range(SSD_CONV_K):
                            start = r * CONV_BLOCK + gap + s * pitch - SSD_CONV_K // 2 + ph + k
                            acc = acc + pad[sl, pl.ds(start, SUBLANES, stride=CONV_STRIDE), :] * ws[k]
                        row0 = r * CONV_BLOCK + s * seq_len + ph
                        u_ref[sl, pl.ds(row0, SUBLANES, stride=CONV_STRIDE), :] = _silu(acc)

    @pl.when(j < n_z)
    def _():
        u = project()
        for sl in range(n_slab):
            u_ref[sl] = u[:, sl * LANES:(sl + 1) * LANES]

    @pl.when(j == n_z)
    def _():
        stage(project(), 0)

    for tile in range(n_z + 1, n_tiles):
        @pl.when(j == tile)
        def _(tile=tile):
            buf = (tile - n_z) % 2
            for part in range(n_slab // MM_SLABS):
                slabs = range(part * MM_SLABS, (part + 1) * MM_SLABS)
                stage(project(slabs), buf, slabs)
                conv(1 - buf, slabs)

    @pl.when(j == n_tiles)
    def _():
        conv((n_tiles - 1 - n_z) % 2)


def _c_in_call(x2, mods, mod_row, seq_len, g_pre, w_main, w_dt, dt_b, conv_w, conv_b):
    rows = x2.shape[0]
    tr = ROW_TILE
    n_seq = tr // seq_len
    n_z = SSD_INNER // C_TILE
    n_slab = C_TILE // LANES
    n_tiles = C_MAIN // C_TILE
    mod_map = (lambda i, j: (i, 0, 0)) if mod_row is None else (lambda i, j: (mod_row, 0, 0))
    conv_map = lambda i, j: (0, jnp.clip(j - 1 - n_z, 0, n_tiles - 1 - n_z))
    out_map = lambda i, j: (j - (j >= n_z).astype(jnp.int32), i, 0)
    return pl.pallas_call(
        functools.partial(_c_in_kernel, seq_len=seq_len, n_seq=n_seq),
        grid=(rows // tr, n_tiles + 1),
        in_specs=[pl.BlockSpec((tr, D_MODEL), lambda i, j: (i, 0)),
                  pl.BlockSpec((1, 3, D_MODEL), mod_map),
                  _full(g_pre.shape),
                  pl.BlockSpec((D_MODEL, C_TILE), lambda i, j: (0, jnp.minimum(j, n_tiles - 1))),
                  _full(w_dt.shape), _full(dt_b.shape),
                  pl.BlockSpec((SSD_CONV_K, C_TILE), conv_map),
                  pl.BlockSpec((1, C_TILE), conv_map)],
        out_specs=[pl.BlockSpec((n_slab, tr, LANES), out_map),
                   pl.BlockSpec((tr, 2 * HEAD_SLAB), lambda i, j: (i, 0))],
        out_shape=[jax.ShapeDtypeStruct((C_MAIN // LANES, rows, LANES), F32),
                   jax.ShapeDtypeStruct((rows, 2 * HEAD_SLAB), F32)],
        scratch_shapes=[pltpu.VMEM((tr, D_MODEL), BF16),
                        pltpu.VMEM((n_slab, SUBLANES + n_seq * (seq_len + SUBLANES), LANES), F32),
                        pltpu.VMEM((n_slab, SUBLANES + n_seq * (seq_len + SUBLANES), LANES), F32)],
        compiler_params=_params(2),
        name="c_in",
    )(x2, mods, g_pre, w_main, w_dt, dt_b, conv_w, conv_b)


def _split3(a):
    hi = a.astype(BF16)
    r1 = a - hi.astype(F32)
    mid = r1.astype(BF16)
    lo = (r1 - mid.astype(F32)).astype(BF16)
    return hi, mid, lo


def _ssd_direction(d, xs_ref, b_ref, c_ref, dt_ref, aneg_ref, st_scr, xw_scr, xt_scr, xtb_scr, ybuf_scr, slot):
    q = SSD_CHUNK
    for p in range(SSD_INNER // LANES):
        xp_t = xs_ref[p].T
        xt_scr[d, p * LANES:(p + 1) * LANES, :] = xp_t
        xtb_scr[d, p * LANES:(p + 1) * LANES, :] = xp_t.astype(BF16)

    dt = dt_ref[...]
    a = dt * aneg_ref[d]
    ri = lax.broadcasted_iota(jnp.int32, (q, q), 0)
    ci = lax.broadcasted_iota(jnp.int32, (q, q), 1)
    keep_t = (ri <= ci) if d == 0 else (ri >= ci)
    keep_n = (ci <= ri) if d == 0 else (ci >= ri)
    pieces = jnp.concatenate(_split3(a), axis=1)
    c3 = _dot(jnp.where(keep_n, 1.0, 0.0).astype(BF16), pieces)
    cum = c3[:, 0:LANES] + c3[:, LANES:2 * LANES] + c3[:, 2 * LANES:3 * LANES]
    cum_t = cum.T
    dt_t = dt.T
    total_t = jnp.sum(a.T, axis=1, keepdims=True)
    ecum_t = jnp.exp(cum_t)
    wgt_t = jnp.exp(total_t - cum_t) * dt_t
    cdec = jnp.exp(total_t)
    neg_inf = jnp.float32(-jnp.inf)

    def row_bf16(v, e, n):
        return jnp.broadcast_to(v[e:e + 1, :], (n, q)).astype(BF16)

    hpg = SSD_HEADS // SSD_GROUPS
    gw = hpg * SSD_HEAD_DIM
    for g in range(SSD_GROUPS):
        grows = slice(g * gw, (g + 1) * gw)
        bg = b_ref[g].astype(BF16)
        cg32 = c_ref[g]
        cb_t = lax.dot_general(bg, cg32.astype(BF16), NT, preferred_element_type=F32)
        cg_t = cg32.T.astype(BF16)
        stg = st_scr[d, grows, :]
        stg_b = stg.astype(BF16)
        decs = []
        for eh in range(hpg):
            e = g * hpg + eh
            rows = slice(e * SSD_HEAD_DIM, (e + 1) * SSD_HEAD_DIM)
            row = jnp.broadcast_to(cum_t[e:e + 1, :], (q, q))
            seg = row - row.T
            m_t = (cb_t * jnp.exp(jnp.where(keep_t, seg, neg_inf))).astype(BF16)
            xe = xtb_scr[d, rows, :]
            lhs = jnp.concatenate([xe * row_bf16(dt_t, e, SSD_HEAD_DIM),
                                   stg_b[eh * SSD_HEAD_DIM:(eh + 1) * SSD_HEAD_DIM, :]], axis=1)
            rhs = jnp.concatenate([m_t, cg_t * row_bf16(ecum_t, e, SSD_STATE)], axis=0)
            ybuf_scr[slot, rows, :] = _dot(lhs, rhs)
            xw_scr[d, rows, :] = xe * row_bf16(wgt_t, e, SSD_HEAD_DIM)
            decs.append(jnp.broadcast_to(cdec[e:e + 1, :], (SSD_HEAD_DIM, SSD_STATE)))
        new = _dot(xw_scr[d, grows, :], bg)
        st_scr[d, grows, :] = stg * jnp.concatenate(decs, axis=0) + new


def _ssd_finish(c, first_slot, last_slot, xt_ref, z_ref, x_ref, dskip_ref, gnorm_ref, wout_ref, gate, gfin_ref,
                ybuf_scr, y_ref):
    q = SSD_CHUNK
    acc = None
    ss = jnp.zeros((1, q), F32)
    slabs_per = FIN_CHUNK // LANES
    for k in range(SSD_INNER // FIN_CHUNK):
        rows = slice(k * FIN_CHUNK, (k + 1) * FIN_CHUNK)
        ytot = ybuf_scr[first_slot, rows, :] + ybuf_scr[last_slot, rows, :] + dskip_ref[rows, :] * xt_ref[rows, :]
        zt = jnp.concatenate([z_ref[p].T for p in range(k * slabs_per, (k + 1) * slabs_per)], axis=0)
        gated = ytot * _silu(zt)
        ss = ss + jnp.sum(gated * gated, axis=0, keepdims=True)
        part = _dot((gated * gnorm_ref[rows, :]).T.astype(BF16), wout_ref[rows, :])
        acc = part if acc is None else acc + part
    inv = lax.rsqrt(ss * (1.0 / SSD_INNER) + EPS)
    inv_col = jnp.broadcast_to(inv, (q, q)).T
    scale = jnp.concatenate([inv_col] * (D_MODEL // LANES), axis=1)
    xn = x_ref[...] + gate * (acc * scale)
    y_ref[pl.ds(pl.multiple_of(c * q, q), q), :] = _rms(xn, gfin_ref[...])


def _ssd_kernel(*refs, n_chunks, has_init, emit_state):
    refs = list(refs)
    xs_refs, b_refs, c_refs, dt_refs, z_refs, x_refs = (refs[0:2], refs[2:4], refs[4:6], refs[6:8], refs[8:10],
                                                        refs[10:12])
    refs = refs[12:]
    st0_ref = refs.pop(0) if has_init else None
    aneg_ref, dskip_ref, gnorm_ref, wout_ref, mod_ref, gfin_ref, y_ref = refs[:7]
    refs = refs[7:]
    stout_ref = refs.pop(0) if emit_state else None
    st_scr, ybuf_scr, xw_scr, xt_scr, xtb_scr = refs

    t = pl.program_id(1)
    second = t >= n_chunks // 2
    chunks = (t, n_chunks - 1 - t)

    @pl.when(t == 0)
    def _():
        if has_init:
            st_scr[...] = st0_ref[0]
        else:
            st_scr[...] = jnp.zeros(st_scr.shape, F32)

    for d in range(2):
        slot = jnp.where(second, n_chunks + d, chunks[d])
        _ssd_direction(d, xs_refs[d], b_refs[d], c_refs[d], dt_refs[d], aneg_ref, st_scr, xw_scr, xt_scr, xtb_scr,
                       ybuf_scr, slot)

    if emit_state:
        @pl.when(t == n_chunks - 1)
        def _():
            stout_ref[0] = st_scr[...]

    @pl.when(second)
    def _():
        for d in range(2):
            _ssd_finish(chunks[d], chunks[d], n_chunks + d, xt_scr.at[d], z_refs[d], x_refs[d], dskip_ref,
                        gnorm_ref, wout_ref, mod_ref[0, 2:3, :], gfin_ref, ybuf_scr, y_ref)


def _ssd_call(u3, dt2, x2, seq_len, st0, a_neg, d_skip, g_norm, w_out, mods, mod_row, g_final, emit_state):
    rows = x2.shape[0]
    q = SSD_CHUNK
    nc = seq_len // q
    half = nc // 2
    nb = rows // seq_len
    has_init = st0 is not None
    n_slab = SSD_INNER // LANES
    bslab = 2 * n_slab // SSD_GROUPS

    fwd = lambda b, t: b * nc + t
    bwd = lambda b, t: b * nc + nc - 1 - t
    fwd_fin = lambda b, t: b * nc + jnp.maximum(t, half)
    bwd_fin = lambda b, t: b * nc + jnp.minimum(nc - 1 - t, half - 1)
    mod_map = (lambda b, t: (b, 0, 0)) if mod_row is None else (lambda b, t: (mod_row, 0, 0))

    def pair(block, make_map):
        return [pl.BlockSpec(block, make_map(fwd)), pl.BlockSpec(block, make_map(bwd))]

    in_specs = (pair((n_slab, q, LANES), lambda ch: (lambda b, t: (1, ch(b, t), 0)))
                + pair((SSD_GROUPS, q, LANES), lambda ch: (lambda b, t: (bslab, ch(b, t), 0)))
                + pair((SSD_GROUPS, q, LANES), lambda ch: (lambda b, t: (bslab + 1, ch(b, t), 0)))
                + [pl.BlockSpec((q, HEAD_SLAB), lambda b, t: (fwd(b, t), 0)),
                   pl.BlockSpec((q, HEAD_SLAB), lambda b, t: (bwd(b, t), 1)),
                   pl.BlockSpec((n_slab, q, LANES), lambda b, t: (0, fwd_fin(b, t), 0)),
                   pl.BlockSpec((n_slab, q, LANES), lambda b, t: (0, bwd_fin(b, t), 0)),
                   pl.BlockSpec((q, D_MODEL), lambda b, t: (fwd_fin(b, t), 0)),
                   pl.BlockSpec((q, D_MODEL), lambda b, t: (bwd_fin(b, t), 0))])
    args = [u3] * 6 + [dt2, dt2, u3, u3, x2, x2]
    if has_init:
        in_specs.append(pl.BlockSpec((1, 2, SSD_INNER, SSD_STATE), lambda b, t: (b, 0, 0, 0)))
        args.append(st0)
    in_specs += [_full(a_neg.shape), _full(d_skip.shape), _full(g_norm.shape), _full(w_out.shape),
                 pl.BlockSpec((1, 3, D_MODEL), mod_map), _full(g_final.shape)]
    args += [a_neg, d_skip, g_norm, w_out, mods, g_final]
    out_specs = [pl.BlockSpec((seq_len, D_MODEL), lambda b, t: (b, 0))]
    out_shape = [jax.ShapeDtypeStruct((rows, D_MODEL), F32)]
    if emit_state:
        out_specs.append(pl.BlockSpec((1, 2, SSD_INNER, SSD_STATE), lambda b, t: (b, 0, 0, 0)))
        out_shape.append(jax.ShapeDtypeStruct((nb, 2, SSD_INNER, SSD_STATE), F32))
    return pl.pallas_call(
        functools.partial(_ssd_kernel, n_chunks=nc, has_init=has_init, emit_state=emit_state),
        grid=(nb, nc),
        in_specs=in_specs, out_specs=out_specs, out_shape=out_shape,
        scratch_shapes=[pltpu.VMEM((2, SSD_INNER, SSD_STATE), F32),
                        pltpu.VMEM((nc + 2, SSD_INNER, q), F32),
                        pltpu.VMEM((2, SSD_INNER, q), BF16),
                        pltpu.VMEM((2, SSD_INNER, q), F32),
                        pltpu.VMEM((2, SSD_INNER, q), BF16)],
        compiler_params=_params(2),
        name="ssd_init" if has_init else "ssd",
    )(*args)


def _rope_tables(length):
    rows = length // GRID_W
    row = jnp.repeat(jnp.arange(rows, dtype=F32), GRID_W)
    col = jnp.tile(jnp.arange(GRID_W, dtype=F32), rows)
    n_freq = QK_ROPE // 4
    inv = jnp.power(ROPE_THETA, -jnp.arange(n_freq, dtype=F32) / n_freq)
    ang = jnp.concatenate([row[:, None] * inv, col[:, None] * inv], axis=-1)
    cos, sin = jnp.cos(ang), jnp.sin(ang)
    cos2 = jnp.repeat(cos, 2, axis=-1)
    sin2 = jnp.stack([-sin, sin], axis=-1).reshape(length, QK_ROPE)
    pad = HEAD_SLAB - QK_ROPE
    return (jnp.concatenate([cos2, jnp.ones((length, pad), F32)], axis=-1),
            jnp.concatenate([sin2, jnp.zeros((length, pad), F32)], axis=-1))


def _prep_layer_a(w_in, w_uq, w_uk, w_uv):
    d = w_in.shape[0]
    swap = jnp.arange(QK_ROPE) ^ 1
    kr = w_in[:, 384:416]
    z96 = jnp.zeros((d, HEAD_SLAB - QK_ROPE), F32)
    w_r = jnp.concatenate([w_in[:, 0:384], kr, z96, kr[:, swap], z96, w_in[:, 416:]], axis=1).astype(BF16)
    uq = w_uq.reshape(Q_RANK, MLA_HEADS, QK_NOPE + QK_ROPE)
    nope, rp = uq[..., :QK_NOPE], uq[..., QK_NOPE:]
    z32 = jnp.zeros((Q_RANK, MLA_HEADS, 32), F32)
    w1 = jnp.concatenate([rp, z32, nope], axis=-1).reshape(Q_RANK, MLA_HEADS * HEAD_SLAB).astype(BF16)
    w2 = jnp.concatenate([rp[..., swap], z32, jnp.zeros_like(nope)], axis=-1)
    w2 = w2.reshape(Q_RANK, MLA_HEADS * HEAD_SLAB).astype(BF16)
    uk = w_uk.reshape(KV_RANK, MLA_HEADS, QK_NOPE)
    top = jnp.concatenate([jnp.zeros((KV_RANK, MLA_HEADS, 64), F32), uk], axis=-1)
    eye = jnp.broadcast_to(jnp.eye(QK_ROPE, dtype=F32)[:, None, :], (QK_ROPE, MLA_HEADS, QK_ROPE))
    mid = jnp.concatenate([eye, jnp.zeros((QK_ROPE, MLA_HEADS, HEAD_SLAB - QK_ROPE), F32)], axis=-1)
    bot = jnp.zeros((KV_RANK - QK_ROPE, MLA_HEADS, HEAD_SLAB), F32)
    w_k = jnp.concatenate([top, mid, bot], axis=0).reshape(2 * KV_RANK, MLA_HEADS * HEAD_SLAB).astype(BF16)
    return w_r, w1, w2, w_k, w_uv.T.astype(BF16)


def kernel(x_prompt, x_sample, cache_ckv, cache_krope, state_ssd, c, c_ctx, w_mod, b_mod, g_pre, g_final,
           a_w_in, a_g_q, a_g_kv, a_w_uq, a_w_uk, a_w_uv, a_conv_w, a_conv_b, a_ln_g, a_ln_b, a_w_out,
           c_w_in, c_conv_w, c_conv_b, c_dt_bias, c_a_log, c_d, c_g_norm, c_w_out):
    nbp, lp, d = x_prompt.shape
    nbs, ls, _ = x_sample.shape
    ctx_row = nbs
    cond = jnp.concatenate([c, c_ctx[None, :], jnp.zeros((16 - nbs - 1, d), F32)], axis=0)
    mods_all = _mod_call(cond, w_mod, b_mod).reshape(DEPTH, 16, 3, d)
    row2 = lambda v: v.reshape(1, -1)

    mods = mods_all[0]
    w_r, w1, w2, w_k, w_vt = _prep_layer_a(a_w_in[0], a_w_uq[0], a_w_uk[0], a_w_uv[0])
    cos_t, sin_t = _rope_tables(ls)
    gpre = row2(g_pre[0])
    gq, gkv = row2(a_g_q[0]), row2(a_g_kv[0])
    a_wout = a_w_out[0].astype(BF16)
    conv_args = (a_conv_w[0], row2(a_conv_b[0]), row2(a_ln_g[0]), row2(a_ln_b[0]), a_wout)

    qp, kxp, cvp, sgp, ckv_p, kr_p = _a_in_call(x_prompt.reshape(nbp * lp, d), mods, ctx_row, lp, gpre,
                                                w_r, gq, gkv, w1, None, None, None)
    attn_p = _attn_call(qp.reshape(nbp, lp, -1), kxp.reshape(nbp, lp, -1), None, w_k, w_vt)
    xp1 = _a_out_call(attn_p, cvp.reshape(nbp, lp, -1), sgp.reshape(nbp, lp, -1), x_prompt, mods, ctx_row,
                      *conv_args)

    qs, kxs, cvs, sgs = _a_in_call(x_sample.reshape(nbs * ls, d), mods, None, ls, gpre,
                                   w_r, gq, gkv, w1, w2, cos_t, sin_t)
    n_ctx = cache_ckv.shape[2]
    kx_ctx = jnp.concatenate([cache_ckv[:, 0], cache_krope[:, 0],
                              jnp.zeros((nbs, n_ctx, KV_RANK - QK_ROPE), F32)], axis=-1).astype(BF16)
    attn_s = _attn_call(qs.reshape(nbs, ls, -1), kxs.reshape(nbs, ls, -1), kx_ctx, w_k, w_vt)
    xs1 = _a_out_call(attn_s, cvs.reshape(nbs, ls, -1), sgs.reshape(nbs, ls, -1), x_sample, mods, None,
                      *conv_args)

    mods = mods_all[1]
    gpre = row2(g_pre[1])
    w_in = c_w_in[0]
    w_main = w_in[:, 0:C_MAIN].astype(BF16)
    dtw = w_in[:, C_MAIN:]
    zpad = jnp.zeros((d, HEAD_SLAB - SSD_HEADS), F32)
    w_dt = jnp.concatenate([dtw[:, 0:SSD_HEADS], zpad, dtw[:, SSD_HEADS:], zpad], axis=1).astype(BF16)
    pad32 = jnp.zeros((2, HEAD_SLAB - SSD_HEADS), F32)
    dt_b = jnp.concatenate([c_dt_bias[0], pad32], axis=1).reshape(1, 2 * HEAD_SLAB)
    a_neg = jnp.concatenate([-jnp.exp(c_a_log[0]), pad32], axis=1).reshape(2, 1, HEAD_SLAB)
    lane_rep = lambda v: jnp.broadcast_to(v.reshape(-1, 1), (v.size, LANES))
    d_skip = lane_rep(jnp.repeat(c_d[0], SSD_HEAD_DIM))
    gnorm = lane_rep(c_g_norm[0])
    c_wout = c_w_out[0].astype(BF16)
    gfin = row2(g_final)
    cw, cbias = c_conv_w[0], row2(c_conv_b[0])

    xp1f = xp1.reshape(nbp * lp, d)
    up, dtp = _c_in_call(xp1f, mods, ctx_row, lp, gpre, w_main, w_dt, dt_b, cw, cbias)
    y_prompt, st_p = _ssd_call(up, dtp, xp1f, lp, None, a_neg, d_skip, gnorm, c_wout, mods, ctx_row, gfin, True)
    xs1f = xs1.reshape(nbs * ls, d)
    us, dts = _c_in_call(xs1f, mods, None, ls, gpre, w_main, w_dt, dt_b, cw, cbias)
    st0 = state_ssd[:, 0].reshape(nbs, 2, SSD_INNER, SSD_STATE)
    (y_sample,) = _ssd_call(us, dts, xs1f, ls, st0, a_neg, d_skip, gnorm, c_wout, mods, None, gfin, False)

    new_cache_ckv = ckv_p.reshape(nbp, 1, lp, KV_RANK)
    new_cache_krope = kr_p.reshape(nbp, 1, lp, QK_ROPE)
    new_state_ssd = st_p.reshape(nbp, 1, 2, SSD_HEADS, SSD_HEAD_DIM, SSD_STATE)
    return (y_prompt.reshape(nbp, lp, d), y_sample.reshape(nbs, ls, d), new_cache_ckv, new_cache_krope,
            new_state_ssd)
```

```python
import functools
import math

import jax
import jax.numpy as jnp
from jax import lax
from jax.experimental import pallas as pl
from jax.experimental.pallas import tpu as pltpu

F32 = jnp.float32
BF16 = jnp.bfloat16
ACT = BF16

LANES = 128
SUBLANES = 8

D_MODEL = 1024
DEPTH = 2
GRID_W = 64
EPS = 1e-6

MLA_HEADS = 8
Q_RANK = 256
KV_RANK = 128
QK_NOPE = 64
QK_ROPE = 32
V_HEAD = 64
MLA_WIDTH = MLA_HEADS * V_HEAD
ATTN_SCALE = (QK_NOPE + QK_ROPE) ** -0.5
ROPE_THETA = 10000.0
HEAD_SLAB = LANES

CONV_WIDTH = 512
CONV_K = 31
A_GLU0 = 640
A_GATE0 = A_GLU0 + 2 * CONV_WIDTH
A_COLS = A_GATE0 + MLA_WIDTH + CONV_WIDTH

SSD_INNER = 2048
SSD_HEAD_DIM = 64
SSD_HEADS = 32
SSD_GROUPS = 4
SSD_STATE = 128
SSD_CONV_K = 5
SSD_CHUNK = 128
FIN_CHUNK = 512
SSD_CONV_CH = SSD_INNER + 2 * SSD_GROUPS * SSD_STATE
C_MAIN = SSD_INNER + SSD_CONV_CH
C_TILE = 1024
ROW_TILE = 1024
MM_SLABS = 2
A_ROW_TILE = 512

CONV_STRIDE = 4
CONV_BLOCK = SUBLANES * CONV_STRIDE

VMEM_LIMIT = 56 * 1024 * 1024
HI = lax.Precision.HIGHEST
NT = (((1,), (1,)), ((), ()))
LOG2E = math.log2(math.e)


def _dot(a, b):
    return jnp.dot(a, b, preferred_element_type=F32)


def _sigmoid(x):
    return 1.0 / (1.0 + jnp.exp(-x))


def _silu(x):
    return x * _sigmoid(x)


def _rms(x, g):
    return x * lax.rsqrt(jnp.mean(x * x, axis=-1, keepdims=True) + EPS) * g


def _params(n_axes):
    return pltpu.CompilerParams(dimension_semantics=("arbitrary",) * n_axes,
                                vmem_limit_bytes=VMEM_LIMIT)


def _full(shape):
    nd = len(shape)
    return pl.BlockSpec(shape, lambda *_: (0,) * nd)


def _mod_kernel(cond_ref, w_ref, b_ref, o_ref):
    s = _silu(cond_ref[...])
    o_ref[0] = jnp.dot(s, w_ref[0], precision=HI, preferred_element_type=F32) + b_ref[0]


def _mod_call(cond, w_mod, b_mod):
    nrow = cond.shape[0]
    tn = 1024
    return pl.pallas_call(
        _mod_kernel,
        grid=(DEPTH, 3 * D_MODEL // tn),
        in_specs=[pl.BlockSpec((nrow, D_MODEL), lambda l, j: (0, 0)),
                  pl.BlockSpec((1, D_MODEL, tn), lambda l, j: (l, 0, j)),
                  pl.BlockSpec((1, 1, tn), lambda l, j: (l, 0, j))],
        out_specs=pl.BlockSpec((1, nrow, tn), lambda l, j: (l, 0, j)),
        out_shape=jax.ShapeDtypeStruct((DEPTH, nrow, 3 * D_MODEL), F32),
        compiler_params=_params(2),
        name="modulation",
    )(cond, w_mod, b_mod.reshape(DEPTH, 1, 3 * D_MODEL))


def _a_in_kernel(*refs, rope):
    if rope:
        (x_ref, mod_ref, gpre_ref, w_ref, gq_ref, gkv_ref, w1_ref, w2_ref, cos_ref, sin_ref,
         q_ref, kx_ref, cv_ref, sg_ref) = refs
    else:
        (x_ref, mod_ref, gpre_ref, w_ref, gq_ref, gkv_ref, w1_ref,
         q_ref, kx_ref, cv_ref, sg_ref, ckv_ref, kr_ref) = refs
    x = x_ref[...]
    h = _rms(x, gpre_ref[...]) * (1.0 + mod_ref[0, 1:2, :]) + mod_ref[0, 0:1, :]
    hb = h.astype(BF16)

    u0 = _dot(hb, w_ref[:, 0:A_GLU0])
    qn = _rms(u0[:, 0:Q_RANK], gq_ref[...]).astype(BF16)
    ckv = _rms(u0[:, Q_RANK:Q_RANK + KV_RANK], gkv_ref[...])
    kr = u0[:, 384:512]
    qf = _dot(qn, w1_ref[...])
    qscale = ATTN_SCALE * LOG2E
    if rope:
        cos = cos_ref[...]
        sin = sin_ref[...]
        qs = _dot(qn, w2_ref[...])
        for hd in range(MLA_HEADS):
            sl = slice(hd * HEAD_SLAB, (hd + 1) * HEAD_SLAB)
            q_ref[:, sl] = ((qf[:, sl] * cos + qs[:, sl] * sin) * qscale).astype(BF16)
        kr = kr * cos + u0[:, 512:640] * sin
    else:
        q_ref[...] = (qf * qscale).astype(BF16)
        ckv_ref[...] = ckv
        kr_ref[...] = kr[:, 0:QK_ROPE]
    kx_ref[:, 0:KV_RANK] = ckv.astype(BF16)
    kx_ref[:, KV_RANK:2 * KV_RANK] = kr.astype(BF16)

    glu = _dot(hb, w_ref[:, A_GLU0:A_GATE0])
    cv_ref[...] = (glu[:, 0:CONV_WIDTH] * _sigmoid(glu[:, CONV_WIDTH:])).astype(ACT)
    sg_ref[...] = _silu(_dot(hb, w_ref[:, A_GATE0:A_COLS])).astype(ACT)


def _a_in_call(x2, mods, mod_row, seq_len, g_pre, w_in, g_q, g_kv, w1, w2, cos_t, sin_t):
    rows = x2.shape[0]
    tr = A_ROW_TILE
    rope = w2 is not None
    per_seq = max(seq_len // tr, 1)
    if mod_row is None:
        mod_map = lambda i: (i // per_seq, 0, 0)
    else:
        mod_map = lambda i: (mod_row, 0, 0)
    row = lambda i: (i, 0)
    in_specs = [pl.BlockSpec((tr, D_MODEL), row),
                pl.BlockSpec((1, 3, D_MODEL), mod_map),
                _full(g_pre.shape), _full(w_in.shape), _full(g_q.shape), _full(g_kv.shape),
                _full(w1.shape)]
    args = [x2, mods, g_pre, w_in, g_q, g_kv, w1]
    out_specs = [pl.BlockSpec((tr, MLA_HEADS * HEAD_SLAB), row),
                 pl.BlockSpec((tr, 2 * KV_RANK), row),
                 pl.BlockSpec((tr, CONV_WIDTH), row),
                 pl.BlockSpec((tr, MLA_WIDTH + CONV_WIDTH), row)]
    out_shape = [jax.ShapeDtypeStruct((rows, MLA_HEADS * HEAD_SLAB), BF16),
                 jax.ShapeDtypeStruct((rows, 2 * KV_RANK), BF16),
                 jax.ShapeDtypeStruct((rows, CONV_WIDTH), ACT),
                 jax.ShapeDtypeStruct((rows, MLA_WIDTH + CONV_WIDTH), ACT)]
    if rope:
        in_specs += [_full(w2.shape),
                     pl.BlockSpec((tr, HEAD_SLAB), lambda i: (i % per_seq, 0)),
                     pl.BlockSpec((tr, HEAD_SLAB), lambda i: (i % per_seq, 0))]
        args += [w2, cos_t, sin_t]
    else:
        out_specs += [pl.BlockSpec((tr, KV_RANK), row), pl.BlockSpec((tr, QK_ROPE), row)]
        out_shape += [jax.ShapeDtypeStruct((rows, KV_RANK), F32),
                      jax.ShapeDtypeStruct((rows, QK_ROPE), F32)]
    return pl.pallas_call(
        functools.partial(_a_in_kernel, rope=rope),
        grid=(rows // tr,),
        in_specs=in_specs, out_specs=out_specs, out_shape=out_shape,
        compiler_params=_params(1),
        name="a_in_rope" if rope else "a_in",
    )(*args)


def _attn_kernel(*refs, n_ctx, seq_len):
    if n_ctx:
        q_ref, kxn_ref, kxc_ref, wk_ref, wvt_ref, o_ref, k_scr, vt_scr, ot_scr, s_scr = refs
    else:
        q_ref, kxn_ref, wk_ref, wvt_ref, o_ref, k_scr, vt_scr, ot_scr, s_scr = refs

    @pl.when(pl.program_id(1) == 0)
    def _():
        def fill(kx, lo, n):
            k_scr[lo:lo + n, :] = _dot(kx, wk_ref[...]).astype(BF16)
            vt_scr[:, lo:lo + n] = lax.dot_general(
                wvt_ref[...], kx[:, 0:KV_RANK], NT, preferred_element_type=F32).astype(BF16)
        if n_ctx:
            fill(kxc_ref[0], 0, n_ctx)
        fill(kxn_ref[0], n_ctx, seq_len)

    def scores(hd):
        sl = slice(hd * HEAD_SLAB, (hd + 1) * HEAD_SLAB)
        s_scr[hd % 2] = lax.dot_general(k_scr[:, sl], q_ref[0, :, sl], NT, preferred_element_type=F32)

    scores(0)
    for hd in range(MLA_HEADS):
        vrows = slice(hd * V_HEAD, (hd + 1) * V_HEAD)
        if hd + 1 < MLA_HEADS:
            scores(hd + 1)
        st = s_scr[hd % 2]
        p = jnp.exp2(st - jnp.max(st, axis=0, keepdims=True))
        den = jnp.sum(p, axis=0, keepdims=True)
        ot_scr[vrows, :] = _dot(vt_scr[vrows, :], p.astype(BF16)) / den
    o_ref[0] = ot_scr[...].T.astype(ACT)


def _attn_call(q3, kx_new, kx_ctx, w_k, w_vt):
    nb, seq_len, _ = q3.shape
    n_ctx = 0 if kx_ctx is None else kx_ctx.shape[1]
    tq = min(seq_len, 256)
    lk = n_ctx + seq_len
    in_specs = [pl.BlockSpec((1, tq, MLA_HEADS * HEAD_SLAB), lambda b, i: (b, i, 0)),
                pl.BlockSpec((1, seq_len, 2 * KV_RANK), lambda b, i: (b, 0, 0))]
    args = [q3, kx_new]
    if n_ctx:
        in_specs.append(pl.BlockSpec((1, n_ctx, 2 * KV_RANK), lambda b, i: (b, 0, 0)))
        args.append(kx_ctx)
    in_specs += [_full(w_k.shape), _full(w_vt.shape)]
    args += [w_k, w_vt]
    return pl.pallas_call(
        functools.partial(_attn_kernel, n_ctx=n_ctx, seq_len=seq_len),
        grid=(nb, seq_len // tq),
        in_specs=in_specs,
        out_specs=pl.BlockSpec((1, tq, MLA_WIDTH), lambda b, i: (b, i, 0)),
        out_shape=jax.ShapeDtypeStruct((nb, seq_len, MLA_WIDTH), ACT),
        scratch_shapes=[pltpu.VMEM((lk, MLA_HEADS * HEAD_SLAB), BF16),
                        pltpu.VMEM((MLA_WIDTH, lk), BF16),
                        pltpu.VMEM((MLA_WIDTH, tq), F32),
                        pltpu.VMEM((2, lk, tq), F32)],
        compiler_params=_params(2),
        name="attn_ctx" if n_ctx else "attn",
    )(*args)


def _a_out_kernel(attn_ref, cv_ref, sg_ref, x_ref, mod_ref, cw_ref, cb_ref, lng_ref, lnb_ref, wout_ref,
                  o_ref, pad_scr, cvn_scr, *, seq_len):
    halo = 16
    n_slab = CONV_WIDTH // LANES
    zeros = jnp.zeros((halo, LANES), F32)
    for sl in range(n_slab):
        pad_scr[sl, 0:halo, :] = zeros
        pad_scr[sl, halo + seq_len:2 * halo + seq_len, :] = zeros
        pad_scr[sl, halo:halo + seq_len, :] = cv_ref[0, :, sl * LANES:(sl + 1) * LANES].astype(F32)

    def block(r, carry):
        base = r * CONV_BLOCK
        for sl in range(n_slab):
            lanes = slice(sl * LANES, (sl + 1) * LANES)
            accs = [None] * CONV_STRIDE
            for k in range(CONV_K):
                w = jnp.broadcast_to(cw_ref[k:k + 1, lanes], (SUBLANES, LANES))
                for ph in range(CONV_STRIDE):
                    start = base + (halo - CONV_K // 2 + ph + k)
                    v = pad_scr[sl, pl.ds(start, SUBLANES, stride=CONV_STRIDE), :] * w
                    accs[ph] = v if accs[ph] is None else accs[ph] + v
            for ph in range(CONV_STRIDE):
                cvn_scr[sl, pl.ds(base + ph, SUBLANES, stride=CONV_STRIDE), :] = accs[ph]
        return carry

    lax.fori_loop(0, seq_len // CONV_BLOCK, block, 0)
    conv = jnp.concatenate([cvn_scr[sl] for sl in range(n_slab)], axis=1) + cb_ref[...]
    mu = jnp.mean(conv, axis=-1, keepdims=True)
    cen = conv - mu
    ln = cen * lax.rsqrt(jnp.mean(cen * cen, axis=-1, keepdims=True) + EPS) * lng_ref[...] + lnb_ref[...]
    sg = sg_ref[0].astype(F32)
    mix = jnp.concatenate([attn_ref[0].astype(F32) * sg[:, 0:MLA_WIDTH], _silu(ln) * sg[:, MLA_WIDTH:]], axis=1)
    out = _dot(mix.astype(BF16), wout_ref[...])
    o_ref[0] = x_ref[0] + mod_ref[0, 2:3, :] * out


def _a_out_call(attn3, cv3, sg3, x3, mods, mod_row, conv_w, conv_b, ln_g, ln_b, w_out):
    nb, seq_len, _ = x3.shape
    blk = lambda c: pl.BlockSpec((1, seq_len, c), lambda b: (b, 0, 0))
    mod_map = (lambda b: (b, 0, 0)) if mod_row is None else (lambda b: (mod_row, 0, 0))
    n_slab = CONV_WIDTH // LANES
    return pl.pallas_call(
        functools.partial(_a_out_kernel, seq_len=seq_len),
        grid=(nb,),
        in_specs=[blk(MLA_WIDTH), blk(CONV_WIDTH), blk(MLA_WIDTH + CONV_WIDTH), blk(D_MODEL),
                  pl.BlockSpec((1, 3, D_MODEL), mod_map),
                  _full(conv_w.shape), _full(conv_b.shape), _full(ln_g.shape), _full(ln_b.shape),
                  _full(w_out.shape)],
        out_specs=blk(D_MODEL),
        out_shape=jax.ShapeDtypeStruct((nb, seq_len, D_MODEL), F32),
        scratch_shapes=[pltpu.VMEM((n_slab, seq_len + 32, LANES), F32),
                        pltpu.VMEM((n_slab, seq_len, LANES), F32)],
        compiler_params=_params(1),
        name="a_out",
    )(attn3, cv3, sg3, x3, mods, conv_w, conv_b, ln_g, ln_b, w_out)


def _softplus(x):
    return jnp.maximum(x, 0.0) + jnp.log1p(jnp.exp(-jnp.abs(x)))


def _c_in_kernel(x_ref, mod_ref, gpre_ref, w_ref, wdt_ref, dtb_ref, cw_ref, cb_ref,
                 u_ref, dt_ref, h_scr, pad0_scr, pad1_scr, res_scr, *, seq_len, n_seq):
    j = pl.program_id(1)
    gap = SUBLANES
    pitch = seq_len + gap
    n_slab = C_TILE // LANES
    n_z = SSD_INNER // C_TILE
    n_tiles = C_MAIN // C_TILE
    pads = (pad0_scr, pad1_scr)

    @pl.when(j == 0)
    def _():
        h = _rms(x_ref[...], gpre_ref[...]) * (1.0 + mod_ref[0, 1:2, :]) + mod_ref[0, 0:1, :]
        h_scr[...] = h.astype(BF16)
        dt_ref[...] = _softplus(_dot(h_scr[...], wdt_ref[...]) + dtb_ref[...])

    def project(tile, slabs=range(C_TILE // LANES)):
        return _dot(h_scr[...], w_ref[tile, :, slabs[0] * LANES:(slabs[-1] + 1) * LANES])

    def stage(u, buf, slabs=range(C_TILE // LANES)):
        zeros = jnp.zeros((gap, LANES), F32)
        pad = pads[buf]
        for i, sl in enumerate(slabs):
            pad[sl, 0:gap, :] = zeros
            for s in range(n_seq):
                lo = gap + s * pitch
                pad[sl, lo:lo + seq_len, :] = u[s * seq_len:(s + 1) * seq_len, i * LANES:(i + 1) * LANES]
                pad[sl, lo + seq_len:lo + pitch, :] = zeros

    def conv(buf, slabs=range(C_TILE // LANES)):
        pad = pads[buf]
        for sl in slabs:
            lanes = slice(sl * LANES, (sl + 1) * LANES)
            ws = [jnp.broadcast_to(cw_ref[k:k + 1, lanes], (SUBLANES, LANES)) for k in range(SSD_CONV_K)]
            bias = jnp.broadcast_to(cb_ref[:, lanes], (SUBLANES, LANES))
            for s in range(n_seq):
                for r in range(seq_len // CONV_BLOCK):
                    for ph in range(CONV_STRIDE):
                        acc = bias
                        for k in range(SSD_CONV_K):
                            start = r * CONV_BLOCK + gap + s * pitch - SSD_CONV_K // 2 + ph + k
                            acc = acc + pad[sl, pl.ds(start, SUBLANES, stride=CONV_STRIDE), :] * ws[k]
                        row0 = r * CONV_BLOCK + s * seq_len + ph
                        res_scr[sl, pl.ds(row0, SUBLANES, stride=CONV_STRIDE), :] = _silu(acc)
            u_ref[sl] = res_scr[sl].astype(ACT)

    @pl.when(j < n_z)
    def _():
        u = project(j)
        for sl in range(n_slab):
            u_ref[sl] = u[:, sl * LANES:(sl + 1) * LANES].astype(ACT)

    @pl.when(j == n_z)
    def _():
        stage(project(n_z), 0)

    for tile in range(n_z + 1, n_tiles):
        @pl.when(j == tile)
        def _(tile=tile):
            buf = (tile - n_z) % 2
            for part in range(n_slab // MM_SLABS):
                slabs = range(part * MM_SLABS, (part + 1) * MM_SLABS)
                stage(project(tile, slabs), buf, slabs)
                conv(1 - buf, slabs)

    @pl.when(j == n_tiles)
    def _():
        conv((n_tiles - 1 - n_z) % 2)


def _c_in_call(x2, mods, mod_row, seq_len, g_pre, w_main, w_dt, dt_b, conv_w, conv_b):
    rows = x2.shape[0]
    tr = ROW_TILE
    n_seq = tr // seq_len
    n_z = SSD_INNER // C_TILE
    n_slab = C_TILE // LANES
    n_tiles = C_MAIN // C_TILE
    mod_map = (lambda i, j: (i, 0, 0)) if mod_row is None else (lambda i, j: (mod_row, 0, 0))
    conv_map = lambda i, j: (0, jnp.clip(j - 1 - n_z, 0, n_tiles - 1 - n_z))
    out_map = lambda i, j: (j - (j >= n_z).astype(jnp.int32), i, 0)
    return pl.pallas_call(
        functools.partial(_c_in_kernel, seq_len=seq_len, n_seq=n_seq),
        grid=(rows // tr, n_tiles + 1),
        in_specs=[pl.BlockSpec((tr, D_MODEL), lambda i, j: (i, 0)),
                  pl.BlockSpec((1, 3, D_MODEL), mod_map),
                  _full(g_pre.shape),
                  pl.BlockSpec(w_main.shape, lambda i, j: (0, 0, 0), pipeline_mode=pl.Buffered(1)),
                  _full(w_dt.shape), _full(dt_b.shape),
                  pl.BlockSpec((SSD_CONV_K, C_TILE), conv_map),
                  pl.BlockSpec((1, C_TILE), conv_map)],
        out_specs=[pl.BlockSpec((n_slab, tr, LANES), out_map),
                   pl.BlockSpec((tr, 2 * HEAD_SLAB), lambda i, j: (i, 0))],
        out_shape=[jax.ShapeDtypeStruct((C_MAIN // LANES, rows, LANES), ACT),
                   jax.ShapeDtypeStruct((rows, 2 * HEAD_SLAB), F32)],
        scratch_shapes=[pltpu.VMEM((tr, D_MODEL), BF16),
                        pltpu.VMEM((n_slab, SUBLANES + n_seq * (seq_len + SUBLANES), LANES), F32),
                        pltpu.VMEM((n_slab, SUBLANES + n_seq * (seq_len + SUBLANES), LANES), F32),
                        pltpu.VMEM((n_slab, tr, LANES), F32)],
        compiler_params=_params(2),
        name="c_in",
    )(x2, mods, g_pre, w_main, w_dt, dt_b, conv_w, conv_b)


def _split3(a):
    hi = a.astype(BF16)
    r1 = a - hi.astype(F32)
    mid = r1.astype(BF16)
    lo = (r1 - mid.astype(F32)).astype(BF16)
    return hi, mid, lo


def _ssd_direction(d, xs_ref, b_ref, c_ref, dt_ref, aneg_ref, st_scr, xw_scr, xt_scr, xtb_scr, ybuf_scr, slot):
    q = SSD_CHUNK
    for p in range(SSD_INNER // LANES):
        xp_t = xs_ref[p].astype(F32).T
        xt_scr[d, p * LANES:(p + 1) * LANES, :] = xp_t
        xtb_scr[d, p * LANES:(p + 1) * LANES, :] = xp_t.astype(BF16)

    dt = dt_ref[...]
    a = dt * aneg_ref[d]
    ri = lax.broadcasted_iota(jnp.int32, (q, q), 0)
    ci = lax.broadcasted_iota(jnp.int32, (q, q), 1)
    keep_t = (ri <= ci) if d == 0 else (ri >= ci)
    keep_n = (ci <= ri) if d == 0 else (ci >= ri)
    pieces = jnp.concatenate(_split3(a), axis=1)
    c3 = _dot(jnp.where(keep_n, 1.0, 0.0).astype(BF16), pieces)
    cum = c3[:, 0:LANES] + c3[:, LANES:2 * LANES] + c3[:, 2 * LANES:3 * LANES]
    cum_t = cum.T
    dt_t = dt.T
    total_t = jnp.sum(a.T, axis=1, keepdims=True)
    ecum_t = jnp.exp(cum_t)
    wgt_t = jnp.exp(total_t - cum_t) * dt_t
    cdec = jnp.exp(total_t)
    neg_inf = jnp.float32(-jnp.inf)

    def row_bf16(v, e, n):
        return jnp.broadcast_to(v[e:e + 1, :], (n, q)).astype(BF16)

    hpg = SSD_HEADS // SSD_GROUPS
    gw = hpg * SSD_HEAD_DIM
    for g in range(SSD_GROUPS):
        grows = slice(g * gw, (g + 1) * gw)
        bg = b_ref[g].astype(BF16)
        cg = c_ref[g]
        cb_t = lax.dot_general(bg, cg.astype(BF16), NT, preferred_element_type=F32)
        cg_t = cg.astype(F32).T.astype(BF16)
        stg = st_scr[d, grows, :]
        stg_b = stg.astype(BF16)
        decs = []
        for eh in range(hpg):
            e = g * hpg + eh
            rows = slice(e * SSD_HEAD_DIM, (e + 1) * SSD_HEAD_DIM)
            row = jnp.broadcast_to(cum_t[e:e + 1, :], (q, q))
            seg = row - row.T
            m_t = (cb_t * jnp.exp(jnp.where(keep_t, seg, neg_inf))).astype(BF16)
            xe = xtb_scr[d, rows, :]
            lhs = jnp.concatenate([xe * row_bf16(dt_t, e, SSD_HEAD_DIM),
                                   stg_b[eh * SSD_HEAD_DIM:(eh + 1) * SSD_HEAD_DIM, :]], axis=1)
            rhs = jnp.concatenate([m_t, cg_t * row_bf16(ecum_t, e, SSD_STATE)], axis=0)
            ybuf_scr[slot, rows, :] = _dot(lhs, rhs)
            xw_scr[d, rows, :] = xe * row_bf16(wgt_t, e, SSD_HEAD_DIM)
            decs.append(jnp.broadcast_to(cdec[e:e + 1, :], (SSD_HEAD_DIM, SSD_STATE)))
        new = _dot(xw_scr[d, grows, :], bg)
        st_scr[d, grows, :] = stg * jnp.concatenate(decs, axis=0) + new


def _ssd_finish(c, first_slot, last_slot, xt_ref, z_ref, x_ref, dskip_ref, gnorm_ref, wout_ref, gate, gfin_ref,
                ybuf_scr, y_ref):
    q = SSD_CHUNK
    acc = None
    ss = jnp.zeros((1, q), F32)
    slabs_per = FIN_CHUNK // LANES
    for k in range(SSD_INNER // FIN_CHUNK):
        rows = slice(k * FIN_CHUNK, (k + 1) * FIN_CHUNK)
        ytot = ybuf_scr[first_slot, rows, :] + ybuf_scr[last_slot, rows, :] + dskip_ref[rows, :] * xt_ref[rows, :]
        zt = jnp.concatenate([z_ref[p].astype(F32).T for p in range(k * slabs_per, (k + 1) * slabs_per)], axis=0)
        gated = ytot * _silu(zt)
        ss = ss + jnp.sum(gated * gated, axis=0, keepdims=True)
        part = _dot((gated * gnorm_ref[rows, :]).T.astype(BF16), wout_ref[rows, :])
        acc = part if acc is None else acc + part
    inv = lax.rsqrt(ss * (1.0 / SSD_INNER) + EPS)
    inv_col = jnp.broadcast_to(inv, (q, q)).T
    scale = jnp.concatenate([inv_col] * (D_MODEL // LANES), axis=1)
    xn = x_ref[...] + gate * (acc * scale)
    y_ref[pl.ds(pl.multiple_of(c * q, q), q), :] = _rms(xn, gfin_ref[...])


def _ssd_kernel(*refs, n_chunks, has_init, emit_state):
    refs = list(refs)
    xs_refs, b_refs, c_refs, dt_refs, z_refs, x_refs = (refs[0:2], refs[2:4], refs[4:6], refs[6:8], refs[8:10],
                                                        refs[10:12])
    refs = refs[12:]
    st0_ref = refs.pop(0) if has_init else None
    aneg_ref, dskip_ref, gnorm_ref, wout_ref, mod_ref, gfin_ref, y_ref = refs[:7]
    refs = refs[7:]
    stout_ref = refs.pop(0) if emit_state else None
    st_scr, ybuf_scr, xw_scr, xt_scr, xtb_scr = refs

    t = pl.program_id(1)
    second = t >= n_chunks // 2
    chunks = (t, n_chunks - 1 - t)

    @pl.when(t == 0)
    def _():
        if has_init:
            st_scr[...] = st0_ref[0]
        else:
            st_scr[...] = jnp.zeros(st_scr.shape, F32)

    for d in range(2):
        slot = jnp.where(second, n_chunks + d, chunks[d])
        _ssd_direction(d, xs_refs[d], b_refs[d], c_refs[d], dt_refs[d], aneg_ref, st_scr, xw_scr, xt_scr, xtb_scr,
                       ybuf_scr, slot)

    if emit_state:
        @pl.when(t == n_chunks - 1)
        def _():
            stout_ref[0] = st_scr[...]

    @pl.when(second)
    def _():
        for d in range(2):
            _ssd_finish(chunks[d], chunks[d], n_chunks + d, xt_scr.at[d], z_refs[d], x_refs[d], dskip_ref,
                        gnorm_ref, wout_ref, mod_ref[0, 2:3, :], gfin_ref, ybuf_scr, y_ref)


def _ssd_call(u3, dt2, x2, seq_len, st0, a_neg, d_skip, g_norm, w_out, mods, mod_row, g_final, emit_state):
    rows = x2.shape[0]
    q = SSD_CHUNK
    nc = seq_len // q
    half = nc // 2
    nb = rows // seq_len
    has_init = st0 is not None
    n_slab = SSD_INNER // LANES
    bslab = 2 * n_slab // SSD_GROUPS

    fwd = lambda b, t: b * nc + t
    bwd = lambda b, t: b * nc + nc - 1 - t
    fwd_fin = lambda b, t: b * nc + jnp.maximum(t, half)
    bwd_fin = lambda b, t: b * nc + jnp.minimum(nc - 1 - t, half - 1)
    mod_map = (lambda b, t: (b, 0, 0)) if mod_row is None else (lambda b, t: (mod_row, 0, 0))

    def pair(block, make_map):
        return [pl.BlockSpec(block, make_map(fwd)), pl.BlockSpec(block, make_map(bwd))]

    in_specs = (pair((n_slab, q, LANES), lambda ch: (lambda b, t: (1, ch(b, t), 0)))
                + pair((SSD_GROUPS, q, LANES), lambda ch: (lambda b, t: (bslab, ch(b, t), 0)))
                + pair((SSD_GROUPS, q, LANES), lambda ch: (lambda b, t: (bslab + 1, ch(b, t), 0)))
                + [pl.BlockSpec((q, HEAD_SLAB), lambda b, t: (fwd(b, t), 0)),
                   pl.BlockSpec((q, HEAD_SLAB), lambda b, t: (bwd(b, t), 1)),
                   pl.BlockSpec((n_slab, q, LANES), lambda b, t: (0, fwd_fin(b, t), 0)),
                   pl.BlockSpec((n_slab, q, LANES), lambda b, t: (0, bwd_fin(b, t), 0)),
                   pl.BlockSpec((q, D_MODEL), lambda b, t: (fwd_fin(b, t), 0)),
                   pl.BlockSpec((q, D_MODEL), lambda b, t: (bwd_fin(b, t), 0))])
    args = [u3] * 6 + [dt2, dt2, u3, u3, x2, x2]
    if has_init:
        in_specs.append(pl.BlockSpec((1, 2, SSD_INNER, SSD_STATE), lambda b, t: (b, 0, 0, 0)))
        args.append(st0)
    in_specs += [_full(a_neg.shape), _full(d_skip.shape), _full(g_norm.shape), _full(w_out.shape),
                 pl.BlockSpec((1, 3, D_MODEL), mod_map), _full(g_final.shape)]
    args += [a_neg, d_skip, g_norm, w_out, mods, g_final]
    out_specs = [pl.BlockSpec((seq_len, D_MODEL), lambda b, t: (b, 0))]
    out_shape = [jax.ShapeDtypeStruct((rows, D_MODEL), F32)]
    if emit_state:
        out_specs.append(pl.BlockSpec((1, 2, SSD_INNER, SSD_STATE), lambda b, t: (b, 0, 0, 0)))
        out_shape.append(jax.ShapeDtypeStruct((nb, 2, SSD_INNER, SSD_STATE), F32))
    return pl.pallas_call(
        functools.partial(_ssd_kernel, n_chunks=nc, has_init=has_init, emit_state=emit_state),
        grid=(nb, nc),
        in_specs=in_specs, out_specs=out_specs, out_shape=out_shape,
        scratch_shapes=[pltpu.VMEM((2, SSD_INNER, SSD_STATE), F32),
                        pltpu.VMEM((nc + 2, SSD_INNER, q), F32),
                        pltpu.VMEM((2, SSD_INNER, q), BF16),
                        pltpu.VMEM((2, SSD_INNER, q), F32),
                        pltpu.VMEM((2, SSD_INNER, q), BF16)],
        compiler_params=_params(2),
        name="ssd_init" if has_init else "ssd",
    )(*args)


def _rope_tables(length):
    rows = length // GRID_W
    row = jnp.repeat(jnp.arange(rows, dtype=F32), GRID_W)
    col = jnp.tile(jnp.arange(GRID_W, dtype=F32), rows)
    n_freq = QK_ROPE // 4
    inv = jnp.power(ROPE_THETA, -jnp.arange(n_freq, dtype=F32) / n_freq)
    ang = jnp.concatenate([row[:, None] * inv, col[:, None] * inv], axis=-1)
    cos, sin = jnp.cos(ang), jnp.sin(ang)
    cos2 = jnp.repeat(cos, 2, axis=-1)
    sin2 = jnp.stack([-sin, sin], axis=-1).reshape(length, QK_ROPE)
    pad = HEAD_SLAB - QK_ROPE
    return (jnp.concatenate([cos2, jnp.ones((length, pad), F32)], axis=-1),
            jnp.concatenate([sin2, jnp.zeros((length, pad), F32)], axis=-1))


def _prep_layer_a(w_in, w_uq, w_uk, w_uv):
    d = w_in.shape[0]
    swap = jnp.arange(QK_ROPE) ^ 1
    kr = w_in[:, 384:416]
    z96 = jnp.zeros((d, HEAD_SLAB - QK_ROPE), F32)
    w_r = jnp.concatenate([w_in[:, 0:384], kr, z96, kr[:, swap], z96, w_in[:, 416:]], axis=1).astype(BF16)
    uq = w_uq.reshape(Q_RANK, MLA_HEADS, QK_NOPE + QK_ROPE)
    nope, rp = uq[..., :QK_NOPE], uq[..., QK_NOPE:]
    z32 = jnp.zeros((Q_RANK, MLA_HEADS, 32), F32)
    w1 = jnp.concatenate([rp, z32, nope], axis=-1).reshape(Q_RANK, MLA_HEADS * HEAD_SLAB).astype(BF16)
    w2 = jnp.concatenate([rp[..., swap], z32, jnp.zeros_like(nope)], axis=-1)
    w2 = w2.reshape(Q_RANK, MLA_HEADS * HEAD_SLAB).astype(BF16)
    uk = w_uk.reshape(KV_RANK, MLA_HEADS, QK_NOPE)
    top = jnp.concatenate([jnp.zeros((KV_RANK, MLA_HEADS, 64), F32), uk], axis=-1)
    eye = jnp.broadcast_to(jnp.eye(QK_ROPE, dtype=F32)[:, None, :], (QK_ROPE, MLA_HEADS, QK_ROPE))
    mid = jnp.concatenate([eye, jnp.zeros((QK_ROPE, MLA_HEADS, HEAD_SLAB - QK_ROPE), F32)], axis=-1)
    bot = jnp.zeros((KV_RANK - QK_ROPE, MLA_HEADS, HEAD_SLAB), F32)
    w_k = jnp.concatenate([top, mid, bot], axis=0).reshape(2 * KV_RANK, MLA_HEADS * HEAD_SLAB).astype(BF16)
    return w_r, w1, w2, w_k, w_uv.T.astype(BF16)


def kernel(x_prompt, x_sample, cache_ckv, cache_krope, state_ssd, c, c_ctx, w_mod, b_mod, g_pre, g_final,
           a_w_in, a_g_q, a_g_kv, a_w_uq, a_w_uk, a_w_uv, a_conv_w, a_conv_b, a_ln_g, a_ln_b, a_w_out,
           c_w_in, c_conv_w, c_conv_b, c_dt_bias, c_a_log, c_d, c_g_norm, c_w_out):
    nbp, lp, d = x_prompt.shape
    nbs, ls, _ = x_sample.shape
    ctx_row = nbs
    cond = jnp.concatenate([c, c_ctx[None, :], jnp.zeros((16 - nbs - 1, d), F32)], axis=0)
    mods_all = _mod_call(cond, w_mod, b_mod).reshape(DEPTH, 16, 3, d)
    row2 = lambda v: v.reshape(1, -1)

    mods = mods_all[0]
    w_r, w1, w2, w_k, w_vt = _prep_layer_a(a_w_in[0], a_w_uq[0], a_w_uk[0], a_w_uv[0])
    cos_t, sin_t = _rope_tables(ls)
    gpre = row2(g_pre[0])
    gq, gkv = row2(a_g_q[0]), row2(a_g_kv[0])
    a_wout = a_w_out[0].astype(BF16)
    conv_args = (a_conv_w[0], row2(a_conv_b[0]), row2(a_ln_g[0]), row2(a_ln_b[0]), a_wout)

    qp, kxp, cvp, sgp, ckv_p, kr_p = _a_in_call(x_prompt.reshape(nbp * lp, d), mods, ctx_row, lp, gpre,
                                                w_r, gq, gkv, w1, None, None, None)
    attn_p = _attn_call(qp.reshape(nbp, lp, -1), kxp.reshape(nbp, lp, -1), None, w_k, w_vt)
    xp1 = _a_out_call(attn_p, cvp.reshape(nbp, lp, -1), sgp.reshape(nbp, lp, -1), x_prompt, mods, ctx_row,
                      *conv_args)

    qs, kxs, cvs, sgs = _a_in_call(x_sample.reshape(nbs * ls, d), mods, None, ls, gpre,
                                   w_r, gq, gkv, w1, w2, cos_t, sin_t)
    n_ctx = cache_ckv.shape[2]
    kx_ctx = jnp.concatenate([cache_ckv[:, 0], cache_krope[:, 0],
                              jnp.zeros((nbs, n_ctx, KV_RANK - QK_ROPE), F32)], axis=-1).astype(BF16)
    attn_s = _attn_call(qs.reshape(nbs, ls, -1), kxs.reshape(nbs, ls, -1), kx_ctx, w_k, w_vt)
    xs1 = _a_out_call(attn_s, cvs.reshape(nbs, ls, -1), sgs.reshape(nbs, ls, -1), x_sample, mods, None,
                      *conv_args)

    mods = mods_all[1]
    gpre = row2(g_pre[1])
    w_in = c_w_in[0]
    w_main = w_in[:, 0:C_MAIN].astype(BF16).reshape(d, C_MAIN // C_TILE, C_TILE).transpose(1, 0, 2)
    dtw = w_in[:, C_MAIN:]
    zpad = jnp.zeros((d, HEAD_SLAB - SSD_HEADS), F32)
    w_dt = jnp.concatenate([dtw[:, 0:SSD_HEADS], zpad, dtw[:, SSD_HEADS:], zpad], axis=1).astype(BF16)
    pad32 = jnp.zeros((2, HEAD_SLAB - SSD_HEADS), F32)
    dt_b = jnp.concatenate([c_dt_bias[0], pad32], axis=1).reshape(1, 2 * HEAD_SLAB)
    a_neg = jnp.concatenate([-jnp.exp(c_a_log[0]), pad32], axis=1).reshape(2, 1, HEAD_SLAB)
    lane_rep = lambda v: jnp.broadcast_to(v.reshape(-1, 1), (v.size, LANES))
    d_skip = lane_rep(jnp.repeat(c_d[0], SSD_HEAD_DIM))
    gnorm = lane_rep(c_g_norm[0])
    c_wout = c_w_out[0].astype(BF16)
    gfin = row2(g_final)
    cw, cbias = c_conv_w[0], row2(c_conv_b[0])

    xp1f = xp1.reshape(nbp * lp, d)
    up, dtp = _c_in_call(xp1f, mods, ctx_row, lp, gpre, w_main, w_dt, dt_b, cw, cbias)
    y_prompt, st_p = _ssd_call(up, dtp, xp1f, lp, None, a_neg, d_skip, gnorm, c_wout, mods, ctx_row, gfin, True)
    xs1f = xs1.reshape(nbs * ls, d)
    us, dts = _c_in_call(xs1f, mods, None, ls, gpre, w_main, w_dt, dt_b, cw, cbias)
    st0 = state_ssd[:, 0].reshape(nbs, 2, SSD_INNER, SSD_STATE)
    (y_sample,) = _ssd_call(us, dts, xs1f, ls, st0, a_neg, d_skip, gnorm, c_wout, mods, None, gfin, False)

    new_cache_ckv = ckv_p.reshape(nbp, 1, lp, KV_RANK)
    new_cache_krope = kr_p.reshape(nbp, 1, lp, QK_ROPE)
    new_state_ssd = st_p.reshape(nbp, 1, 2, SSD_HEADS, SSD_HEAD_DIM, SSD_STATE)
    return (y_prompt.reshape(nbp, lp, d), y_sample.reshape(nbs, ls, d), new_cache_ckv, new_cache_krope,
            new_state_ssd)
```

```python
import functools
import math

import jax
import jax.numpy as jnp
from jax import lax
from jax.experimental import pallas as pl
from jax.experimental.pallas import tpu as pltpu

F32 = jnp.float32
BF16 = jnp.bfloat16
ACT = BF16

LANES = 128
SUBLANES = 8

D_MODEL = 1024
DEPTH = 2
GRID_W = 64
EPS = 1e-6

MLA_HEADS = 8
Q_RANK = 256
KV_RANK = 128
QK_NOPE = 64
QK_ROPE = 32
V_HEAD = 64
MLA_WIDTH = MLA_HEADS * V_HEAD
ATTN_SCALE = (QK_NOPE + QK_ROPE) ** -0.5
ROPE_THETA = 10000.0
HEAD_SLAB = LANES

CONV_WIDTH = 512
CONV_K = 31
A_GLU0 = 640
A_GATE0 = A_GLU0 + 2 * CONV_WIDTH
A_COLS = A_GATE0 + MLA_WIDTH + CONV_WIDTH

SSD_INNER = 2048
SSD_HEAD_DIM = 64
SSD_HEADS = 32
SSD_GROUPS = 4
SSD_STATE = 128
SSD_CONV_K = 5
SSD_CHUNK = 128
FIN_CHUNK = 512
SSD_CONV_CH = SSD_INNER + 2 * SSD_GROUPS * SSD_STATE
C_MAIN = SSD_INNER + SSD_CONV_CH
C_TILE = 1024
ROW_TILE = 1024
MM_SLABS = 2
A_ROW_TILE = 512

CONV_STRIDE = 4
CONV_BLOCK = SUBLANES * CONV_STRIDE

VMEM_LIMIT = 56 * 1024 * 1024
HI = lax.Precision.HIGHEST
NT = (((1,), (1,)), ((), ()))
LOG2E = math.log2(math.e)


def _dot(a, b):
    return jnp.dot(a, b, preferred_element_type=F32)


def _sigmoid(x):
    return 1.0 / (1.0 + jnp.exp(-x))


def _silu(x):
    return x * _sigmoid(x)


def _rms(x, g):
    return x * lax.rsqrt(jnp.mean(x * x, axis=-1, keepdims=True) + EPS) * g


def _params(n_axes):
    return pltpu.CompilerParams(dimension_semantics=("arbitrary",) * n_axes,
                                vmem_limit_bytes=VMEM_LIMIT)


def _full(shape):
    nd = len(shape)
    return pl.BlockSpec(shape, lambda *_: (0,) * nd)


def _mod_kernel(cond_ref, w_ref, b_ref, o_ref):
    s = _silu(cond_ref[...])
    o_ref[0] = jnp.dot(s, w_ref[0], precision=HI, preferred_element_type=F32) + b_ref[0]


def _mod_call(cond, w_mod, b_mod):
    nrow = cond.shape[0]
    tn = 1024
    return pl.pallas_call(
        _mod_kernel,
        grid=(DEPTH, 3 * D_MODEL // tn),
        in_specs=[pl.BlockSpec((nrow, D_MODEL), lambda l, j: (0, 0)),
                  pl.BlockSpec((1, D_MODEL, tn), lambda l, j: (l, 0, j)),
                  pl.BlockSpec((1, 1, tn), lambda l, j: (l, 0, j))],
        out_specs=pl.BlockSpec((1, nrow, tn), lambda l, j: (l, 0, j)),
        out_shape=jax.ShapeDtypeStruct((DEPTH, nrow, 3 * D_MODEL), F32),
        compiler_params=_params(2),
        name="modulation",
    )(cond, w_mod, b_mod.reshape(DEPTH, 1, 3 * D_MODEL))


def _a_in_kernel(*refs, rope):
    if rope:
        (x_ref, mod_ref, gpre_ref, w_ref, gq_ref, gkv_ref, w1_ref, w2_ref, cos_ref, sin_ref,
         q_ref, kx_ref, cv_ref, sg_ref) = refs
    else:
        (x_ref, mod_ref, gpre_ref, w_ref, gq_ref, gkv_ref, w1_ref,
         q_ref, kx_ref, cv_ref, sg_ref, ckv_ref, kr_ref) = refs
    x = x_ref[...]
    h = _rms(x, gpre_ref[...]) * (1.0 + mod_ref[0, 1:2, :]) + mod_ref[0, 0:1, :]
    hb = h.astype(BF16)

    u0 = _dot(hb, w_ref[:, 0:A_GLU0])
    qn = _rms(u0[:, 0:Q_RANK], gq_ref[...]).astype(BF16)
    ckv = _rms(u0[:, Q_RANK:Q_RANK + KV_RANK], gkv_ref[...])
    kr = u0[:, 384:512]
    qf = _dot(qn, w1_ref[...])
    qscale = ATTN_SCALE * LOG2E
    if rope:
        cos = cos_ref[...]
        sin = sin_ref[...]
        qs = _dot(qn, w2_ref[...])
        for hd in range(MLA_HEADS):
            sl = slice(hd * HEAD_SLAB, (hd + 1) * HEAD_SLAB)
            q_ref[:, sl] = ((qf[:, sl] * cos + qs[:, sl] * sin) * qscale).astype(BF16)
        kr = kr * cos + u0[:, 512:640] * sin
    else:
        q_ref[...] = (qf * qscale).astype(BF16)
        ckv_ref[...] = ckv
        kr_ref[...] = kr[:, 0:QK_ROPE]
    kx_ref[:, 0:KV_RANK] = ckv.astype(BF16)
    kx_ref[:, KV_RANK:2 * KV_RANK] = kr.astype(BF16)

    glu = _dot(hb, w_ref[:, A_GLU0:A_GATE0])
    cv_ref[...] = (glu[:, 0:CONV_WIDTH] * _sigmoid(glu[:, CONV_WIDTH:])).astype(ACT)
    sg_ref[...] = _silu(_dot(hb, w_ref[:, A_GATE0:A_COLS])).astype(ACT)


def _a_in_call(x2, mods, mod_row, seq_len, g_pre, w_in, g_q, g_kv, w1, w2, cos_t, sin_t):
    rows = x2.shape[0]
    tr = A_ROW_TILE
    rope = w2 is not None
    per_seq = max(seq_len // tr, 1)
    if mod_row is None:
        mod_map = lambda i: (i // per_seq, 0, 0)
    else:
        mod_map = lambda i: (mod_row, 0, 0)
    row = lambda i: (i, 0)
    in_specs = [pl.BlockSpec((tr, D_MODEL), row),
                pl.BlockSpec((1, 3, D_MODEL), mod_map),
                _full(g_pre.shape), _full(w_in.shape), _full(g_q.shape), _full(g_kv.shape),
                _full(w1.shape)]
    args = [x2, mods, g_pre, w_in, g_q, g_kv, w1]
    out_specs = [pl.BlockSpec((tr, MLA_HEADS * HEAD_SLAB), row),
                 pl.BlockSpec((tr, 2 * KV_RANK), row),
                 pl.BlockSpec((tr, CONV_WIDTH), row),
                 pl.BlockSpec((tr, MLA_WIDTH + CONV_WIDTH), row)]
    out_shape = [jax.ShapeDtypeStruct((rows, MLA_HEADS * HEAD_SLAB), BF16),
                 jax.ShapeDtypeStruct((rows, 2 * KV_RANK), BF16),
                 jax.ShapeDtypeStruct((rows, CONV_WIDTH), ACT),
                 jax.ShapeDtypeStruct((rows, MLA_WIDTH + CONV_WIDTH), ACT)]
    if rope:
        in_specs += [_full(w2.shape),
                     pl.BlockSpec((tr, HEAD_SLAB), lambda i: (i % per_seq, 0)),
                     pl.BlockSpec((tr, HEAD_SLAB), lambda i: (i % per_seq, 0))]
        args += [w2, cos_t, sin_t]
    else:
        out_specs += [pl.BlockSpec((tr, KV_RANK), row), pl.BlockSpec((tr, QK_ROPE), row)]
        out_shape += [jax.ShapeDtypeStruct((rows, KV_RANK), F32),
                      jax.ShapeDtypeStruct((rows, QK_ROPE), F32)]
    return pl.pallas_call(
        functools.partial(_a_in_kernel, rope=rope),
        grid=(rows // tr,),
        in_specs=in_specs, out_specs=out_specs, out_shape=out_shape,
        compiler_params=_params(1),
        name="a_in_rope" if rope else "a_in",
    )(*args)


def _attn_kernel(*refs, n_ctx, seq_len):
    if n_ctx:
        q_ref, kxn_ref, kxc_ref, wk_ref, wvt_ref, o_ref, k_scr, vt_scr, ot_scr, s_scr = refs
    else:
        q_ref, kxn_ref, wk_ref, wvt_ref, o_ref, k_scr, vt_scr, ot_scr, s_scr = refs

    @pl.when(pl.program_id(1) == 0)
    def _():
        def fill(kx, lo, n):
            k_scr[lo:lo + n, :] = _dot(kx, wk_ref[...]).astype(BF16)
            vt_scr[:, lo:lo + n] = lax.dot_general(
                wvt_ref[...], kx[:, 0:KV_RANK], NT, preferred_element_type=F32).astype(BF16)
        if n_ctx:
            fill(kxc_ref[0], 0, n_ctx)
        fill(kxn_ref[0], n_ctx, seq_len)

    def scores(hd):
        sl = slice(hd * HEAD_SLAB, (hd + 1) * HEAD_SLAB)
        s_scr[hd % 2] = lax.dot_general(k_scr[:, sl], q_ref[0, :, sl], NT, preferred_element_type=F32)

    scores(0)
    for hd in range(MLA_HEADS):
        vrows = slice(hd * V_HEAD, (hd + 1) * V_HEAD)
        if hd + 1 < MLA_HEADS:
            scores(hd + 1)
        st = s_scr[hd % 2]
        p = jnp.exp2(st - jnp.max(st, axis=0, keepdims=True))
        den = jnp.sum(p, axis=0, keepdims=True)
        ot_scr[vrows, :] = _dot(vt_scr[vrows, :], p.astype(BF16)) / den
    o_ref[0] = ot_scr[...].T.astype(ACT)


def _attn_call(q3, kx_new, kx_ctx, w_k, w_vt):
    nb, seq_len, _ = q3.shape
    n_ctx = 0 if kx_ctx is None else kx_ctx.shape[1]
    tq = min(seq_len, 512)
    lk = n_ctx + seq_len
    in_specs = [pl.BlockSpec((1, tq, MLA_HEADS * HEAD_SLAB), lambda b, i: (b, i, 0)),
                pl.BlockSpec((1, seq_len, 2 * KV_RANK), lambda b, i: (b, 0, 0))]
    args = [q3, kx_new]
    if n_ctx:
        in_specs.append(pl.BlockSpec((1, n_ctx, 2 * KV_RANK), lambda b, i: (b, 0, 0)))
        args.append(kx_ctx)
    in_specs += [_full(w_k.shape), _full(w_vt.shape)]
    args += [w_k, w_vt]
    return pl.pallas_call(
        functools.partial(_attn_kernel, n_ctx=n_ctx, seq_len=seq_len),
        grid=(nb, seq_len // tq),
        in_specs=in_specs,
        out_specs=pl.BlockSpec((1, tq, MLA_WIDTH), lambda b, i: (b, i, 0)),
        out_shape=jax.ShapeDtypeStruct((nb, seq_len, MLA_WIDTH), ACT),
        scratch_shapes=[pltpu.VMEM((lk, MLA_HEADS * HEAD_SLAB), BF16),
                        pltpu.VMEM((MLA_WIDTH, lk), BF16),
                        pltpu.VMEM((MLA_WIDTH, tq), F32),
                        pltpu.VMEM((2, lk, tq), F32)],
        compiler_params=_params(2),
        name="attn_ctx" if n_ctx else "attn",
    )(*args)


def _a_out_kernel(attn_ref, cv_ref, sg_ref, x_ref, mod_ref, cw_ref, cb_ref, lng_ref, lnb_ref, wout_ref,
                  o_ref, pad_scr, cvn_scr, *, seq_len):
    halo = 16
    n_slab = CONV_WIDTH // LANES
    zeros = jnp.zeros((halo, LANES), F32)
    for sl in range(n_slab):
        pad_scr[sl, 0:halo, :] = zeros
        pad_scr[sl, halo + seq_len:2 * halo + seq_len, :] = zeros
        pad_scr[sl, halo:halo + seq_len, :] = cv_ref[0, :, sl * LANES:(sl + 1) * LANES].astype(F32)

    def block(r, carry):
        base = r * CONV_BLOCK
        for sl in range(n_slab):
            lanes = slice(sl * LANES, (sl + 1) * LANES)
            accs = [None] * CONV_STRIDE
            for k in range(CONV_K):
                w = jnp.broadcast_to(cw_ref[k:k + 1, lanes], (SUBLANES, LANES))
                for ph in range(CONV_STRIDE):
                    start = base + (halo - CONV_K // 2 + ph + k)
                    v = pad_scr[sl, pl.ds(start, SUBLANES, stride=CONV_STRIDE), :] * w
                    accs[ph] = v if accs[ph] is None else accs[ph] + v
            for ph in range(CONV_STRIDE):
                cvn_scr[sl, pl.ds(base + ph, SUBLANES, stride=CONV_STRIDE), :] = accs[ph]
        return carry

    lax.fori_loop(0, seq_len // CONV_BLOCK, block, 0)
    conv = jnp.concatenate([cvn_scr[sl] for sl in range(n_slab)], axis=1) + cb_ref[...]
    mu = jnp.mean(conv, axis=-1, keepdims=True)
    cen = conv - mu
    ln = cen * lax.rsqrt(jnp.mean(cen * cen, axis=-1, keepdims=True) + EPS) * lng_ref[...] + lnb_ref[...]
    sg = sg_ref[0].astype(F32)
    mix = jnp.concatenate([attn_ref[0].astype(F32) * sg[:, 0:MLA_WIDTH], _silu(ln) * sg[:, MLA_WIDTH:]], axis=1)
    out = _dot(mix.astype(BF16), wout_ref[...])
    o_ref[0] = x_ref[0] + mod_ref[0, 2:3, :] * out


def _a_out_call(attn3, cv3, sg3, x3, mods, mod_row, conv_w, conv_b, ln_g, ln_b, w_out):
    nb, seq_len, _ = x3.shape
    blk = lambda c: pl.BlockSpec((1, seq_len, c), lambda b: (b, 0, 0))
    mod_map = (lambda b: (b, 0, 0)) if mod_row is None else (lambda b: (mod_row, 0, 0))
    n_slab = CONV_WIDTH // LANES
    return pl.pallas_call(
        functools.partial(_a_out_kernel, seq_len=seq_len),
        grid=(nb,),
        in_specs=[blk(MLA_WIDTH), blk(CONV_WIDTH), blk(MLA_WIDTH + CONV_WIDTH), blk(D_MODEL),
                  pl.BlockSpec((1, 3, D_MODEL), mod_map),
                  _full(conv_w.shape), _full(conv_b.shape), _full(ln_g.shape), _full(ln_b.shape),
                  _full(w_out.shape)],
        out_specs=blk(D_MODEL),
        out_shape=jax.ShapeDtypeStruct((nb, seq_len, D_MODEL), F32),
        scratch_shapes=[pltpu.VMEM((n_slab, seq_len + 32, LANES), F32),
                        pltpu.VMEM((n_slab, seq_len, LANES), F32)],
        compiler_params=_params(1),
        name="a_out",
    )(attn3, cv3, sg3, x3, mods, conv_w, conv_b, ln_g, ln_b, w_out)


def _softplus(x):
    return jnp.maximum(x, 0.0) + jnp.log1p(jnp.exp(-jnp.abs(x)))


def _c_in_kernel(x_ref, mod_ref, gpre_ref, w_ref, wdt_ref, dtb_ref, cw_ref, cb_ref,
                 u_ref, dt_ref, h_scr, pad0_scr, pad1_scr, res_scr, *, seq_len, n_seq):
    j = pl.program_id(1)
    gap = SUBLANES
    pitch = seq_len + gap
    n_slab = C_TILE // LANES
    n_z = SSD_INNER // C_TILE
    n_tiles = C_MAIN // C_TILE
    pads = (pad0_scr, pad1_scr)

    @pl.when(j == 0)
    def _():
        h = _rms(x_ref[...], gpre_ref[...]) * (1.0 + mod_ref[0, 1:2, :]) + mod_ref[0, 0:1, :]
        h_scr[...] = h.astype(BF16)
        dt_ref[...] = _softplus(_dot(h_scr[...], wdt_ref[...]) + dtb_ref[...])

    def project(tile, slabs=range(C_TILE // LANES)):
        lo = tile * C_TILE + slabs[0] * LANES
        return _dot(h_scr[...], w_ref[:, lo:lo + len(slabs) * LANES])

    def stage(u, buf, slabs=range(C_TILE // LANES)):
        zeros = jnp.zeros((gap, LANES), F32)
        pad = pads[buf]
        for i, sl in enumerate(slabs):
            pad[sl, 0:gap, :] = zeros
            for s in range(n_seq):
                lo = gap + s * pitch
                pad[sl, lo:lo + seq_len, :] = u[s * seq_len:(s + 1) * seq_len, i * LANES:(i + 1) * LANES]
                pad[sl, lo + seq_len:lo + pitch, :] = zeros

    def conv(buf, slabs=range(C_TILE // LANES)):
        pad = pads[buf]
        for sl in slabs:
            lanes = slice(sl * LANES, (sl + 1) * LANES)
            ws = [jnp.broadcast_to(cw_ref[k:k + 1, lanes], (SUBLANES, LANES)) for k in range(SSD_CONV_K)]
            bias = jnp.broadcast_to(cb_ref[:, lanes], (SUBLANES, LANES))
            for s in range(n_seq):
                for r in range(seq_len // CONV_BLOCK):
                    for ph in range(CONV_STRIDE):
                        acc = bias
                        for k in range(SSD_CONV_K):
                            start = r * CONV_BLOCK + gap + s * pitch - SSD_CONV_K // 2 + ph + k
                            acc = acc + pad[sl, pl.ds(start, SUBLANES, stride=CONV_STRIDE), :] * ws[k]
                        row0 = r * CONV_BLOCK + s * seq_len + ph
                        res_scr[sl, pl.ds(row0, SUBLANES, stride=CONV_STRIDE), :] = _silu(acc)
            u_ref[sl] = res_scr[sl].astype(ACT)

    for tile in range(n_z):
        @pl.when(j == tile)
        def _(tile=tile):
            u = project(tile)
            for sl in range(n_slab):
                u_ref[sl] = u[:, sl * LANES:(sl + 1) * LANES].astype(ACT)

    @pl.when(j == n_z)
    def _():
        stage(project(n_z), 0)

    for tile in range(n_z + 1, n_tiles):
        @pl.when(j == tile)
        def _(tile=tile):
            buf = (tile - n_z) % 2
            for part in range(n_slab // MM_SLABS):
                slabs = range(part * MM_SLABS, (part + 1) * MM_SLABS)
                stage(project(tile, slabs), buf, slabs)
                conv(1 - buf, slabs)

    @pl.when(j == n_tiles)
    def _():
        conv((n_tiles - 1 - n_z) % 2)


def _c_in_call(x2, mods, mod_row, seq_len, g_pre, w_main, w_dt, dt_b, conv_w, conv_b):
    rows = x2.shape[0]
    tr = ROW_TILE
    n_seq = tr // seq_len
    n_z = SSD_INNER // C_TILE
    n_slab = C_TILE // LANES
    n_tiles = C_MAIN // C_TILE
    mod_map = (lambda i, j: (i, 0, 0)) if mod_row is None else (lambda i, j: (mod_row, 0, 0))
    conv_map = lambda i, j: (0, jnp.clip(j - 1 - n_z, 0, n_tiles - 1 - n_z))
    out_map = lambda i, j: (j - (j >= n_z).astype(jnp.int32), i, 0)
    return pl.pallas_call(
        functools.partial(_c_in_kernel, seq_len=seq_len, n_seq=n_seq),
        grid=(rows // tr, n_tiles + 1),
        in_specs=[pl.BlockSpec((tr, D_MODEL), lambda i, j: (i, 0)),
                  pl.BlockSpec((1, 3, D_MODEL), mod_map),
                  _full(g_pre.shape),
                  pl.BlockSpec((D_MODEL, C_MAIN), lambda i, j: (0, 0), pipeline_mode=pl.Buffered(1)),
                  _full(w_dt.shape), _full(dt_b.shape),
                  pl.BlockSpec((SSD_CONV_K, C_TILE), conv_map),
                  pl.BlockSpec((1, C_TILE), conv_map)],
        out_specs=[pl.BlockSpec((n_slab, tr, LANES), out_map),
                   pl.BlockSpec((tr, 2 * HEAD_SLAB), lambda i, j: (i, 0))],
        out_shape=[jax.ShapeDtypeStruct((C_MAIN // LANES, rows, LANES), ACT),
                   jax.ShapeDtypeStruct((rows, 2 * HEAD_SLAB), F32)],
        scratch_shapes=[pltpu.VMEM((tr, D_MODEL), BF16),
                        pltpu.VMEM((n_slab, SUBLANES + n_seq * (seq_len + SUBLANES), LANES), F32),
                        pltpu.VMEM((n_slab, SUBLANES + n_seq * (seq_len + SUBLANES), LANES), F32),
                        pltpu.VMEM((n_slab, tr, LANES), F32)],
        compiler_params=_params(2),
        name="c_in",
    )(x2, mods, g_pre, w_main, w_dt, dt_b, conv_w, conv_b)


def _split3(a):
    hi = a.astype(BF16)
    r1 = a - hi.astype(F32)
    mid = r1.astype(BF16)
    lo = (r1 - mid.astype(F32)).astype(BF16)
    return hi, mid, lo


def _ssd_direction(d, xs_ref, b_ref, c_ref, dt_ref, aneg_ref, st_scr, xw_scr, xt_scr, xtb_scr, ybuf_scr, slot):
    q = SSD_CHUNK
    for p in range(SSD_INNER // LANES):
        xp_t = xs_ref[p].astype(F32).T
        xt_scr[d, p * LANES:(p + 1) * LANES, :] = xp_t
        xtb_scr[d, p * LANES:(p + 1) * LANES, :] = xp_t.astype(BF16)

    dt = dt_ref[...]
    a = dt * aneg_ref[d]
    ri = lax.broadcasted_iota(jnp.int32, (q, q), 0)
    ci = lax.broadcasted_iota(jnp.int32, (q, q), 1)
    keep_t = (ri <= ci) if d == 0 else (ri >= ci)
    keep_n = (ci <= ri) if d == 0 else (ci >= ri)
    pieces = jnp.concatenate(_split3(a), axis=1)
    c3 = _dot(jnp.where(keep_n, 1.0, 0.0).astype(BF16), pieces)
    cum = c3[:, 0:LANES] + c3[:, LANES:2 * LANES] + c3[:, 2 * LANES:3 * LANES]
    cum_t = cum.T
    dt_t = dt.T
    total_t = jnp.sum(a.T, axis=1, keepdims=True)
    ecum_t = jnp.exp(cum_t)
    wgt_t = jnp.exp(total_t - cum_t) * dt_t
    cdec = jnp.exp(total_t)
    neg_inf = jnp.float32(-jnp.inf)

    def row_bf16(v, e, n):
        return jnp.broadcast_to(v[e:e + 1, :], (n, q)).astype(BF16)

    hpg = SSD_HEADS // SSD_GROUPS
    gw = hpg * SSD_HEAD_DIM
    for g in range(SSD_GROUPS):
        grows = slice(g * gw, (g + 1) * gw)
        bg = b_ref[g].astype(BF16)
        cg = c_ref[g]
        cb_t = lax.dot_general(bg, cg.astype(BF16), NT, preferred_element_type=F32)
        cg_t = cg.astype(F32).T.astype(BF16)
        stg = st_scr[d, grows, :]
        stg_b = stg.astype(BF16)
        decs = []
        for eh in range(hpg):
            e = g * hpg + eh
            rows = slice(e * SSD_HEAD_DIM, (e + 1) * SSD_HEAD_DIM)
            row = jnp.broadcast_to(cum_t[e:e + 1, :], (q, q))
            seg = row - row.T
            m_t = (cb_t * jnp.exp(jnp.where(keep_t, seg, neg_inf))).astype(BF16)
            xe = xtb_scr[d, rows, :]
            lhs = jnp.concatenate([xe * row_bf16(dt_t, e, SSD_HEAD_DIM),
                                   stg_b[eh * SSD_HEAD_DIM:(eh + 1) * SSD_HEAD_DIM, :]], axis=1)
            rhs = jnp.concatenate([m_t, cg_t * row_bf16(ecum_t, e, SSD_STATE)], axis=0)
            ybuf_scr[slot, rows, :] = _dot(lhs, rhs)
            xw_scr[d, rows, :] = xe * row_bf16(wgt_t, e, SSD_HEAD_DIM)
            decs.append(jnp.broadcast_to(cdec[e:e + 1, :], (SSD_HEAD_DIM, SSD_STATE)))
        new = _dot(xw_scr[d, grows, :], bg)
        st_scr[d, grows, :] = stg * jnp.concatenate(decs, axis=0) + new


def _ssd_finish(c, first_slot, last_slot, xt_ref, z_ref, x_ref, dskip_ref, gnorm_ref, wout_ref, gate, gfin_ref,
                ybuf_scr, y_ref):
    q = SSD_CHUNK
    acc = None
    ss = jnp.zeros((1, q), F32)
    slabs_per = FIN_CHUNK // LANES
    for k in range(SSD_INNER // FIN_CHUNK):
        rows = slice(k * FIN_CHUNK, (k + 1) * FIN_CHUNK)
        ytot = ybuf_scr[first_slot, rows, :] + ybuf_scr[last_slot, rows, :] + dskip_ref[rows, :] * xt_ref[rows, :]
        zt = jnp.concatenate([z_ref[p].astype(F32).T for p in range(k * slabs_per, (k + 1) * slabs_per)], axis=0)
        gated = ytot * _silu(zt)
        ss = ss + jnp.sum(gated * gated, axis=0, keepdims=True)
        part = _dot((gated * gnorm_ref[rows, :]).T.astype(BF16), wout_ref[rows, :])
        acc = part if acc is None else acc + part
    inv = lax.rsqrt(ss * (1.0 / SSD_INNER) + EPS)
    inv_col = jnp.broadcast_to(inv, (q, q)).T
    scale = jnp.concatenate([inv_col] * (D_MODEL // LANES), axis=1)
    xn = x_ref[...] + gate * (acc * scale)
    y_ref[pl.ds(pl.multiple_of(c * q, q), q), :] = _rms(xn, gfin_ref[...])


def _ssd_kernel(*refs, n_chunks, has_init, emit_state):
    refs = list(refs)
    xs_refs, b_refs, c_refs, dt_refs, z_refs, x_refs = (refs[0:2], refs[2:4], refs[4:6], refs[6:8], refs[8:10],
                                                        refs[10:12])
    refs = refs[12:]
    st0_ref = refs.pop(0) if has_init else None
    aneg_ref, dskip_ref, gnorm_ref, wout_ref, mod_ref, gfin_ref, y_ref = refs[:7]
    refs = refs[7:]
    stout_ref = refs.pop(0) if emit_state else None
    st_scr, ybuf_scr, xw_scr, xt_scr, xtb_scr = refs

    t = pl.program_id(1)
    second = t >= n_chunks // 2
    chunks = (t, n_chunks - 1 - t)

    @pl.when(t == 0)
    def _():
        if has_init:
            st_scr[...] = st0_ref[0]
        else:
            st_scr[...] = jnp.zeros(st_scr.shape, F32)

    for d in range(2):
        slot = jnp.where(second, n_chunks + d, chunks[d])
        _ssd_direction(d, xs_refs[d], b_refs[d], c_refs[d], dt_refs[d], aneg_ref, st_scr, xw_scr, xt_scr, xtb_scr,
                       ybuf_scr, slot)

    if emit_state:
        @pl.when(t == n_chunks - 1)
        def _():
            stout_ref[0] = st_scr[...]

    @pl.when(second)
    def _():
        for d in range(2):
            _ssd_finish(chunks[d], chunks[d], n_chunks + d, xt_scr.at[d], z_refs[d], x_refs[d], dskip_ref,
                        gnorm_ref, wout_ref, mod_ref[0, 2:3, :], gfin_ref, ybuf_scr, y_ref)


def _ssd_call(u3, dt2, x2, seq_len, st0, a_neg, d_skip, g_norm, w_out, mods, mod_row, g_final, emit_state):
    rows = x2.shape[0]
    q = SSD_CHUNK
    nc = seq_len // q
    half = nc // 2
    nb = rows // seq_len
    has_init = st0 is not None
    n_slab = SSD_INNER // LANES
    bslab = 2 * n_slab // SSD_GROUPS

    fwd = lambda b, t: b * nc + t
    bwd = lambda b, t: b * nc + nc - 1 - t
    fwd_fin = lambda b, t: b * nc + jnp.maximum(t, half)
    bwd_fin = lambda b, t: b * nc + jnp.minimum(nc - 1 - t, half - 1)
    mod_map = (lambda b, t: (b, 0, 0)) if mod_row is None else (lambda b, t: (mod_row, 0, 0))

    def pair(block, make_map):
        return [pl.BlockSpec(block, make_map(fwd)), pl.BlockSpec(block, make_map(bwd))]

    in_specs = (pair((n_slab, q, LANES), lambda ch: (lambda b, t: (1, ch(b, t), 0)))
                + pair((SSD_GROUPS, q, LANES), lambda ch: (lambda b, t: (bslab, ch(b, t), 0)))
                + pair((SSD_GROUPS, q, LANES), lambda ch: (lambda b, t: (bslab + 1, ch(b, t), 0)))
                + [pl.BlockSpec((q, HEAD_SLAB), lambda b, t: (fwd(b, t), 0)),
                   pl.BlockSpec((q, HEAD_SLAB), lambda b, t: (bwd(b, t), 1)),
                   pl.BlockSpec((n_slab, q, LANES), lambda b, t: (0, fwd_fin(b, t), 0)),
                   pl.BlockSpec((n_slab, q, LANES), lambda b, t: (0, bwd_fin(b, t), 0)),
                   pl.BlockSpec((q, D_MODEL), lambda b, t: (fwd_fin(b, t), 0)),
                   pl.BlockSpec((q, D_MODEL), lambda b, t: (bwd_fin(b, t), 0))])
    args = [u3] * 6 + [dt2, dt2, u3, u3, x2, x2]
    if has_init:
        in_specs.append(pl.BlockSpec((1, 2, SSD_INNER, SSD_STATE), lambda b, t: (b, 0, 0, 0)))
        args.append(st0)
    in_specs += [_full(a_neg.shape), _full(d_skip.shape), _full(g_norm.shape), _full(w_out.shape),
                 pl.BlockSpec((1, 3, D_MODEL), mod_map), _full(g_final.shape)]
    args += [a_neg, d_skip, g_norm, w_out, mods, g_final]
    out_specs = [pl.BlockSpec((seq_len, D_MODEL), lambda b, t: (b, 0))]
    out_shape = [jax.ShapeDtypeStruct((rows, D_MODEL), F32)]
    if emit_state:
        out_specs.append(pl.BlockSpec((1, 2, SSD_INNER, SSD_STATE), lambda b, t: (b, 0, 0, 0)))
        out_shape.append(jax.ShapeDtypeStruct((nb, 2, SSD_INNER, SSD_STATE), F32))
    return pl.pallas_call(
        functools.partial(_ssd_kernel, n_chunks=nc, has_init=has_init, emit_state=emit_state),
        grid=(nb, nc),
        in_specs=in_specs, out_specs=out_specs, out_shape=out_shape,
        scratch_shapes=[pltpu.VMEM((2, SSD_INNER, SSD_STATE), F32),
                        pltpu.VMEM((nc + 2, SSD_INNER, q), F32),
                        pltpu.VMEM((2, SSD_INNER, q), BF16),
                        pltpu.VMEM((2, SSD_INNER, q), F32),
                        pltpu.VMEM((2, SSD_INNER, q), BF16)],
        compiler_params=_params(2),
        name="ssd_init" if has_init else "ssd",
    )(*args)


def _rope_tables(length):
    rows = length // GRID_W
    row = jnp.repeat(jnp.arange(rows, dtype=F32), GRID_W)
    col = jnp.tile(jnp.arange(GRID_W, dtype=F32), rows)
    n_freq = QK_ROPE // 4
    inv = jnp.power(ROPE_THETA, -jnp.arange(n_freq, dtype=F32) / n_freq)
    ang = jnp.concatenate([row[:, None] * inv, col[:, None] * inv], axis=-1)
    cos, sin = jnp.cos(ang), jnp.sin(ang)
    cos2 = jnp.repeat(cos, 2, axis=-1)
    sin2 = jnp.stack([-sin, sin], axis=-1).reshape(length, QK_ROPE)
    pad = HEAD_SLAB - QK_ROPE
    return (jnp.concatenate([cos2, jnp.ones((length, pad), F32)], axis=-1),
            jnp.concatenate([sin2, jnp.zeros((length, pad), F32)], axis=-1))


def _prep_layer_a(w_in, w_uq, w_uk, w_uv):
    d = w_in.shape[0]
    swap = jnp.arange(QK_ROPE) ^ 1
    w_in = w_in.astype(BF16)
    kr = w_in[:, 384:416]
    z96 = jnp.zeros((d, HEAD_SLAB - QK_ROPE), BF16)
    w_r = jnp.concatenate([w_in[:, 0:384], kr, z96, kr[:, swap], z96, w_in[:, 416:]], axis=1)
    uq = w_uq.reshape(Q_RANK, MLA_HEADS, QK_NOPE + QK_ROPE)
    nope, rp = uq[..., :QK_NOPE], uq[..., QK_NOPE:]
    z32 = jnp.zeros((Q_RANK, MLA_HEADS, 32), F32)
    w1 = jnp.concatenate([rp, z32, nope], axis=-1).reshape(Q_RANK, MLA_HEADS * HEAD_SLAB).astype(BF16)
    w2 = jnp.concatenate([rp[..., swap], z32, jnp.zeros_like(nope)], axis=-1)
    w2 = w2.reshape(Q_RANK, MLA_HEADS * HEAD_SLAB).astype(BF16)
    uk = w_uk.reshape(KV_RANK, MLA_HEADS, QK_NOPE)
    top = jnp.concatenate([jnp.zeros((KV_RANK, MLA_HEADS, 64), F32), uk], axis=-1)
    eye = jnp.broadcast_to(jnp.eye(QK_ROPE, dtype=F32)[:, None, :], (QK_ROPE, MLA_HEADS, QK_ROPE))
    mid = jnp.concatenate([eye, jnp.zeros((QK_ROPE, MLA_HEADS, HEAD_SLAB - QK_ROPE), F32)], axis=-1)
    bot = jnp.zeros((KV_RANK - QK_ROPE, MLA_HEADS, HEAD_SLAB), F32)
    w_k = jnp.concatenate([top, mid, bot], axis=0).reshape(2 * KV_RANK, MLA_HEADS * HEAD_SLAB).astype(BF16)
    return w_r, w1, w2, w_k, w_uv.T.astype(BF16)


def kernel(x_prompt, x_sample, cache_ckv, cache_krope, state_ssd, c, c_ctx, w_mod, b_mod, g_pre, g_final,
           a_w_in, a_g_q, a_g_kv, a_w_uq, a_w_uk, a_w_uv, a_conv_w, a_conv_b, a_ln_g, a_ln_b, a_w_out,
           c_w_in, c_conv_w, c_conv_b, c_dt_bias, c_a_log, c_d, c_g_norm, c_w_out):
    nbp, lp, d = x_prompt.shape
    nbs, ls, _ = x_sample.shape
    ctx_row = nbs
    cond = jnp.concatenate([c, c_ctx[None, :], jnp.zeros((16 - nbs - 1, d), F32)], axis=0)
    mods_all = _mod_call(cond, w_mod, b_mod).reshape(DEPTH, 16, 3, d)
    row2 = lambda v: v.reshape(1, -1)

    mods = mods_all[0]
    w_r, w1, w2, w_k, w_vt = _prep_layer_a(a_w_in[0], a_w_uq[0], a_w_uk[0], a_w_uv[0])
    cos_t, sin_t = _rope_tables(ls)
    gpre = row2(g_pre[0])
    gq, gkv = row2(a_g_q[0]), row2(a_g_kv[0])
    a_wout = a_w_out[0].astype(BF16)
    conv_args = (a_conv_w[0], row2(a_conv_b[0]), row2(a_ln_g[0]), row2(a_ln_b[0]), a_wout)

    qp, kxp, cvp, sgp, ckv_p, kr_p = _a_in_call(x_prompt.reshape(nbp * lp, d), mods, ctx_row, lp, gpre,
                                                w_r, gq, gkv, w1, None, None, None)
    attn_p = _attn_call(qp.reshape(nbp, lp, -1), kxp.reshape(nbp, lp, -1), None, w_k, w_vt)
    xp1 = _a_out_call(attn_p, cvp.reshape(nbp, lp, -1), sgp.reshape(nbp, lp, -1), x_prompt, mods, ctx_row,
                      *conv_args)

    qs, kxs, cvs, sgs = _a_in_call(x_sample.reshape(nbs * ls, d), mods, None, ls, gpre,
                                   w_r, gq, gkv, w1, w2, cos_t, sin_t)
    n_ctx = cache_ckv.shape[2]
    kx_ctx = jnp.concatenate([cache_ckv[:, 0], cache_krope[:, 0],
                              jnp.zeros((nbs, n_ctx, KV_RANK - QK_ROPE), F32)], axis=-1).astype(BF16)
    attn_s = _attn_call(qs.reshape(nbs, ls, -1), kxs.reshape(nbs, ls, -1), kx_ctx, w_k, w_vt)
    xs1 = _a_out_call(attn_s, cvs.reshape(nbs, ls, -1), sgs.reshape(nbs, ls, -1), x_sample, mods, None,
                      *conv_args)

    mods = mods_all[1]
    gpre = row2(g_pre[1])
    w_in = c_w_in[0]
    w_main = w_in.astype(BF16)
    dtw = w_in[:, C_MAIN:]
    zpad = jnp.zeros((d, HEAD_SLAB - SSD_HEADS), F32)
    w_dt = jnp.concatenate([dtw[:, 0:SSD_HEADS], zpad, dtw[:, SSD_HEADS:], zpad], axis=1).astype(BF16)
    pad32 = jnp.zeros((2, HEAD_SLAB - SSD_HEADS), F32)
    dt_b = jnp.concatenate([c_dt_bias[0], pad32], axis=1).reshape(1, 2 * HEAD_SLAB)
    a_neg = jnp.concatenate([-jnp.exp(c_a_log[0]), pad32], axis=1).reshape(2, 1, HEAD_SLAB)
    lane_rep = lambda v: jnp.broadcast_to(v.reshape(-1, 1), (v.size, LANES))
    d_skip = lane_rep(jnp.repeat(c_d[0], SSD_HEAD_DIM))
    gnorm = lane_rep(c_g_norm[0])
    c_wout = c_w_out[0].astype(BF16)
    gfin = row2(g_final)
    cw, cbias = c_conv_w[0], row2(c_conv_b[0])

    xp1f = xp1.reshape(nbp * lp, d)
    up, dtp = _c_in_call(xp1f, mods, ctx_row, lp, gpre, w_main, w_dt, dt_b, cw, cbias)
    y_prompt, st_p = _ssd_call(up, dtp, xp1f, lp, None, a_neg, d_skip, gnorm, c_wout, mods, ctx_row, gfin, True)
    xs1f = xs1.reshape(nbs * ls, d)
    us, dts = _c_in_call(xs1f, mods, None, ls, gpre, w_main, w_dt, dt_b, cw, cbias)
    st0 = state_ssd[:, 0].reshape(nbs, 2, SSD_INNER, SSD_STATE)
    (y_sample,) = _ssd_call(us, dts, xs1f, ls, st0, a_neg, d_skip, gnorm, c_wout, mods, None, gfin, False)

    new_cache_ckv = ckv_p.reshape(nbp, 1, lp, KV_RANK)
    new_cache_krope = kr_p.reshape(nbp, 1, lp, QK_ROPE)
    new_state_ssd = st_p.reshape(nbp, 1, 2, SSD_HEADS, SSD_HEAD_DIM, SSD_STATE)
    return (y_prompt.reshape(nbp, lp, d), y_sample.reshape(nbs, ls, d), new_cache_ckv, new_cache_krope,
            new_state_ssd)
```

```python
import functools
import math

import jax
import jax.numpy as jnp
from jax import lax
from jax.experimental import pallas as pl
from jax.experimental.pallas import tpu as pltpu

F32 = jnp.float32
BF16 = jnp.bfloat16
ACT = BF16

LANES = 128
SUBLANES = 8

D_MODEL = 1024
DEPTH = 2
GRID_W = 64
EPS = 1e-6

MLA_HEADS = 8
Q_RANK = 256
KV_RANK = 128
QK_NOPE = 64
QK_ROPE = 32
V_HEAD = 64
MLA_WIDTH = MLA_HEADS * V_HEAD
ATTN_SCALE = (QK_NOPE + QK_ROPE) ** -0.5
ROPE_THETA = 10000.0
HEAD_SLAB = LANES

CONV_WIDTH = 512
CONV_K = 31
A_GLU0 = 640
A_GATE0 = A_GLU0 + 2 * CONV_WIDTH
A_COLS = A_GATE0 + MLA_WIDTH + CONV_WIDTH

SSD_INNER = 2048
SSD_HEAD_DIM = 64
SSD_HEADS = 32
SSD_GROUPS = 4
SSD_STATE = 128
SSD_CONV_K = 5
SSD_CHUNK = 128
FIN_CHUNK = 512
SSD_CONV_CH = SSD_INNER + 2 * SSD_GROUPS * SSD_STATE
C_MAIN = SSD_INNER + SSD_CONV_CH
C_TILE = 1024
ROW_TILE = 1024
MM_SLABS = 2
A_ROW_TILE = 1024

CONV_STRIDE = 4
CONV_BLOCK = SUBLANES * CONV_STRIDE

VMEM_LIMIT = 56 * 1024 * 1024
HI = lax.Precision.HIGHEST
NT = (((1,), (1,)), ((), ()))
LOG2E = math.log2(math.e)


def _dot(a, b):
    return jnp.dot(a, b, preferred_element_type=F32)


def _sigmoid(x):
    return 1.0 / (1.0 + jnp.exp(-x))


def _silu(x):
    return x * _sigmoid(x)


def _rms(x, g):
    return x * lax.rsqrt(jnp.mean(x * x, axis=-1, keepdims=True) + EPS) * g


def _params(n_axes):
    return pltpu.CompilerParams(dimension_semantics=("arbitrary",) * n_axes,
                                vmem_limit_bytes=VMEM_LIMIT)


def _full(shape):
    nd = len(shape)
    return pl.BlockSpec(shape, lambda *_: (0,) * nd)


def _mod_kernel(cond_ref, w_ref, b_ref, o_ref):
    s = _silu(cond_ref[...])
    o_ref[0] = jnp.dot(s, w_ref[0], precision=HI, preferred_element_type=F32) + b_ref[0]


def _mod_call(cond, w_mod, b_mod):
    nrow = cond.shape[0]
    tn = 1024
    return pl.pallas_call(
        _mod_kernel,
        grid=(DEPTH, 3 * D_MODEL // tn),
        in_specs=[pl.BlockSpec((nrow, D_MODEL), lambda l, j: (0, 0)),
                  pl.BlockSpec((1, D_MODEL, tn), lambda l, j: (l, 0, j)),
                  pl.BlockSpec((1, 1, tn), lambda l, j: (l, 0, j))],
        out_specs=pl.BlockSpec((1, nrow, tn), lambda l, j: (l, 0, j)),
        out_shape=jax.ShapeDtypeStruct((DEPTH, nrow, 3 * D_MODEL), F32),
        compiler_params=_params(2),
        name="modulation",
    )(cond, w_mod, b_mod.reshape(DEPTH, 1, 3 * D_MODEL))


def _a_in_kernel(*refs, rope):
    if rope:
        (x_ref, mod_ref, gpre_ref, w_ref, gq_ref, gkv_ref, w1_ref, w2_ref, cos_ref, sin_ref,
         q_ref, kx_ref, cv_ref, sg_ref) = refs
    else:
        (x_ref, mod_ref, gpre_ref, w_ref, gq_ref, gkv_ref, w1_ref,
         q_ref, kx_ref, cv_ref, sg_ref, ckv_ref, kr_ref) = refs
    x = x_ref[...]
    h = _rms(x, gpre_ref[...]) * (1.0 + mod_ref[0, 1:2, :]) + mod_ref[0, 0:1, :]
    hb = h.astype(BF16)

    u0 = _dot(hb, w_ref[:, 0:A_GLU0])
    qn = _rms(u0[:, 0:Q_RANK], gq_ref[...]).astype(BF16)
    ckv = _rms(u0[:, Q_RANK:Q_RANK + KV_RANK], gkv_ref[...])
    kr = u0[:, 384:512]
    qf = _dot(qn, w1_ref[...])
    qscale = ATTN_SCALE * LOG2E
    if rope:
        cos = cos_ref[...]
        sin = sin_ref[...]
        qs = _dot(qn, w2_ref[...])
        for hd in range(MLA_HEADS):
            sl = slice(hd * HEAD_SLAB, (hd + 1) * HEAD_SLAB)
            q_ref[:, sl] = ((qf[:, sl] * cos + qs[:, sl] * sin) * qscale).astype(BF16)
        kr = kr * cos + u0[:, 512:640] * sin
    else:
        q_ref[...] = (qf * qscale).astype(BF16)
        ckv_ref[...] = ckv
        kr_ref[...] = kr[:, 0:QK_ROPE]
    kx_ref[:, 0:KV_RANK] = ckv.astype(BF16)
    kx_ref[:, KV_RANK:2 * KV_RANK] = kr.astype(BF16)

    glu = _dot(hb, w_ref[:, A_GLU0:A_GATE0])
    cv_ref[...] = (glu[:, 0:CONV_WIDTH] * _sigmoid(glu[:, CONV_WIDTH:])).astype(ACT)
    sg_ref[...] = _silu(_dot(hb, w_ref[:, A_GATE0:A_COLS])).astype(ACT)


def _a_in_call(x2, mods, mod_row, seq_len, g_pre, w_in, g_q, g_kv, w1, w2, cos_t, sin_t):
    rows = x2.shape[0]
    tr = A_ROW_TILE
    rope = w2 is not None
    per_seq = max(seq_len // tr, 1)
    if mod_row is None:
        mod_map = lambda i: (i // per_seq, 0, 0)
    else:
        mod_map = lambda i: (mod_row, 0, 0)
    row = lambda i: (i, 0)
    in_specs = [pl.BlockSpec((tr, D_MODEL), row),
                pl.BlockSpec((1, 3, D_MODEL), mod_map),
                _full(g_pre.shape), _full(w_in.shape), _full(g_q.shape), _full(g_kv.shape),
                _full(w1.shape)]
    args = [x2, mods, g_pre, w_in, g_q, g_kv, w1]
    out_specs = [pl.BlockSpec((tr, MLA_HEADS * HEAD_SLAB), row),
                 pl.BlockSpec((tr, 2 * KV_RANK), row),
                 pl.BlockSpec((tr, CONV_WIDTH), row),
                 pl.BlockSpec((tr, MLA_WIDTH + CONV_WIDTH), row)]
    out_shape = [jax.ShapeDtypeStruct((rows, MLA_HEADS * HEAD_SLAB), BF16),
                 jax.ShapeDtypeStruct((rows, 2 * KV_RANK), BF16),
                 jax.ShapeDtypeStruct((rows, CONV_WIDTH), ACT),
                 jax.ShapeDtypeStruct((rows, MLA_WIDTH + CONV_WIDTH), ACT)]
    if rope:
        in_specs += [_full(w2.shape),
                     pl.BlockSpec((tr, HEAD_SLAB), lambda i: (i % per_seq, 0)),
                     pl.BlockSpec((tr, HEAD_SLAB), lambda i: (i % per_seq, 0))]
        args += [w2, cos_t, sin_t]
    else:
        out_specs += [pl.BlockSpec((tr, KV_RANK), row), pl.BlockSpec((tr, QK_ROPE), row)]
        out_shape += [jax.ShapeDtypeStruct((rows, KV_RANK), F32),
                      jax.ShapeDtypeStruct((rows, QK_ROPE), F32)]
    return pl.pallas_call(
        functools.partial(_a_in_kernel, rope=rope),
        grid=(rows // tr,),
        in_specs=in_specs, out_specs=out_specs, out_shape=out_shape,
        compiler_params=_params(1),
        name="a_in_rope" if rope else "a_in",
    )(*args)


def _attn_kernel(*refs, n_ctx, seq_len):
    if n_ctx:
        q_ref, kxn_ref, kxc_ref, wk_ref, wvt_ref, o_ref, k_scr, vt_scr, ot_scr, s_scr = refs
    else:
        q_ref, kxn_ref, wk_ref, wvt_ref, o_ref, k_scr, vt_scr, ot_scr, s_scr = refs

    @pl.when(pl.program_id(1) == 0)
    def _():
        def fill(kx, lo, n):
            k_scr[lo:lo + n, :] = _dot(kx, wk_ref[...]).astype(BF16)
            vt_scr[:, lo:lo + n] = lax.dot_general(
                wvt_ref[...], kx[:, 0:KV_RANK], NT, preferred_element_type=F32).astype(BF16)
        if n_ctx:
            fill(kxc_ref[0], 0, n_ctx)
        fill(kxn_ref[0], n_ctx, seq_len)

    def scores(hd):
        sl = slice(hd * HEAD_SLAB, (hd + 1) * HEAD_SLAB)
        s_scr[hd % 2] = lax.dot_general(k_scr[:, sl], q_ref[0, :, sl], NT, preferred_element_type=F32)

    scores(0)
    for hd in range(MLA_HEADS):
        vrows = slice(hd * V_HEAD, (hd + 1) * V_HEAD)
        if hd + 1 < MLA_HEADS:
            scores(hd + 1)
        st = s_scr[hd % 2]
        p = jnp.exp2(st - jnp.max(st, axis=0, keepdims=True))
        den = jnp.sum(p, axis=0, keepdims=True)
        ot_scr[vrows, :] = _dot(vt_scr[vrows, :], p.astype(BF16)) / den
    o_ref[0] = ot_scr[...].T.astype(ACT)


def _attn_call(q3, kx_new, kx_ctx, w_k, w_vt):
    nb, seq_len, _ = q3.shape
    n_ctx = 0 if kx_ctx is None else kx_ctx.shape[1]
    tq = min(seq_len, 1024)
    lk = n_ctx + seq_len
    in_specs = [pl.BlockSpec((1, tq, MLA_HEADS * HEAD_SLAB), lambda b, i: (b, i, 0)),
                pl.BlockSpec((1, seq_len, 2 * KV_RANK), lambda b, i: (b, 0, 0))]
    args = [q3, kx_new]
    if n_ctx:
        in_specs.append(pl.BlockSpec((1, n_ctx, 2 * KV_RANK), lambda b, i: (b, 0, 0)))
        args.append(kx_ctx)
    in_specs += [_full(w_k.shape), _full(w_vt.shape)]
    args += [w_k, w_vt]
    return pl.pallas_call(
        functools.partial(_attn_kernel, n_ctx=n_ctx, seq_len=seq_len),
        grid=(nb, seq_len // tq),
        in_specs=in_specs,
        out_specs=pl.BlockSpec((1, tq, MLA_WIDTH), lambda b, i: (b, i, 0)),
        out_shape=jax.ShapeDtypeStruct((nb, seq_len, MLA_WIDTH), ACT),
        scratch_shapes=[pltpu.VMEM((lk, MLA_HEADS * HEAD_SLAB), BF16),
                        pltpu.VMEM((MLA_WIDTH, lk), BF16),
                        pltpu.VMEM((MLA_WIDTH, tq), F32),
                        pltpu.VMEM((2, lk, tq), F32)],
        compiler_params=_params(2),
        name="attn_ctx" if n_ctx else "attn",
    )(*args)


def _a_out_kernel(attn_ref, cv_ref, sg_ref, x_ref, mod_ref, cw_ref, cb_ref, lng_ref, lnb_ref, wout_ref,
                  o_ref, pad_scr, cvn_scr, *, seq_len):
    halo = 16
    n_slab = CONV_WIDTH // LANES
    zeros = jnp.zeros((halo, LANES), F32)
    for sl in range(n_slab):
        pad_scr[sl, 0:halo, :] = zeros
        pad_scr[sl, halo + seq_len:2 * halo + seq_len, :] = zeros
        pad_scr[sl, halo:halo + seq_len, :] = cv_ref[0, :, sl * LANES:(sl + 1) * LANES].astype(F32)

    def block(r, carry):
        base = r * CONV_BLOCK
        for sl in range(n_slab):
            lanes = slice(sl * LANES, (sl + 1) * LANES)
            accs = [None] * CONV_STRIDE
            for k in range(CONV_K):
                w = jnp.broadcast_to(cw_ref[k:k + 1, lanes], (SUBLANES, LANES))
                for ph in range(CONV_STRIDE):
                    start = base + (halo - CONV_K // 2 + ph + k)
                    v = pad_scr[sl, pl.ds(start, SUBLANES, stride=CONV_STRIDE), :] * w
                    accs[ph] = v if accs[ph] is None else accs[ph] + v
            for ph in range(CONV_STRIDE):
                cvn_scr[sl, pl.ds(base + ph, SUBLANES, stride=CONV_STRIDE), :] = accs[ph]
        return carry

    lax.fori_loop(0, seq_len // CONV_BLOCK, block, 0)
    conv = jnp.concatenate([cvn_scr[sl] for sl in range(n_slab)], axis=1) + cb_ref[...]
    mu = jnp.mean(conv, axis=-1, keepdims=True)
    cen = conv - mu
    ln = cen * lax.rsqrt(jnp.mean(cen * cen, axis=-1, keepdims=True) + EPS) * lng_ref[...] + lnb_ref[...]
    sg = sg_ref[0].astype(F32)
    mix = jnp.concatenate([attn_ref[0].astype(F32) * sg[:, 0:MLA_WIDTH], _silu(ln) * sg[:, MLA_WIDTH:]], axis=1)
    out = _dot(mix.astype(BF16), wout_ref[...])
    o_ref[0] = x_ref[0] + mod_ref[0, 2:3, :] * out


def _a_out_call(attn3, cv3, sg3, x3, mods, mod_row, conv_w, conv_b, ln_g, ln_b, w_out):
    nb, seq_len, _ = x3.shape
    blk = lambda c: pl.BlockSpec((1, seq_len, c), lambda b: (b, 0, 0))
    mod_map = (lambda b: (b, 0, 0)) if mod_row is None else (lambda b: (mod_row, 0, 0))
    n_slab = CONV_WIDTH // LANES
    return pl.pallas_call(
        functools.partial(_a_out_kernel, seq_len=seq_len),
        grid=(nb,),
        in_specs=[blk(MLA_WIDTH), blk(CONV_WIDTH), blk(MLA_WIDTH + CONV_WIDTH), blk(D_MODEL),
                  pl.BlockSpec((1, 3, D_MODEL), mod_map),
                  _full(conv_w.shape), _full(conv_b.shape), _full(ln_g.shape), _full(ln_b.shape),
                  _full(w_out.shape)],
        out_specs=blk(D_MODEL),
        out_shape=jax.ShapeDtypeStruct((nb, seq_len, D_MODEL), F32),
        scratch_shapes=[pltpu.VMEM((n_slab, seq_len + 32, LANES), F32),
                        pltpu.VMEM((n_slab, seq_len, LANES), F32)],
        compiler_params=_params(1),
        name="a_out",
    )(attn3, cv3, sg3, x3, mods, conv_w, conv_b, ln_g, ln_b, w_out)


def _softplus(x):
    return jnp.maximum(x, 0.0) + jnp.log1p(jnp.exp(-jnp.abs(x)))


def _c_in_kernel(x_ref, mod_ref, gpre_ref, w_ref, wdt_ref, dtb_ref, cw_ref, cb_ref,
                 u_ref, dt_ref, h_scr, pad0_scr, pad1_scr, res_scr, *, seq_len, n_seq):
    j = pl.program_id(1)
    gap = SUBLANES
    pitch = seq_len + gap
    n_slab = C_TILE // LANES
    n_z = SSD_INNER // C_TILE
    n_tiles = C_MAIN // C_TILE
    pads = (pad0_scr, pad1_scr)

    @pl.when(j == 0)
    def _():
        h = _rms(x_ref[...], gpre_ref[...]) * (1.0 + mod_ref[0, 1:2, :]) + mod_ref[0, 0:1, :]
        h_scr[...] = h.astype(BF16)
        dt_ref[...] = _softplus(_dot(h_scr[...], wdt_ref[...]) + dtb_ref[...])

    def project(tile, slabs=range(C_TILE // LANES)):
        lo = tile * C_TILE + slabs[0] * LANES
        return _dot(h_scr[...], w_ref[:, lo:lo + len(slabs) * LANES])

    def stage(u, buf, slabs=range(C_TILE // LANES)):
        zeros = jnp.zeros((gap, LANES), F32)
        pad = pads[buf]
        for i, sl in enumerate(slabs):
            pad[sl, 0:gap, :] = zeros
            for s in range(n_seq):
                lo = gap + s * pitch
                pad[sl, lo:lo + seq_len, :] = u[s * seq_len:(s + 1) * seq_len, i * LANES:(i + 1) * LANES]
                pad[sl, lo + seq_len:lo + pitch, :] = zeros

    def conv(buf, slabs=range(C_TILE // LANES)):
        pad = pads[buf]
        for sl in slabs:
            lanes = slice(sl * LANES, (sl + 1) * LANES)
            ws = [jnp.broadcast_to(cw_ref[k:k + 1, lanes], (SUBLANES, LANES)) for k in range(SSD_CONV_K)]
            bias = jnp.broadcast_to(cb_ref[:, lanes], (SUBLANES, LANES))
            for s in range(n_seq):
                for r in range(seq_len // CONV_BLOCK):
                    for ph in range(CONV_STRIDE):
                        acc = bias
                        for k in range(SSD_CONV_K):
                            start = r * CONV_BLOCK + gap + s * pitch - SSD_CONV_K // 2 + ph + k
                            acc = acc + pad[sl, pl.ds(start, SUBLANES, stride=CONV_STRIDE), :] * ws[k]
                        row0 = r * CONV_BLOCK + s * seq_len + ph
                        res_scr[sl, pl.ds(row0, SUBLANES, stride=CONV_STRIDE), :] = _silu(acc)
            u_ref[sl] = res_scr[sl].astype(ACT)

    for tile in range(n_z):
        @pl.when(j == tile)
        def _(tile=tile):
            u = project(tile)
            for sl in range(n_slab):
                u_ref[sl] = u[:, sl * LANES:(sl + 1) * LANES].astype(ACT)

    @pl.when(j == n_z)
    def _():
        stage(project(n_z), 0)

    for tile in range(n_z + 1, n_tiles):
        @pl.when(j == tile)
        def _(tile=tile):
            buf = (tile - n_z) % 2
            for part in range(n_slab // MM_SLABS):
                slabs = range(part * MM_SLABS, (part + 1) * MM_SLABS)
                stage(project(tile, slabs), buf, slabs)
                conv(1 - buf, slabs)

    @pl.when(j == n_tiles)
    def _():
        conv((n_tiles - 1 - n_z) % 2)


def _c_in_call(x2, mods, mod_row, seq_len, g_pre, w_main, w_dt, dt_b, conv_w, conv_b):
    rows = x2.shape[0]
    tr = ROW_TILE
    n_seq = tr // seq_len
    n_z = SSD_INNER // C_TILE
    n_slab = C_TILE // LANES
    n_tiles = C_MAIN // C_TILE
    mod_map = (lambda i, j: (i, 0, 0)) if mod_row is None else (lambda i, j: (mod_row, 0, 0))
    conv_map = lambda i, j: (0, jnp.clip(j - 1 - n_z, 0, n_tiles - 1 - n_z))
    out_map = lambda i, j: (j - (j >= n_z).astype(jnp.int32), i, 0)
    return pl.pallas_call(
        functools.partial(_c_in_kernel, seq_len=seq_len, n_seq=n_seq),
        grid=(rows // tr, n_tiles + 1),
        in_specs=[pl.BlockSpec((tr, D_MODEL), lambda i, j: (i, 0)),
                  pl.BlockSpec((1, 3, D_MODEL), mod_map),
                  _full(g_pre.shape),
                  pl.BlockSpec((D_MODEL, C_MAIN), lambda i, j: (0, 0), pipeline_mode=pl.Buffered(1)),
                  _full(w_dt.shape), _full(dt_b.shape),
                  pl.BlockSpec((SSD_CONV_K, C_TILE), conv_map),
                  pl.BlockSpec((1, C_TILE), conv_map)],
        out_specs=[pl.BlockSpec((n_slab, tr, LANES), out_map),
                   pl.BlockSpec((tr, 2 * HEAD_SLAB), lambda i, j: (i, 0))],
        out_shape=[jax.ShapeDtypeStruct((C_MAIN // LANES, rows, LANES), ACT),
                   jax.ShapeDtypeStruct((rows, 2 * HEAD_SLAB), F32)],
        scratch_shapes=[pltpu.VMEM((tr, D_MODEL), BF16),
                        pltpu.VMEM((n_slab, SUBLANES + n_seq * (seq_len + SUBLANES), LANES), F32),
                        pltpu.VMEM((n_slab, SUBLANES + n_seq * (seq_len + SUBLANES), LANES), F32),
                        pltpu.VMEM((n_slab, tr, LANES), F32)],
        compiler_params=_params(2),
        name="c_in",
    )(x2, mods, g_pre, w_main, w_dt, dt_b, conv_w, conv_b)


def _split3(a):
    hi = a.astype(BF16)
    r1 = a - hi.astype(F32)
    mid = r1.astype(BF16)
    lo = (r1 - mid.astype(F32)).astype(BF16)
    return hi, mid, lo


def _ssd_direction(d, xs_ref, b_ref, c_ref, dt_ref, aneg_ref, st_scr, xw_scr, xt_scr, xtb_scr, ybuf_scr, slot):
    q = SSD_CHUNK
    for p in range(SSD_INNER // LANES):
        xp_t = xs_ref[p].astype(F32).T
        xt_scr[d, p * LANES:(p + 1) * LANES, :] = xp_t
        xtb_scr[d, p * LANES:(p + 1) * LANES, :] = xp_t.astype(BF16)

    dt = dt_ref[...]
    a = dt * aneg_ref[d]
    ri = lax.broadcasted_iota(jnp.int32, (q, q), 0)
    ci = lax.broadcasted_iota(jnp.int32, (q, q), 1)
    keep_t = (ri <= ci) if d == 0 else (ri >= ci)
    keep_n = (ci <= ri) if d == 0 else (ci >= ri)
    pieces = jnp.concatenate(_split3(a), axis=1)
    c3 = _dot(jnp.where(keep_n, 1.0, 0.0).astype(BF16), pieces)
    cum = c3[:, 0:LANES] + c3[:, LANES:2 * LANES] + c3[:, 2 * LANES:3 * LANES]
    cum_t = cum.T
    dt_t = dt.T
    total_t = jnp.sum(a.T, axis=1, keepdims=True)
    ecum_t = jnp.exp(cum_t)
    wgt_t = jnp.exp(total_t - cum_t) * dt_t
    cdec = jnp.exp(total_t)
    neg_inf = jnp.float32(-jnp.inf)

    def row_bf16(v, e, n):
        return jnp.broadcast_to(v[e:e + 1, :], (n, q)).astype(BF16)

    hpg = SSD_HEADS // SSD_GROUPS
    gw = hpg * SSD_HEAD_DIM
    for g in range(SSD_GROUPS):
        grows = slice(g * gw, (g + 1) * gw)
        bg = b_ref[g].astype(BF16)
        cg = c_ref[g]
        cb_t = lax.dot_general(bg, cg.astype(BF16), NT, preferred_element_type=F32)
        cg_t = cg.astype(F32).T.astype(BF16)
        stg = st_scr[d, grows, :]
        stg_b = stg.astype(BF16)
        decs = []
        for eh in range(hpg):
            e = g * hpg + eh
            rows = slice(e * SSD_HEAD_DIM, (e + 1) * SSD_HEAD_DIM)
            row = jnp.broadcast_to(cum_t[e:e + 1, :], (q, q))
            seg = row - row.T
            m_t = (cb_t * jnp.exp(jnp.where(keep_t, seg, neg_inf))).astype(BF16)
            xe = xtb_scr[d, rows, :]
            lhs = jnp.concatenate([xe * row_bf16(dt_t, e, SSD_HEAD_DIM),
                                   stg_b[eh * SSD_HEAD_DIM:(eh + 1) * SSD_HEAD_DIM, :]], axis=1)
            rhs = jnp.concatenate([m_t, cg_t * row_bf16(ecum_t, e, SSD_STATE)], axis=0)
            ybuf_scr[slot, rows, :] = _dot(lhs, rhs)
            xw_scr[d, rows, :] = xe * row_bf16(wgt_t, e, SSD_HEAD_DIM)
            decs.append(jnp.broadcast_to(cdec[e:e + 1, :], (SSD_HEAD_DIM, SSD_STATE)))
        new = _dot(xw_scr[d, grows, :], bg)
        st_scr[d, grows, :] = stg * jnp.concatenate(decs, axis=0) + new


def _ssd_finish(c, first_slot, last_slot, xt_ref, z_ref, x_ref, dskip_ref, gnorm_ref, wout_ref, gate, gfin_ref,
                ybuf_scr, y_ref):
    q = SSD_CHUNK
    acc = None
    ss = jnp.zeros((1, q), F32)
    slabs_per = FIN_CHUNK // LANES
    for k in range(SSD_INNER // FIN_CHUNK):
        rows = slice(k * FIN_CHUNK, (k + 1) * FIN_CHUNK)
        ytot = ybuf_scr[first_slot, rows, :] + ybuf_scr[last_slot, rows, :] + dskip_ref[rows, :] * xt_ref[rows, :]
        zt = jnp.concatenate([z_ref[p].astype(F32).T for p in range(k * slabs_per, (k + 1) * slabs_per)], axis=0)
        gated = ytot * _silu(zt)
        ss = ss + jnp.sum(gated * gated, axis=0, keepdims=True)
        part = _dot((gated * gnorm_ref[rows, :]).T.astype(BF16), wout_ref[rows, :])
        acc = part if acc is None else acc + part
    inv = lax.rsqrt(ss * (1.0 / SSD_INNER) + EPS)
    inv_col = jnp.broadcast_to(inv, (q, q)).T
    scale = jnp.concatenate([inv_col] * (D_MODEL // LANES), axis=1)
    xn = x_ref[...] + gate * (acc * scale)
    y_ref[pl.ds(pl.multiple_of(c * q, q), q), :] = _rms(xn, gfin_ref[...])


def _ssd_kernel(*refs, n_chunks, has_init, emit_state):
    refs = list(refs)
    xs_refs, b_refs, c_refs, dt_refs, z_refs, x_refs = (refs[0:2], refs[2:4], refs[4:6], refs[6:8], refs[8:10],
                                                        refs[10:12])
    refs = refs[12:]
    st0_ref = refs.pop(0) if has_init else None
    aneg_ref, dskip_ref, gnorm_ref, wout_ref, mod_ref, gfin_ref, y_ref = refs[:7]
    refs = refs[7:]
    stout_ref = refs.pop(0) if emit_state else None
    st_scr, ybuf_scr, xw_scr, xt_scr, xtb_scr = refs

    t = pl.program_id(1)
    second = t >= n_chunks // 2
    chunks = (t, n_chunks - 1 - t)

    @pl.when(t == 0)
    def _():
        if has_init:
            st_scr[...] = st0_ref[0]
        else:
            st_scr[...] = jnp.zeros(st_scr.shape, F32)

    for d in range(2):
        slot = jnp.where(second, n_chunks + d, chunks[d])
        _ssd_direction(d, xs_refs[d], b_refs[d], c_refs[d], dt_refs[d], aneg_ref, st_scr, xw_scr, xt_scr, xtb_scr,
                       ybuf_scr, slot)

    if emit_state:
        @pl.when(t == n_chunks - 1)
        def _():
            stout_ref[0] = st_scr[...]

    @pl.when(second)
    def _():
        for d in range(2):
            _ssd_finish(chunks[d], chunks[d], n_chunks + d, xt_scr.at[d], z_refs[d], x_refs[d], dskip_ref,
                        gnorm_ref, wout_ref, mod_ref[0, 2:3, :], gfin_ref, ybuf_scr, y_ref)


def _ssd_call(u3, dt2, x2, seq_len, st0, a_neg, d_skip, g_norm, w_out, mods, mod_row, g_final, emit_state):
    rows = x2.shape[0]
    q = SSD_CHUNK
    nc = seq_len // q
    half = nc // 2
    nb = rows // seq_len
    has_init = st0 is not None
    n_slab = SSD_INNER // LANES
    bslab = 2 * n_slab // SSD_GROUPS

    fwd = lambda b, t: b * nc + t
    bwd = lambda b, t: b * nc + nc - 1 - t
    fwd_fin = lambda b, t: b * nc + jnp.maximum(t, half)
    bwd_fin = lambda b, t: b * nc + jnp.minimum(nc - 1 - t, half - 1)
    mod_map = (lambda b, t: (b, 0, 0)) if mod_row is None else (lambda b, t: (mod_row, 0, 0))

    def pair(block, make_map):
        return [pl.BlockSpec(block, make_map(fwd)), pl.BlockSpec(block, make_map(bwd))]

    in_specs = (pair((n_slab, q, LANES), lambda ch: (lambda b, t: (1, ch(b, t), 0)))
                + pair((SSD_GROUPS, q, LANES), lambda ch: (lambda b, t: (bslab, ch(b, t), 0)))
                + pair((SSD_GROUPS, q, LANES), lambda ch: (lambda b, t: (bslab + 1, ch(b, t), 0)))
                + [pl.BlockSpec((q, HEAD_SLAB), lambda b, t: (fwd(b, t), 0)),
                   pl.BlockSpec((q, HEAD_SLAB), lambda b, t: (bwd(b, t), 1)),
                   pl.BlockSpec((n_slab, q, LANES), lambda b, t: (0, fwd_fin(b, t), 0)),
                   pl.BlockSpec((n_slab, q, LANES), lambda b, t: (0, bwd_fin(b, t), 0)),
                   pl.BlockSpec((q, D_MODEL), lambda b, t: (fwd_fin(b, t), 0)),
                   pl.BlockSpec((q, D_MODEL), lambda b, t: (bwd_fin(b, t), 0))])
    args = [u3] * 6 + [dt2, dt2, u3, u3, x2, x2]
    if has_init:
        in_specs.append(pl.BlockSpec((1, 2, SSD_INNER, SSD_STATE), lambda b, t: (b, 0, 0, 0)))
        args.append(st0)
    in_specs += [_full(a_neg.shape), _full(d_skip.shape), _full(g_norm.shape), _full(w_out.shape),
                 pl.BlockSpec((1, 3, D_MODEL), mod_map), _full(g_final.shape)]
    args += [a_neg, d_skip, g_norm, w_out, mods, g_final]
    out_specs = [pl.BlockSpec((seq_len, D_MODEL), lambda b, t: (b, 0))]
    out_shape = [jax.ShapeDtypeStruct((rows, D_MODEL), F32)]
    if emit_state:
        out_specs.append(pl.BlockSpec((1, 2, SSD_INNER, SSD_STATE), lambda b, t: (b, 0, 0, 0)))
        out_shape.append(jax.ShapeDtypeStruct((nb, 2, SSD_INNER, SSD_STATE), F32))
    return pl.pallas_call(
        functools.partial(_ssd_kernel, n_chunks=nc, has_init=has_init, emit_state=emit_state),
        grid=(nb, nc),
        in_specs=in_specs, out_specs=out_specs, out_shape=out_shape,
        scratch_shapes=[pltpu.VMEM((2, SSD_INNER, SSD_STATE), F32),
                        pltpu.VMEM((nc + 2, SSD_INNER, q), F32),
                        pltpu.VMEM((2, SSD_INNER, q), BF16),
                        pltpu.VMEM((2, SSD_INNER, q), F32),
                        pltpu.VMEM((2, SSD_INNER, q), BF16)],
        compiler_params=_params(2),
        name="ssd_init" if has_init else "ssd",
    )(*args)


def _rope_tables(length):
    rows = length // GRID_W
    row = jnp.repeat(jnp.arange(rows, dtype=F32), GRID_W)
    col = jnp.tile(jnp.arange(GRID_W, dtype=F32), rows)
    n_freq = QK_ROPE // 4
    inv = jnp.power(ROPE_THETA, -jnp.arange(n_freq, dtype=F32) / n_freq)
    ang = jnp.concatenate([row[:, None] * inv, col[:, None] * inv], axis=-1)
    cos, sin = jnp.cos(ang), jnp.sin(ang)
    cos2 = jnp.repeat(cos, 2, axis=-1)
    sin2 = jnp.stack([-sin, sin], axis=-1).reshape(length, QK_ROPE)
    pad = HEAD_SLAB - QK_ROPE
    return (jnp.concatenate([cos2, jnp.ones((length, pad), F32)], axis=-1),
            jnp.concatenate([sin2, jnp.zeros((length, pad), F32)], axis=-1))


def _prep_layer_a(w_in, w_uq, w_uk, w_uv):
    d = w_in.shape[0]
    swap = jnp.arange(QK_ROPE) ^ 1
    w_in = w_in.astype(BF16)
    kr = w_in[:, 384:416]
    z96 = jnp.zeros((d, HEAD_SLAB - QK_ROPE), BF16)
    w_r = jnp.concatenate([w_in[:, 0:384], kr, z96, kr[:, swap], z96, w_in[:, 416:]], axis=1)
    uq = w_uq.reshape(Q_RANK, MLA_HEADS, QK_NOPE + QK_ROPE)
    nope, rp = uq[..., :QK_NOPE], uq[..., QK_NOPE:]
    z32 = jnp.zeros((Q_RANK, MLA_HEADS, 32), F32)
    w1 = jnp.concatenate([rp, z32, nope], axis=-1).reshape(Q_RANK, MLA_HEADS * HEAD_SLAB).astype(BF16)
    w2 = jnp.concatenate([rp[..., swap], z32, jnp.zeros_like(nope)], axis=-1)
    w2 = w2.reshape(Q_RANK, MLA_HEADS * HEAD_SLAB).astype(BF16)
    uk = w_uk.reshape(KV_RANK, MLA_HEADS, QK_NOPE)
    top = jnp.concatenate([jnp.zeros((KV_RANK, MLA_HEADS, 64), F32), uk], axis=-1)
    eye = jnp.broadcast_to(jnp.eye(QK_ROPE, dtype=F32)[:, None, :], (QK_ROPE, MLA_HEADS, QK_ROPE))
    mid = jnp.concatenate([eye, jnp.zeros((QK_ROPE, MLA_HEADS, HEAD_SLAB - QK_ROPE), F32)], axis=-1)
    bot = jnp.zeros((KV_RANK - QK_ROPE, MLA_HEADS, HEAD_SLAB), F32)
    w_k = jnp.concatenate([top, mid, bot], axis=0).reshape(2 * KV_RANK, MLA_HEADS * HEAD_SLAB).astype(BF16)
    return w_r, w1, w2, w_k, w_uv.T.astype(BF16)


def kernel(x_prompt, x_sample, cache_ckv, cache_krope, state_ssd, c, c_ctx, w_mod, b_mod, g_pre, g_final,
           a_w_in, a_g_q, a_g_kv, a_w_uq, a_w_uk, a_w_uv, a_conv_w, a_conv_b, a_ln_g, a_ln_b, a_w_out,
           c_w_in, c_conv_w, c_conv_b, c_dt_bias, c_a_log, c_d, c_g_norm, c_w_out):
    nbp, lp, d = x_prompt.shape
    nbs, ls, _ = x_sample.shape
    ctx_row = nbs
    cond = jnp.concatenate([c, c_ctx[None, :], jnp.zeros((16 - nbs - 1, d), F32)], axis=0)
    mods_all = _mod_call(cond, w_mod, b_mod).reshape(DEPTH, 16, 3, d)
    row2 = lambda v: v.reshape(1, -1)

    mods = mods_all[0]
    w_r, w1, w2, w_k, w_vt = _prep_layer_a(a_w_in[0], a_w_uq[0], a_w_uk[0], a_w_uv[0])
    cos_t, sin_t = _rope_tables(ls)
    gpre = row2(g_pre[0])
    gq, gkv = row2(a_g_q[0]), row2(a_g_kv[0])
    a_wout = a_w_out[0].astype(BF16)
    conv_args = (a_conv_w[0], row2(a_conv_b[0]), row2(a_ln_g[0]), row2(a_ln_b[0]), a_wout)

    qp, kxp, cvp, sgp, ckv_p, kr_p = _a_in_call(x_prompt.reshape(nbp * lp, d), mods, ctx_row, lp, gpre,
                                                w_r, gq, gkv, w1, None, None, None)
    attn_p = _attn_call(qp.reshape(nbp, lp, -1), kxp.reshape(nbp, lp, -1), None, w_k, w_vt)
    xp1 = _a_out_call(attn_p, cvp.reshape(nbp, lp, -1), sgp.reshape(nbp, lp, -1), x_prompt, mods, ctx_row,
                      *conv_args)

    qs, kxs, cvs, sgs = _a_in_call(x_sample.reshape(nbs * ls, d), mods, None, ls, gpre,
                                   w_r, gq, gkv, w1, w2, cos_t, sin_t)
    n_ctx = cache_ckv.shape[2]
    kx_ctx = jnp.concatenate([cache_ckv[:, 0], cache_krope[:, 0],
                              jnp.zeros((nbs, n_ctx, KV_RANK - QK_ROPE), F32)], axis=-1).astype(BF16)
    attn_s = _attn_call(qs.reshape(nbs, ls, -1), kxs.reshape(nbs, ls, -1), kx_ctx, w_k, w_vt)
    xs1 = _a_out_call(attn_s, cvs.reshape(nbs, ls, -1), sgs.reshape(nbs, ls, -1), x_sample, mods, None,
                      *conv_args)

    mods = mods_all[1]
    gpre = row2(g_pre[1])
    w_in = c_w_in[0]
    w_main = w_in.astype(BF16)
    dtw = w_in[:, C_MAIN:]
    zpad = jnp.zeros((d, HEAD_SLAB - SSD_HEADS), F32)
    w_dt = jnp.concatenate([dtw[:, 0:SSD_HEADS], zpad, dtw[:, SSD_HEADS:], zpad], axis=1).astype(BF16)
    pad32 = jnp.zeros((2, HEAD_SLAB - SSD_HEADS), F32)
    dt_b = jnp.concatenate([c_dt_bias[0], pad32], axis=1).reshape(1, 2 * HEAD_SLAB)
    a_neg = jnp.concatenate([-jnp.exp(c_a_log[0]), pad32], axis=1).reshape(2, 1, HEAD_SLAB)
    lane_rep = lambda v: jnp.broadcast_to(v.reshape(-1, 1), (v.size, LANES))
    d_skip = lane_rep(jnp.repeat(c_d[0], SSD_HEAD_DIM))
    gnorm = lane_rep(c_g_norm[0])
    c_wout = c_w_out[0].astype(BF16)
    gfin = row2(g_final)
    cw, cbias = c_conv_w[0], row2(c_conv_b[0])

    xp1f = xp1.reshape(nbp * lp, d)
    up, dtp = _c_in_call(xp1f, mods, ctx_row, lp, gpre, w_main, w_dt, dt_b, cw, cbias)
    y_prompt, st_p = _ssd_call(up, dtp, xp1f, lp, None, a_neg, d_skip, gnorm, c_wout, mods, ctx_row, gfin, True)
    xs1f = xs1.reshape(nbs * ls, d)
    us, dts = _c_in_call(xs1f, mods, None, ls, gpre, w_main, w_dt, dt_b, cw, cbias)
    st0 = state_ssd[:, 0].reshape(nbs, 2, SSD_INNER, SSD_STATE)
    (y_sample,) = _ssd_call(us, dts, xs1f, ls, st0, a_neg, d_skip, gnorm, c_wout, mods, None, gfin, False)

    new_cache_ckv = ckv_p.reshape(nbp, 1, lp, KV_RANK)
    new_cache_krope = kr_p.reshape(nbp, 1, lp, QK_ROPE)
    new_state_ssd = st_p.reshape(nbp, 1, 2, SSD_HEADS, SSD_HEAD_DIM, SSD_STATE)
    return (y_prompt.reshape(nbp, lp, d), y_sample.reshape(nbs, ls, d), new_cache_ckv, new_cache_krope,
            new_state_ssd)
```

```python
import functools
import math

import jax
import jax.numpy as jnp
from jax import lax
from jax.experimental import pallas as pl
from jax.experimental.pallas import tpu as pltpu

F32 = jnp.float32
BF16 = jnp.bfloat16
ACT = BF16

LANES = 128
SUBLANES = 8

D_MODEL = 1024
DEPTH = 2
GRID_W = 64
EPS = 1e-6

MLA_HEADS = 8
Q_RANK = 256
KV_RANK = 128
QK_NOPE = 64
QK_ROPE = 32
V_HEAD = 64
MLA_WIDTH = MLA_HEADS * V_HEAD
ATTN_SCALE = (QK_NOPE + QK_ROPE) ** -0.5
ROPE_THETA = 10000.0
HEAD_SLAB = LANES

CONV_WIDTH = 512
CONV_K = 31
A_GLU0 = 640
A_GATE0 = A_GLU0 + 2 * CONV_WIDTH
A_COLS = A_GATE0 + MLA_WIDTH + CONV_WIDTH

SSD_INNER = 2048
SSD_HEAD_DIM = 64
SSD_HEADS = 32
SSD_GROUPS = 4
SSD_STATE = 128
SSD_CONV_K = 5
SSD_CHUNK = 128
FIN_CHUNK = 512
SSD_CONV_CH = SSD_INNER + 2 * SSD_GROUPS * SSD_STATE
C_MAIN = SSD_INNER + SSD_CONV_CH
C_TILE = 1024
ROW_TILE = 1024
MM_SLABS = 2
A_ROW_TILE = 1024

CONV_STRIDE = 4
CONV_BLOCK = SUBLANES * CONV_STRIDE

VMEM_LIMIT = 56 * 1024 * 1024
NT = (((1,), (1,)), ((), ()))
LOG2E = math.log2(math.e)


def _dot(a, b):
    return jnp.dot(a, b, preferred_element_type=F32)


def _sigmoid(x):
    return 1.0 / (1.0 + jnp.exp(-x))


def _silu(x):
    return x * _sigmoid(x)


def _rms(x, g):
    return x * lax.rsqrt(jnp.mean(x * x, axis=-1, keepdims=True) + EPS) * g


def _params(n_axes):
    return pltpu.CompilerParams(dimension_semantics=("arbitrary",) * n_axes,
                                vmem_limit_bytes=VMEM_LIMIT)


def _full(shape):
    nd = len(shape)
    return pl.BlockSpec(shape, lambda *_: (0,) * nd)


def _mod_kernel(cond_ref, w_ref, b_ref, o_ref):
    s = _silu(cond_ref[...])
    o_ref[0] = _dot(s.astype(BF16), w_ref[0].astype(BF16)) + b_ref[0]


def _mod_call(cond, w_mod, b_mod):
    nrow = cond.shape[0]
    tn = 1024
    return pl.pallas_call(
        _mod_kernel,
        grid=(DEPTH, 3 * D_MODEL // tn),
        in_specs=[pl.BlockSpec((nrow, D_MODEL), lambda l, j: (0, 0)),
                  pl.BlockSpec((1, D_MODEL, tn), lambda l, j: (l, 0, j)),
                  pl.BlockSpec((1, 1, tn), lambda l, j: (l, 0, j))],
        out_specs=pl.BlockSpec((1, nrow, tn), lambda l, j: (l, 0, j)),
        out_shape=jax.ShapeDtypeStruct((DEPTH, nrow, 3 * D_MODEL), F32),
        compiler_params=_params(2),
        name="modulation",
    )(cond, w_mod, b_mod.reshape(DEPTH, 1, 3 * D_MODEL))


def _a_in_kernel(*refs, rope):
    if rope:
        (x_ref, mod_ref, gpre_ref, w_ref, gq_ref, gkv_ref, w1_ref, w2_ref, cos_ref, sin_ref,
         q_ref, kx_ref, cv_ref, sg_ref) = refs
    else:
        (x_ref, mod_ref, gpre_ref, w_ref, gq_ref, gkv_ref, w1_ref,
         q_ref, kx_ref, cv_ref, sg_ref, ckv_ref, kr_ref) = refs
    x = x_ref[...]
    h = _rms(x, gpre_ref[...]) * (1.0 + mod_ref[0, 1:2, :]) + mod_ref[0, 0:1, :]
    hb = h.astype(BF16)

    u0 = _dot(hb, w_ref[:, 0:A_GLU0])
    qn = _rms(u0[:, 0:Q_RANK], gq_ref[...]).astype(BF16)
    ckv = _rms(u0[:, Q_RANK:Q_RANK + KV_RANK], gkv_ref[...])
    kr = u0[:, 384:512]
    qf = _dot(qn, w1_ref[...])
    qscale = ATTN_SCALE * LOG2E
    if rope:
        cos = cos_ref[...]
        sin = sin_ref[...]
        qs = _dot(qn, w2_ref[...])
        for hd in range(MLA_HEADS):
            sl = slice(hd * HEAD_SLAB, (hd + 1) * HEAD_SLAB)
            q_ref[:, sl] = ((qf[:, sl] * cos + qs[:, sl] * sin) * qscale).astype(BF16)
        kr = kr * cos + u0[:, 512:640] * sin
    else:
        q_ref[...] = (qf * qscale).astype(BF16)
        ckv_ref[...] = ckv
        kr_ref[...] = kr[:, 0:QK_ROPE]
    kx_ref[:, 0:KV_RANK] = ckv.astype(BF16)
    kx_ref[:, KV_RANK:2 * KV_RANK] = kr.astype(BF16)

    glu = _dot(hb, w_ref[:, A_GLU0:A_GATE0])
    cv_ref[...] = (glu[:, 0:CONV_WIDTH] * _sigmoid(glu[:, CONV_WIDTH:])).astype(ACT)
    sg_ref[...] = _silu(_dot(hb, w_ref[:, A_GATE0:A_COLS])).astype(ACT)


def _a_in_call(x2, mods, mod_row, seq_len, g_pre, w_in, g_q, g_kv, w1, w2, cos_t, sin_t):
    rows = x2.shape[0]
    tr = A_ROW_TILE
    rope = w2 is not None
    per_seq = max(seq_len // tr, 1)
    if mod_row is None:
        mod_map = lambda i: (i // per_seq, 0, 0)
    else:
        mod_map = lambda i: (mod_row, 0, 0)
    row = lambda i: (i, 0)
    in_specs = [pl.BlockSpec((tr, D_MODEL), row),
                pl.BlockSpec((1, 3, D_MODEL), mod_map),
                _full(g_pre.shape), _full(w_in.shape), _full(g_q.shape), _full(g_kv.shape),
                _full(w1.shape)]
    args = [x2, mods, g_pre, w_in, g_q, g_kv, w1]
    out_specs = [pl.BlockSpec((tr, MLA_HEADS * HEAD_SLAB), row),
                 pl.BlockSpec((tr, 2 * KV_RANK), row),
                 pl.BlockSpec((tr, CONV_WIDTH), row),
                 pl.BlockSpec((tr, MLA_WIDTH + CONV_WIDTH), row)]
    out_shape = [jax.ShapeDtypeStruct((rows, MLA_HEADS * HEAD_SLAB), BF16),
                 jax.ShapeDtypeStruct((rows, 2 * KV_RANK), BF16),
                 jax.ShapeDtypeStruct((rows, CONV_WIDTH), ACT),
                 jax.ShapeDtypeStruct((rows, MLA_WIDTH + CONV_WIDTH), ACT)]
    if rope:
        in_specs += [_full(w2.shape),
                     pl.BlockSpec((tr, HEAD_SLAB), lambda i: (i % per_seq, 0)),
                     pl.BlockSpec((tr, HEAD_SLAB), lambda i: (i % per_seq, 0))]
        args += [w2, cos_t, sin_t]
    else:
        out_specs += [pl.BlockSpec((tr, KV_RANK), row), pl.BlockSpec((tr, QK_ROPE), row)]
        out_shape += [jax.ShapeDtypeStruct((rows, KV_RANK), F32),
                      jax.ShapeDtypeStruct((rows, QK_ROPE), F32)]
    return pl.pallas_call(
        functools.partial(_a_in_kernel, rope=rope),
        grid=(rows // tr,),
        in_specs=in_specs, out_specs=out_specs, out_shape=out_shape,
        compiler_params=_params(1),
        name="a_in_rope" if rope else "a_in",
    )(*args)


def _attn_kernel(*refs, n_ctx, seq_len):
    if n_ctx:
        q_ref, kxn_ref, kxc_ref, wk_ref, wvt_ref, o_ref, k_scr, vt_scr, ot_scr, s_scr = refs
    else:
        q_ref, kxn_ref, wk_ref, wvt_ref, o_ref, k_scr, vt_scr, ot_scr, s_scr = refs

    @pl.when(pl.program_id(1) == 0)
    def _():
        def fill(kx, lo, n):
            k_scr[lo:lo + n, :] = _dot(kx, wk_ref[...]).astype(BF16)
            vt_scr[:, lo:lo + n] = lax.dot_general(
                wvt_ref[...], kx[:, 0:KV_RANK], NT, preferred_element_type=F32).astype(BF16)
        if n_ctx:
            fill(kxc_ref[0], 0, n_ctx)
        fill(kxn_ref[0], n_ctx, seq_len)

    def scores(hd):
        sl = slice(hd * HEAD_SLAB, (hd + 1) * HEAD_SLAB)
        s_scr[hd % 2] = lax.dot_general(k_scr[:, sl], q_ref[0, :, sl], NT, preferred_element_type=F32)

    scores(0)
    for hd in range(MLA_HEADS):
        vrows = slice(hd * V_HEAD, (hd + 1) * V_HEAD)
        if hd + 1 < MLA_HEADS:
            scores(hd + 1)
        st = s_scr[hd % 2]
        p = jnp.exp2(st - jnp.max(st, axis=0, keepdims=True))
        den = jnp.sum(p, axis=0, keepdims=True)
        ot_scr[vrows, :] = _dot(vt_scr[vrows, :], p.astype(BF16)) / den
    o_ref[0] = ot_scr[...].T.astype(ACT)


def _attn_call(q3, kx_new, kx_ctx, w_k, w_vt):
    nb, seq_len, _ = q3.shape
    n_ctx = 0 if kx_ctx is None else kx_ctx.shape[1]
    tq = min(seq_len, 1024)
    lk = n_ctx + seq_len
    in_specs = [pl.BlockSpec((1, tq, MLA_HEADS * HEAD_SLAB), lambda b, i: (b, i, 0)),
                pl.BlockSpec((1, seq_len, 2 * KV_RANK), lambda b, i: (b, 0, 0))]
    args = [q3, kx_new]
    if n_ctx:
        in_specs.append(pl.BlockSpec((1, n_ctx, 2 * KV_RANK), lambda b, i: (b, 0, 0)))
        args.append(kx_ctx)
    in_specs += [_full(w_k.shape), _full(w_vt.shape)]
    args += [w_k, w_vt]
    return pl.pallas_call(
        functools.partial(_attn_kernel, n_ctx=n_ctx, seq_len=seq_len),
        grid=(nb, seq_len // tq),
        in_specs=in_specs,
        out_specs=pl.BlockSpec((1, tq, MLA_WIDTH), lambda b, i: (b, i, 0)),
        out_shape=jax.ShapeDtypeStruct((nb, seq_len, MLA_WIDTH), ACT),
        scratch_shapes=[pltpu.VMEM((lk, MLA_HEADS * HEAD_SLAB), BF16),
                        pltpu.VMEM((MLA_WIDTH, lk), BF16),
                        pltpu.VMEM((MLA_WIDTH, tq), F32),
                        pltpu.VMEM((2, lk, tq), F32)],
        compiler_params=_params(2),
        name="attn_ctx" if n_ctx else "attn",
    )(*args)


def _a_out_kernel(attn_ref, cv_ref, sg_ref, x_ref, mod_ref, cw_ref, cb_ref, lng_ref, lnb_ref, wout_ref,
                  o_ref, pad_scr, cvn_scr, *, seq_len):
    halo = 16
    n_slab = CONV_WIDTH // LANES
    zeros = jnp.zeros((halo, LANES), F32)
    for sl in range(n_slab):
        pad_scr[sl, 0:halo, :] = zeros
        pad_scr[sl, halo + seq_len:2 * halo + seq_len, :] = zeros
        pad_scr[sl, halo:halo + seq_len, :] = cv_ref[0, :, sl * LANES:(sl + 1) * LANES].astype(F32)

    def block(r, carry):
        base = r * CONV_BLOCK
        for sl in range(n_slab):
            lanes = slice(sl * LANES, (sl + 1) * LANES)
            accs = [None] * CONV_STRIDE
            for k in range(CONV_K):
                w = jnp.broadcast_to(cw_ref[k:k + 1, lanes], (SUBLANES, LANES))
                for ph in range(CONV_STRIDE):
                    start = base + (halo - CONV_K // 2 + ph + k)
                    v = pad_scr[sl, pl.ds(start, SUBLANES, stride=CONV_STRIDE), :] * w
                    accs[ph] = v if accs[ph] is None else accs[ph] + v
            for ph in range(CONV_STRIDE):
                cvn_scr[sl, pl.ds(base + ph, SUBLANES, stride=CONV_STRIDE), :] = accs[ph]
        return carry

    lax.fori_loop(0, seq_len // CONV_BLOCK, block, 0)
    conv = jnp.concatenate([cvn_scr[sl] for sl in range(n_slab)], axis=1) + cb_ref[...]
    mu = jnp.mean(conv, axis=-1, keepdims=True)
    cen = conv - mu
    ln = cen * lax.rsqrt(jnp.mean(cen * cen, axis=-1, keepdims=True) + EPS) * lng_ref[...] + lnb_ref[...]
    sg = sg_ref[0].astype(F32)
    mix = jnp.concatenate([attn_ref[0].astype(F32) * sg[:, 0:MLA_WIDTH], _silu(ln) * sg[:, MLA_WIDTH:]], axis=1)
    out = _dot(mix.astype(BF16), wout_ref[...])
    o_ref[0] = x_ref[0] + mod_ref[0, 2:3, :] * out


def _a_out_call(attn3, cv3, sg3, x3, mods, mod_row, conv_w, conv_b, ln_g, ln_b, w_out):
    nb, seq_len, _ = x3.shape
    blk = lambda c: pl.BlockSpec((1, seq_len, c), lambda b: (b, 0, 0))
    mod_map = (lambda b: (b, 0, 0)) if mod_row is None else (lambda b: (mod_row, 0, 0))
    n_slab = CONV_WIDTH // LANES
    return pl.pallas_call(
        functools.partial(_a_out_kernel, seq_len=seq_len),
        grid=(nb,),
        in_specs=[blk(MLA_WIDTH), blk(CONV_WIDTH), blk(MLA_WIDTH + CONV_WIDTH), blk(D_MODEL),
                  pl.BlockSpec((1, 3, D_MODEL), mod_map),
                  _full(conv_w.shape), _full(conv_b.shape), _full(ln_g.shape), _full(ln_b.shape),
                  _full(w_out.shape)],
        out_specs=blk(D_MODEL),
        out_shape=jax.ShapeDtypeStruct((nb, seq_len, D_MODEL), F32),
        scratch_shapes=[pltpu.VMEM((n_slab, seq_len + 32, LANES), F32),
                        pltpu.VMEM((n_slab, seq_len, LANES), F32)],
        compiler_params=_params(1),
        name="a_out",
    )(attn3, cv3, sg3, x3, mods, conv_w, conv_b, ln_g, ln_b, w_out)


def _softplus(x):
    return jnp.maximum(x, 0.0) + jnp.log1p(jnp.exp(-jnp.abs(x)))


def _c_in_kernel(x_ref, mod_ref, gpre_ref, w_ref, wdt_ref, dtb_ref, cw_ref, cb_ref,
                 u_ref, dt_ref, h_scr, pad0_scr, pad1_scr, res_scr, *, seq_len, n_seq):
    j = pl.program_id(1)
    gap = SUBLANES
    pitch = seq_len + gap
    n_slab = C_TILE // LANES
    n_z = SSD_INNER // C_TILE
    n_tiles = C_MAIN // C_TILE
    pads = (pad0_scr, pad1_scr)

    @pl.when(j == 0)
    def _():
        h = _rms(x_ref[...], gpre_ref[...]) * (1.0 + mod_ref[0, 1:2, :]) + mod_ref[0, 0:1, :]
        h_scr[...] = h.astype(BF16)
        dt_ref[...] = _softplus(_dot(h_scr[...], wdt_ref[...]) + dtb_ref[...])

    def project(tile, slabs=range(C_TILE // LANES)):
        lo = tile * C_TILE + slabs[0] * LANES
        return _dot(h_scr[...], w_ref[:, lo:lo + len(slabs) * LANES])

    def stage(u, buf, slabs=range(C_TILE // LANES)):
        zeros = jnp.zeros((gap, LANES), F32)
        pad = pads[buf]
        for i, sl in enumerate(slabs):
            pad[sl, 0:gap, :] = zeros
            for s in range(n_seq):
                lo = gap + s * pitch
                pad[sl, lo:lo + seq_len, :] = u[s * seq_len:(s + 1) * seq_len, i * LANES:(i + 1) * LANES]
                pad[sl, lo + seq_len:lo + pitch, :] = zeros

    def conv(buf, slabs=range(C_TILE // LANES)):
        pad = pads[buf]
        for sl in slabs:
            lanes = slice(sl * LANES, (sl + 1) * LANES)
            ws = [jnp.broadcast_to(cw_ref[k:k + 1, lanes], (SUBLANES, LANES)) for k in range(SSD_CONV_K)]
            bias = jnp.broadcast_to(cb_ref[:, lanes], (SUBLANES, LANES))
            for s in range(n_seq):
                for r in range(seq_len // CONV_BLOCK):
                    for ph in range(CONV_STRIDE):
                        acc = bias
                        for k in range(SSD_CONV_K):
                            start = r * CONV_BLOCK + gap + s * pitch - SSD_CONV_K // 2 + ph + k
                            acc = acc + pad[sl, pl.ds(start, SUBLANES, stride=CONV_STRIDE), :] * ws[k]
                        row0 = r * CONV_BLOCK + s * seq_len + ph
                        res_scr[sl, pl.ds(row0, SUBLANES, stride=CONV_STRIDE), :] = _silu(acc)
            u_ref[sl] = res_scr[sl].astype(ACT)

    for tile in range(n_z):
        @pl.when(j == tile)
        def _(tile=tile):
            u = project(tile)
            for sl in range(n_slab):
                u_ref[sl] = u[:, sl * LANES:(sl + 1) * LANES].astype(ACT)

    @pl.when(j == n_z)
    def _():
        stage(project(n_z), 0)

    for tile in range(n_z + 1, n_tiles):
        @pl.when(j == tile)
        def _(tile=tile):
            buf = (tile - n_z) % 2
            for part in range(n_slab // MM_SLABS):
                slabs = range(part * MM_SLABS, (part + 1) * MM_SLABS)
                stage(project(tile, slabs), buf, slabs)
                conv(1 - buf, slabs)

    @pl.when(j == n_tiles)
    def _():
        conv((n_tiles - 1 - n_z) % 2)


def _c_in_call(x2, mods, mod_row, seq_len, g_pre, w_main, w_dt, dt_b, conv_w, conv_b):
    rows = x2.shape[0]
    tr = ROW_TILE
    n_seq = tr // seq_len
    n_z = SSD_INNER // C_TILE
    n_slab = C_TILE // LANES
    n_tiles = C_MAIN // C_TILE
    mod_map = (lambda i, j: (i, 0, 0)) if mod_row is None else (lambda i, j: (mod_row, 0, 0))
    conv_map = lambda i, j: (0, jnp.clip(j - 1 - n_z, 0, n_tiles - 1 - n_z))
    out_map = lambda i, j: (j - (j >= n_z).astype(jnp.int32), i, 0)
    return pl.pallas_call(
        functools.partial(_c_in_kernel, seq_len=seq_len, n_seq=n_seq),
        grid=(rows // tr, n_tiles + 1),
        in_specs=[pl.BlockSpec((tr, D_MODEL), lambda i, j: (i, 0)),
                  pl.BlockSpec((1, 3, D_MODEL), mod_map),
                  _full(g_pre.shape),
                  pl.BlockSpec((D_MODEL, C_MAIN), lambda i, j: (0, 0), pipeline_mode=pl.Buffered(1)),
                  _full(w_dt.shape), _full(dt_b.shape),
                  pl.BlockSpec((SSD_CONV_K, C_TILE), conv_map),
                  pl.BlockSpec((1, C_TILE), conv_map)],
        out_specs=[pl.BlockSpec((n_slab, tr, LANES), out_map),
                   pl.BlockSpec((tr, 2 * HEAD_SLAB), lambda i, j: (i, 0))],
        out_shape=[jax.ShapeDtypeStruct((C_MAIN // LANES, rows, LANES), ACT),
                   jax.ShapeDtypeStruct((rows, 2 * HEAD_SLAB), F32)],
        scratch_shapes=[pltpu.VMEM((tr, D_MODEL), BF16),
                        pltpu.VMEM((n_slab, SUBLANES + n_seq * (seq_len + SUBLANES), LANES), F32),
                        pltpu.VMEM((n_slab, SUBLANES + n_seq * (seq_len + SUBLANES), LANES), F32),
                        pltpu.VMEM((n_slab, tr, LANES), F32)],
        compiler_params=_params(2),
        name="c_in",
    )(x2, mods, g_pre, w_main, w_dt, dt_b, conv_w, conv_b)


def _split3(a):
    hi = a.astype(BF16)
    r1 = a - hi.astype(F32)
    mid = r1.astype(BF16)
    lo = (r1 - mid.astype(F32)).astype(BF16)
    return hi, mid, lo


def _ssd_direction(d, xs_ref, b_ref, c_ref, dt_ref, aneg_ref, st_scr, xw_scr, xt_scr, xtb_scr, ybuf_scr, slot):
    q = SSD_CHUNK
    for p in range(SSD_INNER // LANES):
        xp_t = xs_ref[p].astype(F32).T
        xt_scr[d, p * LANES:(p + 1) * LANES, :] = xp_t
        xtb_scr[d, p * LANES:(p + 1) * LANES, :] = xp_t.astype(BF16)

    dt = dt_ref[...]
    a = dt * aneg_ref[d]
    ri = lax.broadcasted_iota(jnp.int32, (q, q), 0)
    ci = lax.broadcasted_iota(jnp.int32, (q, q), 1)
    keep_t = (ri <= ci) if d == 0 else (ri >= ci)
    keep_n = (ci <= ri) if d == 0 else (ci >= ri)
    pieces = jnp.concatenate(_split3(a), axis=1)
    c3 = _dot(jnp.where(keep_n, 1.0, 0.0).astype(BF16), pieces)
    cum = c3[:, 0:LANES] + c3[:, LANES:2 * LANES] + c3[:, 2 * LANES:3 * LANES]
    cum_t = cum.T
    dt_t = dt.T
    total_t = jnp.sum(a.T, axis=1, keepdims=True)
    ecum_t = jnp.exp(cum_t)
    wgt_t = jnp.exp(total_t - cum_t) * dt_t
    cdec = jnp.exp(total_t)
    neg_inf = jnp.float32(-jnp.inf)

    def row_bf16(v, e, n):
        return jnp.broadcast_to(v[e:e + 1, :], (n, q)).astype(BF16)

    hpg = SSD_HEADS // SSD_GROUPS
    gw = hpg * SSD_HEAD_DIM
    for g in range(SSD_GROUPS):
        grows = slice(g * gw, (g + 1) * gw)
        bg = b_ref[g].astype(BF16)
        cg = c_ref[g]
        cb_t = lax.dot_general(bg, cg.astype(BF16), NT, preferred_element_type=F32)
        cg_t = cg.astype(F32).T.astype(BF16)
        stg = st_scr[d, grows, :]
        stg_b = stg.astype(BF16)
        decs = []
        for eh in range(hpg):
            e = g * hpg + eh
            rows = slice(e * SSD_HEAD_DIM, (e + 1) * SSD_HEAD_DIM)
            row = jnp.broadcast_to(cum_t[e:e + 1, :], (q, q))
            seg = row - row.T
            m_t = (cb_t * jnp.exp(jnp.where(keep_t, seg, neg_inf))).astype(BF16)
            xe = xtb_scr[d, rows, :]
            lhs = jnp.concatenate([xe * row_bf16(dt_t, e, SSD_HEAD_DIM),
                                   stg_b[eh * SSD_HEAD_DIM:(eh + 1) * SSD_HEAD_DIM, :]], axis=1)
            rhs = jnp.concatenate([m_t, cg_t * row_bf16(ecum_t, e, SSD_STATE)], axis=0)
            ybuf_scr[slot, rows, :] = _dot(lhs, rhs)
            xw_scr[d, rows, :] = xe * row_bf16(wgt_t, e, SSD_HEAD_DIM)
            decs.append(jnp.broadcast_to(cdec[e:e + 1, :], (SSD_HEAD_DIM, SSD_STATE)))
        new = _dot(xw_scr[d, grows, :], bg)
        st_scr[d, grows, :] = stg * jnp.concatenate(decs, axis=0) + new


def _ssd_finish(c, first_slot, last_slot, xt_ref, z_ref, x_ref, dskip_ref, gnorm_ref, wout_ref, gate, gfin_ref,
                ybuf_scr, y_ref):
    q = SSD_CHUNK
    acc = None
    ss = jnp.zeros((1, q), F32)
    slabs_per = FIN_CHUNK // LANES
    for k in range(SSD_INNER // FIN_CHUNK):
        rows = slice(k * FIN_CHUNK, (k + 1) * FIN_CHUNK)
        ytot = ybuf_scr[first_slot, rows, :] + ybuf_scr[last_slot, rows, :] + dskip_ref[rows, :] * xt_ref[rows, :]
        zt = jnp.concatenate([z_ref[p].astype(F32).T for p in range(k * slabs_per, (k + 1) * slabs_per)], axis=0)
        gated = ytot * _silu(zt)
        ss = ss + jnp.sum(gated * gated, axis=0, keepdims=True)
        part = _dot((gated * gnorm_ref[rows, :]).T.astype(BF16), wout_ref[rows, :])
        acc = part if acc is None else acc + part
    inv = lax.rsqrt(ss * (1.0 / SSD_INNER) + EPS)
    inv_col = jnp.broadcast_to(inv, (q, q)).T
    scale = jnp.concatenate([inv_col] * (D_MODEL // LANES), axis=1)
    xn = x_ref[...] + gate * (acc * scale)
    y_ref[pl.ds(pl.multiple_of(c * q, q), q), :] = _rms(xn, gfin_ref[...])


def _ssd_kernel(*refs, n_chunks, has_init, emit_state):
    refs = list(refs)
    xs_refs, b_refs, c_refs, dt_refs, z_refs, x_refs = (refs[0:2], refs[2:4], refs[4:6], refs[6:8], refs[8:10],
                                                        refs[10:12])
    refs = refs[12:]
    st0_ref = refs.pop(0) if has_init else None
    aneg_ref, dskip_ref, gnorm_ref, wout_ref, mod_ref, gfin_ref, y_ref = refs[:7]
    refs = refs[7:]
    stout_ref = refs.pop(0) if emit_state else None
    st_scr, ybuf_scr, xw_scr, xt_scr, xtb_scr = refs

    t = pl.program_id(1)
    second = t >= n_chunks // 2
    chunks = (t, n_chunks - 1 - t)

    @pl.when(t == 0)
    def _():
        if has_init:
            st_scr[...] = st0_ref[0]
        else:
            st_scr[...] = jnp.zeros(st_scr.shape, F32)

    for d in range(2):
        slot = jnp.where(second, n_chunks + d, chunks[d])
        _ssd_direction(d, xs_refs[d], b_refs[d], c_refs[d], dt_refs[d], aneg_ref, st_scr, xw_scr, xt_scr, xtb_scr,
                       ybuf_scr, slot)

    if emit_state:
        @pl.when(t == n_chunks - 1)
        def _():
            stout_ref[0] = st_scr[...]

    @pl.when(second)
    def _():
        for d in range(2):
            _ssd_finish(chunks[d], chunks[d], n_chunks + d, xt_scr.at[d], z_refs[d], x_refs[d], dskip_ref,
                        gnorm_ref, wout_ref, mod_ref[0, 2:3, :], gfin_ref, ybuf_scr, y_ref)


def _ssd_call(u3, dt2, x2, seq_len, st0, a_neg, d_skip, g_norm, w_out, mods, mod_row, g_final, emit_state):
    rows = x2.shape[0]
    q = SSD_CHUNK
    nc = seq_len // q
    half = nc // 2
    nb = rows // seq_len
    has_init = st0 is not None
    n_slab = SSD_INNER // LANES
    bslab = 2 * n_slab // SSD_GROUPS

    fwd = lambda b, t: b * nc + t
    bwd = lambda b, t: b * nc + nc - 1 - t
    fwd_fin = lambda b, t: b * nc + jnp.maximum(t, half)
    bwd_fin = lambda b, t: b * nc + jnp.minimum(nc - 1 - t, half - 1)
    mod_map = (lambda b, t: (b, 0, 0)) if mod_row is None else (lambda b, t: (mod_row, 0, 0))

    def pair(block, make_map):
        return [pl.BlockSpec(block, make_map(fwd)), pl.BlockSpec(block, make_map(bwd))]

    in_specs = (pair((n_slab, q, LANES), lambda ch: (lambda b, t: (1, ch(b, t), 0)))
                + pair((SSD_GROUPS, q, LANES), lambda ch: (lambda b, t: (bslab, ch(b, t), 0)))
                + pair((SSD_GROUPS, q, LANES), lambda ch: (lambda b, t: (bslab + 1, ch(b, t), 0)))
                + [pl.BlockSpec((q, HEAD_SLAB), lambda b, t: (fwd(b, t), 0)),
                   pl.BlockSpec((q, HEAD_SLAB), lambda b, t: (bwd(b, t), 1)),
                   pl.BlockSpec((n_slab, q, LANES), lambda b, t: (0, fwd_fin(b, t), 0)),
                   pl.BlockSpec((n_slab, q, LANES), lambda b, t: (0, bwd_fin(b, t), 0)),
                   pl.BlockSpec((q, D_MODEL), lambda b, t: (fwd_fin(b, t), 0)),
                   pl.BlockSpec((q, D_MODEL), lambda b, t: (bwd_fin(b, t), 0))])
    args = [u3] * 6 + [dt2, dt2, u3, u3, x2, x2]
    if has_init:
        in_specs.append(pl.BlockSpec((1, 2, SSD_INNER, SSD_STATE), lambda b, t: (b, 0, 0, 0)))
        args.append(st0)
    in_specs += [_full(a_neg.shape), _full(d_skip.shape), _full(g_norm.shape), _full(w_out.shape),
                 pl.BlockSpec((1, 3, D_MODEL), mod_map), _full(g_final.shape)]
    args += [a_neg, d_skip, g_norm, w_out, mods, g_final]
    out_specs = [pl.BlockSpec((seq_len, D_MODEL), lambda b, t: (b, 0))]
    out_shape = [jax.ShapeDtypeStruct((rows, D_MODEL), F32)]
    if emit_state:
        out_specs.append(pl.BlockSpec((1, 2, SSD_INNER, SSD_STATE), lambda b, t: (b, 0, 0, 0)))
        out_shape.append(jax.ShapeDtypeStruct((nb, 2, SSD_INNER, SSD_STATE), F32))
    return pl.pallas_call(
        functools.partial(_ssd_kernel, n_chunks=nc, has_init=has_init, emit_state=emit_state),
        grid=(nb, nc),
        in_specs=in_specs, out_specs=out_specs, out_shape=out_shape,
        scratch_shapes=[pltpu.VMEM((2, SSD_INNER, SSD_STATE), F32),
                        pltpu.VMEM((nc + 2, SSD_INNER, q), F32),
                        pltpu.VMEM((2, SSD_INNER, q), BF16),
                        pltpu.VMEM((2, SSD_INNER, q), F32),
                        pltpu.VMEM((2, SSD_INNER, q), BF16)],
        compiler_params=_params(2),
        name="ssd_init" if has_init else "ssd",
    )(*args)


def _rope_tables(length):
    rows = length // GRID_W
    row = jnp.repeat(jnp.arange(rows, dtype=F32), GRID_W)
    col = jnp.tile(jnp.arange(GRID_W, dtype=F32), rows)
    n_freq = QK_ROPE // 4
    inv = jnp.power(ROPE_THETA, -jnp.arange(n_freq, dtype=F32) / n_freq)
    ang = jnp.concatenate([row[:, None] * inv, col[:, None] * inv], axis=-1)
    cos, sin = jnp.cos(ang), jnp.sin(ang)
    cos2 = jnp.repeat(cos, 2, axis=-1)
    sin2 = jnp.stack([-sin, sin], axis=-1).reshape(length, QK_ROPE)
    pad = HEAD_SLAB - QK_ROPE
    return (jnp.concatenate([cos2, jnp.ones((length, pad), F32)], axis=-1),
            jnp.concatenate([sin2, jnp.zeros((length, pad), F32)], axis=-1))


def _prep_layer_a(w_in, w_uq, w_uk, w_uv):
    d = w_in.shape[0]
    swap = jnp.arange(QK_ROPE) ^ 1
    w_in = w_in.astype(BF16)
    kr = w_in[:, 384:416]
    z96 = jnp.zeros((d, HEAD_SLAB - QK_ROPE), BF16)
    w_r = jnp.concatenate([w_in[:, 0:384], kr, z96, kr[:, swap], z96, w_in[:, 416:]], axis=1)
    uq = w_uq.reshape(Q_RANK, MLA_HEADS, QK_NOPE + QK_ROPE)
    nope, rp = uq[..., :QK_NOPE], uq[..., QK_NOPE:]
    z32 = jnp.zeros((Q_RANK, MLA_HEADS, 32), F32)
    w1 = jnp.concatenate([rp, z32, nope], axis=-1).reshape(Q_RANK, MLA_HEADS * HEAD_SLAB).astype(BF16)
    w2 = jnp.concatenate([rp[..., swap], z32, jnp.zeros_like(nope)], axis=-1)
    w2 = w2.reshape(Q_RANK, MLA_HEADS * HEAD_SLAB).astype(BF16)
    uk = w_uk.reshape(KV_RANK, MLA_HEADS, QK_NOPE)
    top = jnp.concatenate([jnp.zeros((KV_RANK, MLA_HEADS, 64), F32), uk], axis=-1)
    eye = jnp.broadcast_to(jnp.eye(QK_ROPE, dtype=F32)[:, None, :], (QK_ROPE, MLA_HEADS, QK_ROPE))
    mid = jnp.concatenate([eye, jnp.zeros((QK_ROPE, MLA_HEADS, HEAD_SLAB - QK_ROPE), F32)], axis=-1)
    bot = jnp.zeros((KV_RANK - QK_ROPE, MLA_HEADS, HEAD_SLAB), F32)
    w_k = jnp.concatenate([top, mid, bot], axis=0).reshape(2 * KV_RANK, MLA_HEADS * HEAD_SLAB).astype(BF16)
    return w_r, w1, w2, w_k, w_uv.T.astype(BF16)


def kernel(x_prompt, x_sample, cache_ckv, cache_krope, state_ssd, c, c_ctx, w_mod, b_mod, g_pre, g_final,
           a_w_in, a_g_q, a_g_kv, a_w_uq, a_w_uk, a_w_uv, a_conv_w, a_conv_b, a_ln_g, a_ln_b, a_w_out,
           c_w_in, c_conv_w, c_conv_b, c_dt_bias, c_a_log, c_d, c_g_norm, c_w_out):
    nbp, lp, d = x_prompt.shape
    nbs, ls, _ = x_sample.shape
    ctx_row = nbs
    cond = jnp.concatenate([c, c_ctx[None, :], jnp.zeros((16 - nbs - 1, d), F32)], axis=0)
    mods_all = _mod_call(cond, w_mod, b_mod).reshape(DEPTH, 16, 3, d)
    row2 = lambda v: v.reshape(1, -1)

    mods = mods_all[0]
    w_r, w1, w2, w_k, w_vt = _prep_layer_a(a_w_in[0], a_w_uq[0], a_w_uk[0], a_w_uv[0])
    cos_t, sin_t = _rope_tables(ls)
    gpre = row2(g_pre[0])
    gq, gkv = row2(a_g_q[0]), row2(a_g_kv[0])
    a_wout = a_w_out[0].astype(BF16)
    conv_args = (a_conv_w[0], row2(a_conv_b[0]), row2(a_ln_g[0]), row2(a_ln_b[0]), a_wout)

    qp, kxp, cvp, sgp, ckv_p, kr_p = _a_in_call(x_prompt.reshape(nbp * lp, d), mods, ctx_row, lp, gpre,
                                                w_r, gq, gkv, w1, None, None, None)
    attn_p = _attn_call(qp.reshape(nbp, lp, -1), kxp.reshape(nbp, lp, -1), None, w_k, w_vt)
    xp1 = _a_out_call(attn_p, cvp.reshape(nbp, lp, -1), sgp.reshape(nbp, lp, -1), x_prompt, mods, ctx_row,
                      *conv_args)

    qs, kxs, cvs, sgs = _a_in_call(x_sample.reshape(nbs * ls, d), mods, None, ls, gpre,
                                   w_r, gq, gkv, w1, w2, cos_t, sin_t)
    n_ctx = cache_ckv.shape[2]
    kx_ctx = jnp.concatenate([cache_ckv[:, 0], cache_krope[:, 0],
                              jnp.zeros((nbs, n_ctx, KV_RANK - QK_ROPE), F32)], axis=-1).astype(BF16)
    attn_s = _attn_call(qs.reshape(nbs, ls, -1), kxs.reshape(nbs, ls, -1), kx_ctx, w_k, w_vt)
    xs1 = _a_out_call(attn_s, cvs.reshape(nbs, ls, -1), sgs.reshape(nbs, ls, -1), x_sample, mods, None,
                      *conv_args)

    mods = mods_all[1]
    gpre = row2(g_pre[1])
    w_in = c_w_in[0]
    w_main = w_in.astype(BF16)
    dtw = w_in[:, C_MAIN:]
    zpad = jnp.zeros((d, HEAD_SLAB - SSD_HEADS), F32)
    w_dt = jnp.concatenate([dtw[:, 0:SSD_HEADS], zpad, dtw[:, SSD_HEADS:], zpad], axis=1).astype(BF16)
    pad32 = jnp.zeros((2, HEAD_SLAB - SSD_HEADS), F32)
    dt_b = jnp.concatenate([c_dt_bias[0], pad32], axis=1).reshape(1, 2 * HEAD_SLAB)
    a_neg = jnp.concatenate([-jnp.exp(c_a_log[0]), pad32], axis=1).reshape(2, 1, HEAD_SLAB)
    lane_rep = lambda v: jnp.broadcast_to(v.reshape(-1, 1), (v.size, LANES))
    d_skip = lane_rep(jnp.repeat(c_d[0], SSD_HEAD_DIM))
    gnorm = lane_rep(c_g_norm[0])
    c_wout = c_w_out[0].astype(BF16)
    gfin = row2(g_final)
    cw, cbias = c_conv_w[0], row2(c_conv_b[0])

    xp1f = xp1.reshape(nbp * lp, d)
    up, dtp = _c_in_call(xp1f, mods, ctx_row, lp, gpre, w_main, w_dt, dt_b, cw, cbias)
    y_prompt, st_p = _ssd_call(up, dtp, xp1f, lp, None, a_neg, d_skip, gnorm, c_wout, mods, ctx_row, gfin, True)
    xs1f = xs1.reshape(nbs * ls, d)
    us, dts = _c_in_call(xs1f, mods, None, ls, gpre, w_main, w_dt, dt_b, cw, cbias)
    st0 = state_ssd[:, 0].reshape(nbs, 2, SSD_INNER, SSD_STATE)
    (y_sample,) = _ssd_call(us, dts, xs1f, ls, st0, a_neg, d_skip, gnorm, c_wout, mods, None, gfin, False)

    new_cache_ckv = ckv_p.reshape(nbp, 1, lp, KV_RANK)
    new_cache_krope = kr_p.reshape(nbp, 1, lp, QK_ROPE)
    new_state_ssd = st_p.reshape(nbp, 1, 2, SSD_HEADS, SSD_HEAD_DIM, SSD_STATE)
    return (y_prompt.reshape(nbp, lp, d), y_sample.reshape(nbs, ls, d), new_cache_ckv, new_cache_krope,
            new_state_ssd)
```

```python
import functools
import math

import jax
import jax.numpy as jnp
from jax import lax
from jax.experimental import pallas as pl
from jax.experimental.pallas import tpu as pltpu

F32 = jnp.float32
BF16 = jnp.bfloat16
ACT = BF16

LANES = 128
SUBLANES = 8

D_MODEL = 1024
DEPTH = 2
GRID_W = 64
EPS = 1e-6

MLA_HEADS = 8
Q_RANK = 256
KV_RANK = 128
QK_NOPE = 64
QK_ROPE = 32
V_HEAD = 64
MLA_WIDTH = MLA_HEADS * V_HEAD
ATTN_SCALE = (QK_NOPE + QK_ROPE) ** -0.5
ROPE_THETA = 10000.0
HEAD_SLAB = LANES

CONV_WIDTH = 512
CONV_K = 31
A_GLU0 = 640
A_GATE0 = A_GLU0 + 2 * CONV_WIDTH
A_COLS = A_GATE0 + MLA_WIDTH + CONV_WIDTH

SSD_INNER = 2048
SSD_HEAD_DIM = 64
SSD_HEADS = 32
SSD_GROUPS = 4
SSD_STATE = 128
SSD_CONV_K = 5
SSD_CHUNK = 128
FIN_CHUNK = 512
SSD_CONV_CH = SSD_INNER + 2 * SSD_GROUPS * SSD_STATE
C_MAIN = SSD_INNER + SSD_CONV_CH
C_TILE = 1024
ROW_TILE = 1024
MM_SLABS = 2
A_ROW_TILE = 1024

CONV_STRIDE = 4
CONV_BLOCK = SUBLANES * CONV_STRIDE

VMEM_LIMIT = 56 * 1024 * 1024
NT = (((1,), (1,)), ((), ()))
LOG2E = math.log2(math.e)


def _dot(a, b):
    return jnp.dot(a, b, preferred_element_type=F32)


def _sigmoid(x):
    return 1.0 / (1.0 + jnp.exp(-x))


def _silu(x):
    return x * _sigmoid(x)


def _rms(x, g):
    return x * lax.rsqrt(jnp.mean(x * x, axis=-1, keepdims=True) + EPS) * g


def _params(n_axes):
    return pltpu.CompilerParams(dimension_semantics=("arbitrary",) * n_axes,
                                vmem_limit_bytes=VMEM_LIMIT)


def _full(shape):
    nd = len(shape)
    return pl.BlockSpec(shape, lambda *_: (0,) * nd)


def _mod_kernel(cond_ref, w_ref, b_ref, o_ref):
    s = _silu(cond_ref[...])
    o_ref[0] = _dot(s.astype(BF16), w_ref[0].astype(BF16)) + b_ref[0]


def _mod_call(cond, w_mod, b_mod):
    nrow = cond.shape[0]
    tn = 1024
    return pl.pallas_call(
        _mod_kernel,
        grid=(DEPTH, 3 * D_MODEL // tn),
        in_specs=[pl.BlockSpec((nrow, D_MODEL), lambda l, j: (0, 0)),
                  pl.BlockSpec((1, D_MODEL, tn), lambda l, j: (l, 0, j)),
                  pl.BlockSpec((1, 1, tn), lambda l, j: (l, 0, j))],
        out_specs=pl.BlockSpec((1, nrow, tn), lambda l, j: (l, 0, j)),
        out_shape=jax.ShapeDtypeStruct((DEPTH, nrow, 3 * D_MODEL), F32),
        compiler_params=_params(2),
        name="modulation",
    )(cond, w_mod, b_mod.reshape(DEPTH, 1, 3 * D_MODEL))


def _a_in_kernel(*refs, rope):
    if rope:
        (x_ref, mod_ref, gpre_ref, w_ref, gq_ref, gkv_ref, w1_ref, w2_ref, cos_ref, sin_ref,
         q_ref, kx_ref, cv_ref, sg_ref) = refs
    else:
        (x_ref, mod_ref, gpre_ref, w_ref, gq_ref, gkv_ref, w1_ref,
         q_ref, kx_ref, cv_ref, sg_ref, ckv_ref, kr_ref) = refs
    x = x_ref[...]
    h = _rms(x, gpre_ref[...]) * (1.0 + mod_ref[0, 1:2, :]) + mod_ref[0, 0:1, :]
    hb = h.astype(BF16)

    u0 = _dot(hb, w_ref[:, 0:A_GLU0])
    qn = _rms(u0[:, 0:Q_RANK], gq_ref[...]).astype(BF16)
    ckv = _rms(u0[:, Q_RANK:Q_RANK + KV_RANK], gkv_ref[...])
    kr = u0[:, 384:512]
    qf = _dot(qn, w1_ref[...])
    qscale = ATTN_SCALE * LOG2E
    if rope:
        cos = cos_ref[...]
        sin = sin_ref[...]
        qs = _dot(qn, w2_ref[...])
        for hd in range(MLA_HEADS):
            sl = slice(hd * HEAD_SLAB, (hd + 1) * HEAD_SLAB)
            q_ref[:, sl] = ((qf[:, sl] * cos + qs[:, sl] * sin) * qscale).astype(BF16)
        kr = kr * cos + u0[:, 512:640] * sin
    else:
        q_ref[...] = (qf * qscale).astype(BF16)
        ckv_ref[...] = ckv
        kr_ref[...] = kr[:, 0:QK_ROPE]
    kx_ref[:, 0:KV_RANK] = ckv.astype(BF16)
    kx_ref[:, KV_RANK:2 * KV_RANK] = kr.astype(BF16)

    glu = _dot(hb, w_ref[:, A_GLU0:A_GATE0])
    cv_ref[...] = (glu[:, 0:CONV_WIDTH] * _sigmoid(glu[:, CONV_WIDTH:])).astype(ACT)
    sg_ref[...] = _silu(_dot(hb, w_ref[:, A_GATE0:A_COLS])).astype(ACT)


def _a_in_call(x2, mods, mod_row, seq_len, g_pre, w_in, g_q, g_kv, w1, w2, cos_t, sin_t):
    rows = x2.shape[0]
    tr = A_ROW_TILE
    rope = w2 is not None
    per_seq = max(seq_len // tr, 1)
    if mod_row is None:
        mod_map = lambda i: (i // per_seq, 0, 0)
    else:
        mod_map = lambda i: (mod_row, 0, 0)
    row = lambda i: (i, 0)
    in_specs = [pl.BlockSpec((tr, D_MODEL), row),
                pl.BlockSpec((1, 3, D_MODEL), mod_map),
                _full(g_pre.shape), _full(w_in.shape), _full(g_q.shape), _full(g_kv.shape),
                _full(w1.shape)]
    args = [x2, mods, g_pre, w_in, g_q, g_kv, w1]
    out_specs = [pl.BlockSpec((tr, MLA_HEADS * HEAD_SLAB), row),
                 pl.BlockSpec((tr, 2 * KV_RANK), row),
                 pl.BlockSpec((tr, CONV_WIDTH), row),
                 pl.BlockSpec((tr, MLA_WIDTH + CONV_WIDTH), row)]
    out_shape = [jax.ShapeDtypeStruct((rows, MLA_HEADS * HEAD_SLAB), BF16),
                 jax.ShapeDtypeStruct((rows, 2 * KV_RANK), BF16),
                 jax.ShapeDtypeStruct((rows, CONV_WIDTH), ACT),
                 jax.ShapeDtypeStruct((rows, MLA_WIDTH + CONV_WIDTH), ACT)]
    if rope:
        in_specs += [_full(w2.shape),
                     pl.BlockSpec((tr, HEAD_SLAB), lambda i: (i % per_seq, 0)),
                     pl.BlockSpec((tr, HEAD_SLAB), lambda i: (i % per_seq, 0))]
        args += [w2, cos_t, sin_t]
    else:
        out_specs += [pl.BlockSpec((tr, KV_RANK), row), pl.BlockSpec((tr, QK_ROPE), row)]
        out_shape += [jax.ShapeDtypeStruct((rows, KV_RANK), F32),
                      jax.ShapeDtypeStruct((rows, QK_ROPE), F32)]
    return pl.pallas_call(
        functools.partial(_a_in_kernel, rope=rope),
        grid=(rows // tr,),
        in_specs=in_specs, out_specs=out_specs, out_shape=out_shape,
        compiler_params=_params(1),
        name="a_in_rope" if rope else "a_in",
    )(*args)


def _attn_kernel(*refs, n_ctx, seq_len):
    if n_ctx:
        q_ref, kxn_ref, kxc_ref, wk_ref, wvt_ref, o_ref, k_scr, vt_scr, ot_scr, s_scr = refs
    else:
        q_ref, kxn_ref, wk_ref, wvt_ref, o_ref, k_scr, vt_scr, ot_scr, s_scr = refs

    @pl.when(pl.program_id(1) == 0)
    def _():
        def fill(kx, lo, n):
            k_scr[lo:lo + n, :] = _dot(kx, wk_ref[...]).astype(BF16)
            vt_scr[:, lo:lo + n] = lax.dot_general(
                wvt_ref[...], kx[:, 0:KV_RANK], NT, preferred_element_type=F32).astype(BF16)
        if n_ctx:
            fill(kxc_ref[0], 0, n_ctx)
        fill(kxn_ref[0], n_ctx, seq_len)

    def scores(hd):
        sl = slice(hd * HEAD_SLAB, (hd + 1) * HEAD_SLAB)
        s_scr[hd % 2] = lax.dot_general(k_scr[:, sl], q_ref[0, :, sl], NT, preferred_element_type=F32)

    scores(0)
    for hd in range(MLA_HEADS):
        vrows = slice(hd * V_HEAD, (hd + 1) * V_HEAD)
        if hd + 1 < MLA_HEADS:
            scores(hd + 1)
        st = s_scr[hd % 2]
        p = jnp.exp2(st - jnp.max(st, axis=0, keepdims=True))
        den = jnp.sum(p, axis=0, keepdims=True)
        ot_scr[vrows, :] = _dot(vt_scr[vrows, :], p.astype(BF16)) / den
    o_ref[0] = ot_scr[...].T.astype(ACT)


def _attn_call(q3, kx_new, kx_ctx, w_k, w_vt):
    nb, seq_len, _ = q3.shape
    n_ctx = 0 if kx_ctx is None else kx_ctx.shape[1]
    tq = min(seq_len, 1024)
    lk = n_ctx + seq_len
    in_specs = [pl.BlockSpec((1, tq, MLA_HEADS * HEAD_SLAB), lambda b, i: (b, i, 0)),
                pl.BlockSpec((1, seq_len, 2 * KV_RANK), lambda b, i: (b, 0, 0))]
    args = [q3, kx_new]
    if n_ctx:
        in_specs.append(pl.BlockSpec((1, n_ctx, 2 * KV_RANK), lambda b, i: (b, 0, 0)))
        args.append(kx_ctx)
    in_specs += [_full(w_k.shape), _full(w_vt.shape)]
    args += [w_k, w_vt]
    return pl.pallas_call(
        functools.partial(_attn_kernel, n_ctx=n_ctx, seq_len=seq_len),
        grid=(nb, seq_len // tq),
        in_specs=in_specs,
        out_specs=pl.BlockSpec((1, tq, MLA_WIDTH), lambda b, i: (b, i, 0)),
        out_shape=jax.ShapeDtypeStruct((nb, seq_len, MLA_WIDTH), ACT),
        scratch_shapes=[pltpu.VMEM((lk, MLA_HEADS * HEAD_SLAB), BF16),
                        pltpu.VMEM((MLA_WIDTH, lk), BF16),
                        pltpu.VMEM((MLA_WIDTH, tq), F32),
                        pltpu.VMEM((2, lk, tq), F32)],
        compiler_params=_params(2),
        name="attn_ctx" if n_ctx else "attn",
    )(*args)


def _a_out_kernel(attn_ref, cv_ref, sg_ref, x_ref, mod_ref, cw_ref, cb_ref, lng_ref, lnb_ref, wout_ref,
                  o_ref, pad_scr, cvn_scr, *, seq_len):
    halo = 16
    n_slab = CONV_WIDTH // LANES
    zeros = jnp.zeros((halo, LANES), F32)
    for sl in range(n_slab):
        pad_scr[sl, 0:halo, :] = zeros
        pad_scr[sl, halo + seq_len:2 * halo + seq_len, :] = zeros
        pad_scr[sl, halo:halo + seq_len, :] = cv_ref[0, :, sl * LANES:(sl + 1) * LANES].astype(F32)

    def block(r, carry):
        base = r * CONV_BLOCK
        for sl in range(n_slab):
            lanes = slice(sl * LANES, (sl + 1) * LANES)
            accs = [None] * CONV_STRIDE
            for k in range(CONV_K):
                w = jnp.broadcast_to(cw_ref[k:k + 1, lanes], (SUBLANES, LANES))
                for ph in range(CONV_STRIDE):
                    start = base + (halo - CONV_K // 2 + ph + k)
                    v = pad_scr[sl, pl.ds(start, SUBLANES, stride=CONV_STRIDE), :] * w
                    accs[ph] = v if accs[ph] is None else accs[ph] + v
            for ph in range(CONV_STRIDE):
                cvn_scr[sl, pl.ds(base + ph, SUBLANES, stride=CONV_STRIDE), :] = accs[ph]
        return carry

    lax.fori_loop(0, seq_len // CONV_BLOCK, block, 0)
    conv = jnp.concatenate([cvn_scr[sl] for sl in range(n_slab)], axis=1) + cb_ref[...]
    mu = jnp.mean(conv, axis=-1, keepdims=True)
    cen = conv - mu
    ln = cen * lax.rsqrt(jnp.mean(cen * cen, axis=-1, keepdims=True) + EPS) * lng_ref[...] + lnb_ref[...]
    sg = sg_ref[0].astype(F32)
    mix = jnp.concatenate([attn_ref[0].astype(F32) * sg[:, 0:MLA_WIDTH], _silu(ln) * sg[:, MLA_WIDTH:]], axis=1)
    out = _dot(mix.astype(BF16), wout_ref[...])
    o_ref[0] = x_ref[0] + mod_ref[0, 2:3, :] * out


def _a_out_call(attn3, cv3, sg3, x3, mods, mod_row, conv_w, conv_b, ln_g, ln_b, w_out):
    nb, seq_len, _ = x3.shape
    blk = lambda c: pl.BlockSpec((1, seq_len, c), lambda b: (b, 0, 0))
    mod_map = (lambda b: (b, 0, 0)) if mod_row is None else (lambda b: (mod_row, 0, 0))
    n_slab = CONV_WIDTH // LANES
    return pl.pallas_call(
        functools.partial(_a_out_kernel, seq_len=seq_len),
        grid=(nb,),
        in_specs=[blk(MLA_WIDTH), blk(CONV_WIDTH), blk(MLA_WIDTH + CONV_WIDTH), blk(D_MODEL),
                  pl.BlockSpec((1, 3, D_MODEL), mod_map),
                  _full(conv_w.shape), _full(conv_b.shape), _full(ln_g.shape), _full(ln_b.shape),
                  _full(w_out.shape)],
        out_specs=blk(D_MODEL),
        out_shape=jax.ShapeDtypeStruct((nb, seq_len, D_MODEL), F32),
        scratch_shapes=[pltpu.VMEM((n_slab, seq_len + 32, LANES), F32),
                        pltpu.VMEM((n_slab, seq_len, LANES), F32)],
        compiler_params=_params(1),
        name="a_out",
    )(attn3, cv3, sg3, x3, mods, conv_w, conv_b, ln_g, ln_b, w_out)


def _softplus(x):
    return jnp.maximum(x, 0.0) + jnp.log1p(jnp.exp(-jnp.abs(x)))


def _c_in_kernel(x_ref, mod_ref, gpre_ref, w_ref, wdt_ref, dtb_ref, cw_ref, cb_ref,
                 u_ref, dt_ref, h_scr, pad0_scr, pad1_scr, res_scr, *, seq_len, n_seq):
    j = pl.program_id(1)
    gap = SUBLANES
    pitch = seq_len + gap
    n_slab = C_TILE // LANES
    n_z = SSD_INNER // C_TILE
    n_tiles = C_MAIN // C_TILE
    pads = (pad0_scr, pad1_scr)

    @pl.when(j == 0)
    def _():
        h = _rms(x_ref[...], gpre_ref[...]) * (1.0 + mod_ref[0, 1:2, :]) + mod_ref[0, 0:1, :]
        h_scr[...] = h.astype(BF16)
        dt_ref[...] = _softplus(_dot(h_scr[...], wdt_ref[...]) + dtb_ref[...])

    def project(tile, slabs=range(C_TILE // LANES)):
        lo = tile * C_TILE + slabs[0] * LANES
        return _dot(h_scr[...], w_ref[:, lo:lo + len(slabs) * LANES])

    def stage(u, buf, slabs=range(C_TILE // LANES)):
        zeros = jnp.zeros((gap, LANES), F32)
        pad = pads[buf]
        for i, sl in enumerate(slabs):
            pad[sl, 0:gap, :] = zeros
            for s in range(n_seq):
                lo = gap + s * pitch
                pad[sl, lo:lo + seq_len, :] = u[s * seq_len:(s + 1) * seq_len, i * LANES:(i + 1) * LANES]
                pad[sl, lo + seq_len:lo + pitch, :] = zeros

    def conv(buf, slabs=range(C_TILE // LANES)):
        pad = pads[buf]
        for sl in slabs:
            lanes = slice(sl * LANES, (sl + 1) * LANES)
            ws = [jnp.broadcast_to(cw_ref[k:k + 1, lanes], (SUBLANES, LANES)) for k in range(SSD_CONV_K)]
            bias = jnp.broadcast_to(cb_ref[:, lanes], (SUBLANES, LANES))
            for s in range(n_seq):
                for r in range(seq_len // CONV_BLOCK):
                    for ph in range(CONV_STRIDE):
                        acc = bias
                        for k in range(SSD_CONV_K):
                            start = r * CONV_BLOCK + gap + s * pitch - SSD_CONV_K // 2 + ph + k
                            acc = acc + pad[sl, pl.ds(start, SUBLANES, stride=CONV_STRIDE), :] * ws[k]
                        row0 = r * CONV_BLOCK + s * seq_len + ph
                        res_scr[sl, pl.ds(row0, SUBLANES, stride=CONV_STRIDE), :] = _silu(acc)
            u_ref[sl] = res_scr[sl].astype(ACT)

    for tile in range(n_z):
        @pl.when(j == tile)
        def _(tile=tile):
            u = project(tile)
            for sl in range(n_slab):
                u_ref[sl] = u[:, sl * LANES:(sl + 1) * LANES].astype(ACT)

    @pl.when(j == n_z)
    def _():
        stage(project(n_z), 0)

    for tile in range(n_z + 1, n_tiles):
        @pl.when(j == tile)
        def _(tile=tile):
            buf = (tile - n_z) % 2
            for part in range(n_slab // MM_SLABS):
                slabs = range(part * MM_SLABS, (part + 1) * MM_SLABS)
                stage(project(tile, slabs), buf, slabs)
                conv(1 - buf, slabs)

    @pl.when(j == n_tiles)
    def _():
        conv((n_tiles - 1 - n_z) % 2)


def _c_in_call(x2, mods, mod_row, seq_len, g_pre, w_main, w_dt, dt_b, conv_w, conv_b):
    rows = x2.shape[0]
    tr = ROW_TILE
    n_seq = tr // seq_len
    n_z = SSD_INNER // C_TILE
    n_slab = C_TILE // LANES
    n_tiles = C_MAIN // C_TILE
    mod_map = (lambda i, j: (i, 0, 0)) if mod_row is None else (lambda i, j: (mod_row, 0, 0))
    conv_map = lambda i, j: (0, jnp.clip(j - 1 - n_z, 0, n_tiles - 1 - n_z))
    out_map = lambda i, j: (j - (j >= n_z).astype(jnp.int32), i, 0)
    return pl.pallas_call(
        functools.partial(_c_in_kernel, seq_len=seq_len, n_seq=n_seq),
        grid=(rows // tr, n_tiles + 1),
        in_specs=[pl.BlockSpec((tr, D_MODEL), lambda i, j: (i, 0)),
                  pl.BlockSpec((1, 3, D_MODEL), mod_map),
                  _full(g_pre.shape),
                  pl.BlockSpec((D_MODEL, C_MAIN), lambda i, j: (0, 0), pipeline_mode=pl.Buffered(1)),
                  _full(w_dt.shape), _full(dt_b.shape),
                  pl.BlockSpec((SSD_CONV_K, C_TILE), conv_map),
                  pl.BlockSpec((1, C_TILE), conv_map)],
        out_specs=[pl.BlockSpec((n_slab, tr, LANES), out_map),
                   pl.BlockSpec((tr, 2 * HEAD_SLAB), lambda i, j: (i, 0))],
        out_shape=[jax.ShapeDtypeStruct((C_MAIN // LANES, rows, LANES), ACT),
                   jax.ShapeDtypeStruct((rows, 2 * HEAD_SLAB), F32)],
        scratch_shapes=[pltpu.VMEM((tr, D_MODEL), BF16),
                        pltpu.VMEM((n_slab, SUBLANES + n_seq * (seq_len + SUBLANES), LANES), F32),
                        pltpu.VMEM((n_slab, SUBLANES + n_seq * (seq_len + SUBLANES), LANES), F32),
                        pltpu.VMEM((n_slab, tr, LANES), F32)],
        compiler_params=_params(2),
        name="c_in",
    )(x2, mods, g_pre, w_main, w_dt, dt_b, conv_w, conv_b)


def _split3(a):
    hi = a.astype(BF16)
    r1 = a - hi.astype(F32)
    mid = r1.astype(BF16)
    lo = (r1 - mid.astype(F32)).astype(BF16)
    return hi, mid, lo


def _ssd_direction(d, xs_ref, b_ref, c_ref, dt_ref, aneg_ref, st_scr, xw_scr, xt_scr, xtb_scr, ybuf_scr, slot):
    q = SSD_CHUNK
    for p in range(SSD_INNER // LANES):
        xp_t = xs_ref[p].astype(F32).T
        xt_scr[d, p * LANES:(p + 1) * LANES, :] = xp_t
        xtb_scr[d, p * LANES:(p + 1) * LANES, :] = xp_t.astype(BF16)

    dt = dt_ref[...]
    a = dt * (aneg_ref[d] * LOG2E)
    ri = lax.broadcasted_iota(jnp.int32, (q, q), 0)
    ci = lax.broadcasted_iota(jnp.int32, (q, q), 1)
    keep_t = (ri <= ci) if d == 0 else (ri >= ci)
    keep_n = (ci <= ri) if d == 0 else (ci >= ri)
    pieces = jnp.concatenate(_split3(a), axis=1)
    c3 = _dot(jnp.where(keep_n, 1.0, 0.0).astype(BF16), pieces)
    cum = c3[:, 0:LANES] + c3[:, LANES:2 * LANES] + c3[:, 2 * LANES:3 * LANES]
    cum_t = cum.T
    dt_t = dt.T
    total_t = jnp.sum(a.T, axis=1, keepdims=True)
    ecum_t = jnp.exp2(cum_t)
    wgt_t = jnp.exp2(total_t - cum_t) * dt_t
    cdec = jnp.exp2(total_t)
    neg_inf = jnp.float32(-jnp.inf)

    def row_bf16(v, e, n):
        return jnp.broadcast_to(v[e:e + 1, :], (n, q)).astype(BF16)

    hpg = SSD_HEADS // SSD_GROUPS
    gw = hpg * SSD_HEAD_DIM
    for g in range(SSD_GROUPS):
        grows = slice(g * gw, (g + 1) * gw)
        bg = b_ref[g].astype(BF16)
        cg = c_ref[g]
        cb_t = lax.dot_general(bg, cg.astype(BF16), NT, preferred_element_type=F32)
        cg_t = cg.astype(F32).T.astype(BF16)
        stg = st_scr[d, grows, :]
        stg_b = stg.astype(BF16)
        decs = []
        for eh in range(hpg):
            e = g * hpg + eh
            rows = slice(e * SSD_HEAD_DIM, (e + 1) * SSD_HEAD_DIM)
            row = jnp.broadcast_to(cum_t[e:e + 1, :], (q, q))
            seg = row - row.T
            m_t = (cb_t * jnp.exp2(jnp.where(keep_t, seg, neg_inf))).astype(BF16)
            xe = xtb_scr[d, rows, :]
            lhs = jnp.concatenate([xe * row_bf16(dt_t, e, SSD_HEAD_DIM),
                                   stg_b[eh * SSD_HEAD_DIM:(eh + 1) * SSD_HEAD_DIM, :]], axis=1)
            rhs = jnp.concatenate([m_t, cg_t * row_bf16(ecum_t, e, SSD_STATE)], axis=0)
            ybuf_scr[slot, rows, :] = _dot(lhs, rhs)
            xw_scr[d, rows, :] = xe * row_bf16(wgt_t, e, SSD_HEAD_DIM)
            decs.append(jnp.broadcast_to(cdec[e:e + 1, :], (SSD_HEAD_DIM, SSD_STATE)))
        new = _dot(xw_scr[d, grows, :], bg)
        st_scr[d, grows, :] = stg * jnp.concatenate(decs, axis=0) + new


def _ssd_finish(c, first_slot, last_slot, xt_ref, z_ref, x_ref, dskip_ref, gnorm_ref, wout_ref, gate, gfin_ref,
                ybuf_scr, y_ref):
    q = SSD_CHUNK
    acc = None
    ss = jnp.zeros((1, q), F32)
    slabs_per = FIN_CHUNK // LANES
    for k in range(SSD_INNER // FIN_CHUNK):
        rows = slice(k * FIN_CHUNK, (k + 1) * FIN_CHUNK)
        ytot = ybuf_scr[first_slot, rows, :] + ybuf_scr[last_slot, rows, :] + dskip_ref[rows, :] * xt_ref[rows, :]
        zt = jnp.concatenate([z_ref[p].astype(F32).T for p in range(k * slabs_per, (k + 1) * slabs_per)], axis=0)
        gated = ytot * _silu(zt)
        ss = ss + jnp.sum(gated * gated, axis=0, keepdims=True)
        part = _dot((gated * gnorm_ref[rows, :]).T.astype(BF16), wout_ref[rows, :])
        acc = part if acc is None else acc + part
    inv = lax.rsqrt(ss * (1.0 / SSD_INNER) + EPS)
    inv_col = jnp.broadcast_to(inv, (q, q)).T
    scale = jnp.concatenate([inv_col] * (D_MODEL // LANES), axis=1)
    xn = x_ref[...] + gate * (acc * scale)
    y_ref[pl.ds(pl.multiple_of(c * q, q), q), :] = _rms(xn, gfin_ref[...])


def _ssd_kernel(*refs, n_chunks, has_init, emit_state):
    refs = list(refs)
    xs_refs, b_refs, c_refs, dt_refs, z_refs, x_refs = (refs[0:2], refs[2:4], refs[4:6], refs[6:8], refs[8:10],
                                                        refs[10:12])
    refs = refs[12:]
    st0_ref = refs.pop(0) if has_init else None
    aneg_ref, dskip_ref, gnorm_ref, wout_ref, mod_ref, gfin_ref, y_ref = refs[:7]
    refs = refs[7:]
    stout_ref = refs.pop(0) if emit_state else None
    st_scr, ybuf_scr, xw_scr, xt_scr, xtb_scr = refs

    t = pl.program_id(1)
    second = t >= n_chunks // 2
    chunks = (t, n_chunks - 1 - t)

    @pl.when(t == 0)
    def _():
        if has_init:
            st_scr[...] = st0_ref[0]
        else:
            st_scr[...] = jnp.zeros(st_scr.shape, F32)

    for d in range(2):
        slot = jnp.where(second, n_chunks + d, chunks[d])
        _ssd_direction(d, xs_refs[d], b_refs[d], c_refs[d], dt_refs[d], aneg_ref, st_scr, xw_scr, xt_scr, xtb_scr,
                       ybuf_scr, slot)

    if emit_state:
        @pl.when(t == n_chunks - 1)
        def _():
            stout_ref[0] = st_scr[...]

    @pl.when(second)
    def _():
        for d in range(2):
            _ssd_finish(chunks[d], chunks[d], n_chunks + d, xt_scr.at[d], z_refs[d], x_refs[d], dskip_ref,
                        gnorm_ref, wout_ref, mod_ref[0, 2:3, :], gfin_ref, ybuf_scr, y_ref)


def _ssd_call(u3, dt2, x2, seq_len, st0, a_neg, d_skip, g_norm, w_out, mods, mod_row, g_final, emit_state):
    rows = x2.shape[0]
    q = SSD_CHUNK
    nc = seq_len // q
    half = nc // 2
    nb = rows // seq_len
    has_init = st0 is not None
    n_slab = SSD_INNER // LANES
    bslab = 2 * n_slab // SSD_GROUPS

    fwd = lambda b, t: b * nc + t
    bwd = lambda b, t: b * nc + nc - 1 - t
    fwd_fin = lambda b, t: b * nc + jnp.maximum(t, half)
    bwd_fin = lambda b, t: b * nc + jnp.minimum(nc - 1 - t, half - 1)
    mod_map = (lambda b, t: (b, 0, 0)) if mod_row is None else (lambda b, t: (mod_row, 0, 0))

    def pair(block, make_map):
        return [pl.BlockSpec(block, make_map(fwd)), pl.BlockSpec(block, make_map(bwd))]

    in_specs = (pair((n_slab, q, LANES), lambda ch: (lambda b, t: (1, ch(b, t), 0)))
                + pair((SSD_GROUPS, q, LANES), lambda ch: (lambda b, t: (bslab, ch(b, t), 0)))
                + pair((SSD_GROUPS, q, LANES), lambda ch: (lambda b, t: (bslab + 1, ch(b, t), 0)))
                + [pl.BlockSpec((q, HEAD_SLAB), lambda b, t: (fwd(b, t), 0)),
                   pl.BlockSpec((q, HEAD_SLAB), lambda b, t: (bwd(b, t), 1)),
                   pl.BlockSpec((n_slab, q, LANES), lambda b, t: (0, fwd_fin(b, t), 0)),
                   pl.BlockSpec((n_slab, q, LANES), lambda b, t: (0, bwd_fin(b, t), 0)),
                   pl.BlockSpec((q, D_MODEL), lambda b, t: (fwd_fin(b, t), 0)),
                   pl.BlockSpec((q, D_MODEL), lambda b, t: (bwd_fin(b, t), 0))])
    args = [u3] * 6 + [dt2, dt2, u3, u3, x2, x2]
    if has_init:
        in_specs.append(pl.BlockSpec((1, 2, SSD_INNER, SSD_STATE), lambda b, t: (b, 0, 0, 0)))
        args.append(st0)
    in_specs += [_full(a_neg.shape), _full(d_skip.shape), _full(g_norm.shape), _full(w_out.shape),
                 pl.BlockSpec((1, 3, D_MODEL), mod_map), _full(g_final.shape)]
    args += [a_neg, d_skip, g_norm, w_out, mods, g_final]
    out_specs = [pl.BlockSpec((seq_len, D_MODEL), lambda b, t: (b, 0))]
    out_shape = [jax.ShapeDtypeStruct((rows, D_MODEL), F32)]
    if emit_state:
        out_specs.append(pl.BlockSpec((1, 2, SSD_INNER, SSD_STATE), lambda b, t: (b, 0, 0, 0)))
        out_shape.append(jax.ShapeDtypeStruct((nb, 2, SSD_INNER, SSD_STATE), F32))
    return pl.pallas_call(
        functools.partial(_ssd_kernel, n_chunks=nc, has_init=has_init, emit_state=emit_state),
        grid=(nb, nc),
        in_specs=in_specs, out_specs=out_specs, out_shape=out_shape,
        scratch_shapes=[pltpu.VMEM((2, SSD_INNER, SSD_STATE), F32),
                        pltpu.VMEM((nc + 2, SSD_INNER, q), F32),
                        pltpu.VMEM((2, SSD_INNER, q), BF16),
                        pltpu.VMEM((2, SSD_INNER, q), F32),
                        pltpu.VMEM((2, SSD_INNER, q), BF16)],
        compiler_params=_params(2),
        name="ssd_init" if has_init else "ssd",
    )(*args)


def _rope_tables(length):
    rows = length // GRID_W
    row = jnp.repeat(jnp.arange(rows, dtype=F32), GRID_W)
    col = jnp.tile(jnp.arange(GRID_W, dtype=F32), rows)
    n_freq = QK_ROPE // 4
    inv = jnp.power(ROPE_THETA, -jnp.arange(n_freq, dtype=F32) / n_freq)
    ang = jnp.concatenate([row[:, None] * inv, col[:, None] * inv], axis=-1)
    cos, sin = jnp.cos(ang), jnp.sin(ang)
    cos2 = jnp.repeat(cos, 2, axis=-1)
    sin2 = jnp.stack([-sin, sin], axis=-1).reshape(length, QK_ROPE)
    pad = HEAD_SLAB - QK_ROPE
    return (jnp.concatenate([cos2, jnp.ones((length, pad), F32)], axis=-1),
            jnp.concatenate([sin2, jnp.zeros((length, pad), F32)], axis=-1))


def _prep_layer_a(w_in, w_uq, w_uk, w_uv):
    d = w_in.shape[0]
    swap = jnp.arange(QK_ROPE) ^ 1
    w_in = w_in.astype(BF16)
    kr = w_in[:, 384:416]
    z96 = jnp.zeros((d, HEAD_SLAB - QK_ROPE), BF16)
    w_r = jnp.concatenate([w_in[:, 0:384], kr, z96, kr[:, swap], z96, w_in[:, 416:]], axis=1)
    uq = w_uq.reshape(Q_RANK, MLA_HEADS, QK_NOPE + QK_ROPE)
    nope, rp = uq[..., :QK_NOPE], uq[..., QK_NOPE:]
    z32 = jnp.zeros((Q_RANK, MLA_HEADS, 32), F32)
    w1 = jnp.concatenate([rp, z32, nope], axis=-1).reshape(Q_RANK, MLA_HEADS * HEAD_SLAB).astype(BF16)
    w2 = jnp.concatenate([rp[..., swap], z32, jnp.zeros_like(nope)], axis=-1)
    w2 = w2.reshape(Q_RANK, MLA_HEADS * HEAD_SLAB).astype(BF16)
    uk = w_uk.reshape(KV_RANK, MLA_HEADS, QK_NOPE)
    top = jnp.concatenate([jnp.zeros((KV_RANK, MLA_HEADS, 64), F32), uk], axis=-1)
    eye = jnp.broadcast_to(jnp.eye(QK_ROPE, dtype=F32)[:, None, :], (QK_ROPE, MLA_HEADS, QK_ROPE))
    mid = jnp.concatenate([eye, jnp.zeros((QK_ROPE, MLA_HEADS, HEAD_SLAB - QK_ROPE), F32)], axis=-1)
    bot = jnp.zeros((KV_RANK - QK_ROPE, MLA_HEADS, HEAD_SLAB), F32)
    w_k = jnp.concatenate([top, mid, bot], axis=0).reshape(2 * KV_RANK, MLA_HEADS * HEAD_SLAB).astype(BF16)
    return w_r, w1, w2, w_k, w_uv.T.astype(BF16)


def kernel(x_prompt, x_sample, cache_ckv, cache_krope, state_ssd, c, c_ctx, w_mod, b_mod, g_pre, g_final,
           a_w_in, a_g_q, a_g_kv, a_w_uq, a_w_uk, a_w_uv, a_conv_w, a_conv_b, a_ln_g, a_ln_b, a_w_out,
           c_w_in, c_conv_w, c_conv_b, c_dt_bias, c_a_log, c_d, c_g_norm, c_w_out):
    nbp, lp, d = x_prompt.shape
    nbs, ls, _ = x_sample.shape
    ctx_row = nbs
    cond = jnp.concatenate([c, c_ctx[None, :], jnp.zeros((16 - nbs - 1, d), F32)], axis=0)
    mods_all = _mod_call(cond, w_mod, b_mod).reshape(DEPTH, 16, 3, d)
    row2 = lambda v: v.reshape(1, -1)

    mods = mods_all[0]
    w_r, w1, w2, w_k, w_vt = _prep_layer_a(a_w_in[0], a_w_uq[0], a_w_uk[0], a_w_uv[0])
    cos_t, sin_t = _rope_tables(ls)
    gpre = row2(g_pre[0])
    gq, gkv = row2(a_g_q[0]), row2(a_g_kv[0])
    a_wout = a_w_out[0].astype(BF16)
    conv_args = (a_conv_w[0], row2(a_conv_b[0]), row2(a_ln_g[0]), row2(a_ln_b[0]), a_wout)

    qp, kxp, cvp, sgp, ckv_p, kr_p = _a_in_call(x_prompt.reshape(nbp * lp, d), mods, ctx_row, lp, gpre,
                                                w_r, gq, gkv, w1, None, None, None)
    attn_p = _attn_call(qp.reshape(nbp, lp, -1), kxp.reshape(nbp, lp, -1), None, w_k, w_vt)
    xp1 = _a_out_call(attn_p, cvp.reshape(nbp, lp, -1), sgp.reshape(nbp, lp, -1), x_prompt, mods, ctx_row,
                      *conv_args)

    qs, kxs, cvs, sgs = _a_in_call(x_sample.reshape(nbs * ls, d), mods, None, ls, gpre,
                                   w_r, gq, gkv, w1, w2, cos_t, sin_t)
    n_ctx = cache_ckv.shape[2]
    kx_ctx = jnp.concatenate([cache_ckv[:, 0], cache_krope[:, 0],
                              jnp.zeros((nbs, n_ctx, KV_RANK - QK_ROPE), F32)], axis=-1).astype(BF16)
    attn_s = _attn_call(qs.reshape(nbs, ls, -1), kxs.reshape(nbs, ls, -1), kx_ctx, w_k, w_vt)
    xs1 = _a_out_call(attn_s, cvs.reshape(nbs, ls, -1), sgs.reshape(nbs, ls, -1), x_sample, mods, None,
                      *conv_args)

    mods = mods_all[1]
    gpre = row2(g_pre[1])
    w_in = c_w_in[0]
    w_main = w_in.astype(BF16)
    dtw = w_in[:, C_MAIN:]
    zpad = jnp.zeros((d, HEAD_SLAB - SSD_HEADS), F32)
    w_dt = jnp.concatenate([dtw[:, 0:SSD_HEADS], zpad, dtw[:, SSD_HEADS:], zpad], axis=1).astype(BF16)
    pad32 = jnp.zeros((2, HEAD_SLAB - SSD_HEADS), F32)
    dt_b = jnp.concatenate([c_dt_bias[0], pad32], axis=1).reshape(1, 2 * HEAD_SLAB)
    a_neg = jnp.concatenate([-jnp.exp(c_a_log[0]), pad32], axis=1).reshape(2, 1, HEAD_SLAB)
    lane_rep = lambda v: jnp.broadcast_to(v.reshape(-1, 1), (v.size, LANES))
    d_skip = lane_rep(jnp.repeat(c_d[0], SSD_HEAD_DIM))
    gnorm = lane_rep(c_g_norm[0])
    c_wout = c_w_out[0].astype(BF16)
    gfin = row2(g_final)
    cw, cbias = c_conv_w[0], row2(c_conv_b[0])

    xp1f = xp1.reshape(nbp * lp, d)
    up, dtp = _c_in_call(xp1f, mods, ctx_row, lp, gpre, w_main, w_dt, dt_b, cw, cbias)
    y_prompt, st_p = _ssd_call(up, dtp, xp1f, lp, None, a_neg, d_skip, gnorm, c_wout, mods, ctx_row, gfin, True)
    xs1f = xs1.reshape(nbs * ls, d)
    us, dts = _c_in_call(xs1f, mods, None, ls, gpre, w_main, w_dt, dt_b, cw, cbias)
    st0 = state_ssd[:, 0].reshape(nbs, 2, SSD_INNER, SSD_STATE)
    (y_sample,) = _ssd_call(us, dts, xs1f, ls, st0, a_neg, d_skip, gnorm, c_wout, mods, None, gfin, False)

    new_cache_ckv = ckv_p.reshape(nbp, 1, lp, KV_RANK)
    new_cache_krope = kr_p.reshape(nbp, 1, lp, QK_ROPE)
    new_state_ssd = st_p.reshape(nbp, 1, 2, SSD_HEADS, SSD_HEAD_DIM, SSD_STATE)
    return (y_prompt.reshape(nbp, lp, d), y_sample.reshape(nbs, ls, d), new_cache_ckv, new_cache_krope,
            new_state_ssd)
```

```python
import functools
import math

import jax
import jax.numpy as jnp
from jax import lax
from jax.experimental import pallas as pl
from jax.experimental.pallas import tpu as pltpu

F32 = jnp.float32
BF16 = jnp.bfloat16
ACT = BF16

LANES = 128
SUBLANES = 8

D_MODEL = 1024
DEPTH = 2
GRID_W = 64
EPS = 1e-6

MLA_HEADS = 8
Q_RANK = 256
KV_RANK = 128
QK_NOPE = 64
QK_ROPE = 32
V_HEAD = 64
MLA_WIDTH = MLA_HEADS * V_HEAD
ATTN_SCALE = (QK_NOPE + QK_ROPE) ** -0.5
ROPE_THETA = 10000.0
HEAD_SLAB = LANES

CONV_WIDTH = 512
CONV_K = 31
A_GLU0 = 640
A_GATE0 = A_GLU0 + 2 * CONV_WIDTH
A_COLS = A_GATE0 + MLA_WIDTH + CONV_WIDTH

SSD_INNER = 2048
SSD_HEAD_DIM = 64
SSD_HEADS = 32
SSD_GROUPS = 4
SSD_STATE = 128
SSD_CONV_K = 5
SSD_CHUNK = 128
FIN_CHUNK = 512
SSD_CONV_CH = SSD_INNER + 2 * SSD_GROUPS * SSD_STATE
C_MAIN = SSD_INNER + SSD_CONV_CH
C_TILE = 1024
ROW_TILE = 1024
MM_SLABS = 2
A_ROW_TILE = 1024
A_OUT_ROWS = 256

CONV_STRIDE = 4
CONV_BLOCK = SUBLANES * CONV_STRIDE

VMEM_LIMIT = 56 * 1024 * 1024
NT = (((1,), (1,)), ((), ()))
LOG2E = math.log2(math.e)


def _dot(a, b):
    return jnp.dot(a, b, preferred_element_type=F32)


def _sigmoid(x):
    return 1.0 / (1.0 + jnp.exp(-x))


def _silu(x):
    return x * _sigmoid(x)


def _rms(x, g):
    return x * lax.rsqrt(jnp.mean(x * x, axis=-1, keepdims=True) + EPS) * g


def _params(n_axes):
    return pltpu.CompilerParams(dimension_semantics=("arbitrary",) * n_axes,
                                vmem_limit_bytes=VMEM_LIMIT)


def _full(shape):
    nd = len(shape)
    return pl.BlockSpec(shape, lambda *_: (0,) * nd, pipeline_mode=pl.Buffered(1))


def _mod_kernel(cond_ref, w_ref, b_ref, o_ref):
    s = _silu(cond_ref[...])
    o_ref[0] = _dot(s.astype(BF16), w_ref[0].astype(BF16)) + b_ref[0]


def _mod_call(cond, w_mod, b_mod):
    nrow = cond.shape[0]
    tn = 1024
    return pl.pallas_call(
        _mod_kernel,
        grid=(DEPTH, 3 * D_MODEL // tn),
        in_specs=[pl.BlockSpec((nrow, D_MODEL), lambda l, j: (0, 0)),
                  pl.BlockSpec((1, D_MODEL, tn), lambda l, j: (l, 0, j)),
                  pl.BlockSpec((1, 1, tn), lambda l, j: (l, 0, j))],
        out_specs=pl.BlockSpec((1, nrow, tn), lambda l, j: (l, 0, j)),
        out_shape=jax.ShapeDtypeStruct((DEPTH, nrow, 3 * D_MODEL), F32),
        compiler_params=_params(2),
        name="modulation",
    )(cond, w_mod, b_mod.reshape(DEPTH, 1, 3 * D_MODEL))


def _a_in_kernel(*refs, rope):
    if rope:
        (x_ref, mod_ref, gpre_ref, w_ref, gq_ref, gkv_ref, w1_ref, w2_ref, cos_ref, sin_ref,
         q_ref, kx_ref, cv_ref, sg_ref) = refs
    else:
        (x_ref, mod_ref, gpre_ref, w_ref, gq_ref, gkv_ref, w1_ref,
         q_ref, kx_ref, cv_ref, sg_ref, ckv_ref, kr_ref) = refs
    x = x_ref[...]
    h = _rms(x, gpre_ref[...]) * (1.0 + mod_ref[0, 1:2, :]) + mod_ref[0, 0:1, :]
    hb = h.astype(BF16)

    u0 = _dot(hb, w_ref[:, 0:A_GLU0])
    qn = _rms(u0[:, 0:Q_RANK], gq_ref[...]).astype(BF16)
    ckv = _rms(u0[:, Q_RANK:Q_RANK + KV_RANK], gkv_ref[...])
    kr = u0[:, 384:512]
    qf = _dot(qn, w1_ref[...])
    qscale = ATTN_SCALE * LOG2E
    if rope:
        cos = cos_ref[...]
        sin = sin_ref[...]
        qs = _dot(qn, w2_ref[...])
        for hd in range(MLA_HEADS):
            sl = slice(hd * HEAD_SLAB, (hd + 1) * HEAD_SLAB)
            q_ref[:, sl] = ((qf[:, sl] * cos + qs[:, sl] * sin) * qscale).astype(BF16)
        kr = kr * cos + u0[:, 512:640] * sin
    else:
        q_ref[...] = (qf * qscale).astype(BF16)
        ckv_ref[...] = ckv
        kr_ref[...] = kr[:, 0:QK_ROPE]
    kx_ref[:, 0:KV_RANK] = ckv.astype(BF16)
    kx_ref[:, KV_RANK:2 * KV_RANK] = kr.astype(BF16)

    glu = _dot(hb, w_ref[:, A_GLU0:A_GATE0])
    cv_ref[...] = (glu[:, 0:CONV_WIDTH] * _sigmoid(glu[:, CONV_WIDTH:])).astype(ACT)
    sg_ref[...] = _silu(_dot(hb, w_ref[:, A_GATE0:A_COLS])).astype(ACT)


def _a_in_call(x2, mods, mod_row, seq_len, g_pre, w_in, g_q, g_kv, w1, w2, cos_t, sin_t):
    rows = x2.shape[0]
    tr = A_ROW_TILE
    rope = w2 is not None
    per_seq = max(seq_len // tr, 1)
    if mod_row is None:
        mod_map = lambda i: (i // per_seq, 0, 0)
    else:
        mod_map = lambda i: (mod_row, 0, 0)
    row = lambda i: (i, 0)
    in_specs = [pl.BlockSpec((tr, D_MODEL), row),
                pl.BlockSpec((1, 3, D_MODEL), mod_map),
                _full(g_pre.shape), _full(w_in.shape), _full(g_q.shape), _full(g_kv.shape),
                _full(w1.shape)]
    args = [x2, mods, g_pre, w_in, g_q, g_kv, w1]
    out_specs = [pl.BlockSpec((tr, MLA_HEADS * HEAD_SLAB), row),
                 pl.BlockSpec((tr, 2 * KV_RANK), row),
                 pl.BlockSpec((tr, CONV_WIDTH), row),
                 pl.BlockSpec((tr, MLA_WIDTH + CONV_WIDTH), row)]
    out_shape = [jax.ShapeDtypeStruct((rows, MLA_HEADS * HEAD_SLAB), BF16),
                 jax.ShapeDtypeStruct((rows, 2 * KV_RANK), BF16),
                 jax.ShapeDtypeStruct((rows, CONV_WIDTH), ACT),
                 jax.ShapeDtypeStruct((rows, MLA_WIDTH + CONV_WIDTH), ACT)]
    if rope:
        in_specs += [_full(w2.shape),
                     pl.BlockSpec((tr, HEAD_SLAB), lambda i: (i % per_seq, 0)),
                     pl.BlockSpec((tr, HEAD_SLAB), lambda i: (i % per_seq, 0))]
        args += [w2, cos_t, sin_t]
    else:
        out_specs += [pl.BlockSpec((tr, KV_RANK), row), pl.BlockSpec((tr, QK_ROPE), row)]
        out_shape += [jax.ShapeDtypeStruct((rows, KV_RANK), F32),
                      jax.ShapeDtypeStruct((rows, QK_ROPE), F32)]
    return pl.pallas_call(
        functools.partial(_a_in_kernel, rope=rope),
        grid=(rows // tr,),
        in_specs=in_specs, out_specs=out_specs, out_shape=out_shape,
        compiler_params=_params(1),
        name="a_in_rope" if rope else "a_in",
    )(*args)


def _attn_kernel(*refs, n_ctx, seq_len):
    if n_ctx:
        q_ref, kxn_ref, kxc_ref, wk_ref, wvt_ref, o_ref, k_scr, vt_scr, ot_scr, s_scr = refs
    else:
        q_ref, kxn_ref, wk_ref, wvt_ref, o_ref, k_scr, vt_scr, ot_scr, s_scr = refs

    @pl.when(pl.program_id(1) == 0)
    def _():
        def fill(kx, lo, n):
            k_scr[lo:lo + n, :] = _dot(kx, wk_ref[...]).astype(BF16)
            vt_scr[:, lo:lo + n] = lax.dot_general(
                wvt_ref[...], kx[:, 0:KV_RANK], NT, preferred_element_type=F32).astype(BF16)
        if n_ctx:
            fill(kxc_ref[0], 0, n_ctx)
        fill(kxn_ref[0], n_ctx, seq_len)

    def scores(hd):
        sl = slice(hd * HEAD_SLAB, (hd + 1) * HEAD_SLAB)
        s_scr[hd % 2] = lax.dot_general(k_scr[:, sl], q_ref[0, :, sl], NT, preferred_element_type=F32)

    scores(0)
    for hd in range(MLA_HEADS):
        vrows = slice(hd * V_HEAD, (hd + 1) * V_HEAD)
        if hd + 1 < MLA_HEADS:
            scores(hd + 1)
        st = s_scr[hd % 2]
        p = jnp.exp2(st - jnp.max(st, axis=0, keepdims=True))
        den = jnp.sum(p, axis=0, keepdims=True)
        ot_scr[vrows, :] = _dot(vt_scr[vrows, :], p.astype(BF16)) / den
    o_ref[0] = ot_scr[...].T.astype(ACT)


def _attn_call(q3, kx_new, kx_ctx, w_k, w_vt):
    nb, seq_len, _ = q3.shape
    n_ctx = 0 if kx_ctx is None else kx_ctx.shape[1]
    tq = min(seq_len, 1024)
    lk = n_ctx + seq_len
    in_specs = [pl.BlockSpec((1, tq, MLA_HEADS * HEAD_SLAB), lambda b, i: (b, i, 0)),
                pl.BlockSpec((1, seq_len, 2 * KV_RANK), lambda b, i: (b, 0, 0))]
    args = [q3, kx_new]
    if n_ctx:
        in_specs.append(pl.BlockSpec((1, n_ctx, 2 * KV_RANK), lambda b, i: (b, 0, 0)))
        args.append(kx_ctx)
    in_specs += [_full(w_k.shape), _full(w_vt.shape)]
    args += [w_k, w_vt]
    return pl.pallas_call(
        functools.partial(_attn_kernel, n_ctx=n_ctx, seq_len=seq_len),
        grid=(nb, seq_len // tq),
        in_specs=in_specs,
        out_specs=pl.BlockSpec((1, tq, MLA_WIDTH), lambda b, i: (b, i, 0)),
        out_shape=jax.ShapeDtypeStruct((nb, seq_len, MLA_WIDTH), ACT),
        scratch_shapes=[pltpu.VMEM((lk, MLA_HEADS * HEAD_SLAB), BF16),
                        pltpu.VMEM((MLA_WIDTH, lk), BF16),
                        pltpu.VMEM((MLA_WIDTH, tq), F32),
                        pltpu.VMEM((2, lk, tq), F32)],
        compiler_params=_params(2),
        name="attn_ctx" if n_ctx else "attn",
    )(*args)


def _a_out_kernel(attn_ref, cv_ref, sg_ref, x_ref, mod_ref, cw_ref, cb_ref, lng_ref, lnb_ref, wout_ref,
                  o_ref, pad_scr, cvn_scr, *, seq_len):
    halo = 16
    n_slab = CONV_WIDTH // LANES
    zeros = jnp.zeros((halo, LANES), F32)
    for sl in range(n_slab):
        pad_scr[sl, 0:halo, :] = zeros
        pad_scr[sl, halo + seq_len:2 * halo + seq_len, :] = zeros
        pad_scr[sl, halo:halo + seq_len, :] = cv_ref[0, :, sl * LANES:(sl + 1) * LANES].astype(F32)

    def block(r, carry):
        base = r * CONV_BLOCK
        for sl in range(n_slab):
            lanes = slice(sl * LANES, (sl + 1) * LANES)
            accs = [None] * CONV_STRIDE
            for k in range(CONV_K):
                w = jnp.broadcast_to(cw_ref[k:k + 1, lanes], (SUBLANES, LANES))
                for ph in range(CONV_STRIDE):
                    start = base + (halo - CONV_K // 2 + ph + k)
                    v = pad_scr[sl, pl.ds(start, SUBLANES, stride=CONV_STRIDE), :] * w
                    accs[ph] = v if accs[ph] is None else accs[ph] + v
            for ph in range(CONV_STRIDE):
                cvn_scr[sl, pl.ds(base + ph, SUBLANES, stride=CONV_STRIDE), :] = accs[ph]
        return carry

    lax.fori_loop(0, seq_len // CONV_BLOCK, block, 0)
    rc = min(seq_len, A_OUT_ROWS)
    for r0 in range(0, seq_len, rc):
        rs = slice(r0, r0 + rc)
        conv = jnp.concatenate([cvn_scr[sl, rs, :] for sl in range(n_slab)], axis=1) + cb_ref[...]
        mu = jnp.mean(conv, axis=-1, keepdims=True)
        cen = conv - mu
        ln = cen * lax.rsqrt(jnp.mean(cen * cen, axis=-1, keepdims=True) + EPS) * lng_ref[...] + lnb_ref[...]
        sg = sg_ref[0, rs, :].astype(F32)
        mix = jnp.concatenate([attn_ref[0, rs, :].astype(F32) * sg[:, 0:MLA_WIDTH], _silu(ln) * sg[:, MLA_WIDTH:]],
                              axis=1)
        out = _dot(mix.astype(BF16), wout_ref[...])
        o_ref[0, rs, :] = x_ref[0, rs, :] + mod_ref[0, 2:3, :] * out


def _a_out_call(attn3, cv3, sg3, x3, mods, mod_row, conv_w, conv_b, ln_g, ln_b, w_out):
    nb, seq_len, _ = x3.shape
    blk = lambda c: pl.BlockSpec((1, seq_len, c), lambda b: (b, 0, 0))
    mod_map = (lambda b: (b, 0, 0)) if mod_row is None else (lambda b: (mod_row, 0, 0))
    n_slab = CONV_WIDTH // LANES
    return pl.pallas_call(
        functools.partial(_a_out_kernel, seq_len=seq_len),
        grid=(nb,),
        in_specs=[blk(MLA_WIDTH), blk(CONV_WIDTH), blk(MLA_WIDTH + CONV_WIDTH), blk(D_MODEL),
                  pl.BlockSpec((1, 3, D_MODEL), mod_map),
                  _full(conv_w.shape), _full(conv_b.shape), _full(ln_g.shape), _full(ln_b.shape),
                  _full(w_out.shape)],
        out_specs=blk(D_MODEL),
        out_shape=jax.ShapeDtypeStruct((nb, seq_len, D_MODEL), F32),
        scratch_shapes=[pltpu.VMEM((n_slab, seq_len + 32, LANES), F32),
                        pltpu.VMEM((n_slab, seq_len, LANES), F32)],
        compiler_params=_params(1),
        name="a_out",
    )(attn3, cv3, sg3, x3, mods, conv_w, conv_b, ln_g, ln_b, w_out)


def _softplus(x):
    return jnp.maximum(x, 0.0) + jnp.log1p(jnp.exp(-jnp.abs(x)))


def _c_in_kernel(x_ref, mod_ref, gpre_ref, w_ref, wdt_ref, dtb_ref, cw_ref, cb_ref,
                 u_ref, dt_ref, h_scr, pad0_scr, pad1_scr, res_scr, *, seq_len, n_seq):
    j = pl.program_id(1)
    gap = SUBLANES
    pitch = seq_len + gap
    n_slab = C_TILE // LANES
    n_z = SSD_INNER // C_TILE
    n_tiles = C_MAIN // C_TILE
    pads = (pad0_scr, pad1_scr)

    @pl.when(j == 0)
    def _():
        h = _rms(x_ref[...], gpre_ref[...]) * (1.0 + mod_ref[0, 1:2, :]) + mod_ref[0, 0:1, :]
        h_scr[...] = h.astype(BF16)
        dt_ref[...] = _softplus(_dot(h_scr[...], wdt_ref[...]) + dtb_ref[...])

    def project(tile, slabs=range(C_TILE // LANES)):
        lo = tile * C_TILE + slabs[0] * LANES
        return _dot(h_scr[...], w_ref[:, lo:lo + len(slabs) * LANES])

    def stage(u, buf, slabs=range(C_TILE // LANES)):
        zeros = jnp.zeros((gap, LANES), F32)
        pad = pads[buf]
        for i, sl in enumerate(slabs):
            pad[sl, 0:gap, :] = zeros
            for s in range(n_seq):
                lo = gap + s * pitch
                pad[sl, lo:lo + seq_len, :] = u[s * seq_len:(s + 1) * seq_len, i * LANES:(i + 1) * LANES]
                pad[sl, lo + seq_len:lo + pitch, :] = zeros

    def conv(buf, slabs=range(C_TILE // LANES)):
        pad = pads[buf]
        for sl in slabs:
            lanes = slice(sl * LANES, (sl + 1) * LANES)
            ws = [jnp.broadcast_to(cw_ref[k:k + 1, lanes], (SUBLANES, LANES)) for k in range(SSD_CONV_K)]
            bias = jnp.broadcast_to(cb_ref[:, lanes], (SUBLANES, LANES))
            for s in range(n_seq):
                for r in range(seq_len // CONV_BLOCK):
                    for ph in range(CONV_STRIDE):
                        acc = bias
                        for k in range(SSD_CONV_K):
                            start = r * CONV_BLOCK + gap + s * pitch - SSD_CONV_K // 2 + ph + k
                            acc = acc + pad[sl, pl.ds(start, SUBLANES, stride=CONV_STRIDE), :] * ws[k]
                        row0 = r * CONV_BLOCK + s * seq_len + ph
                        res_scr[sl, pl.ds(row0, SUBLANES, stride=CONV_STRIDE), :] = _silu(acc)
            u_ref[sl] = res_scr[sl].astype(ACT)

    for tile in range(n_z):
        @pl.when(j == tile)
        def _(tile=tile):
            u = project(tile)
            for sl in range(n_slab):
                u_ref[sl] = u[:, sl * LANES:(sl + 1) * LANES].astype(ACT)

    @pl.when(j == n_z)
    def _():
        stage(project(n_z), 0)

    for tile in range(n_z + 1, n_tiles):
        @pl.when(j == tile)
        def _(tile=tile):
            buf = (tile - n_z) % 2
            for part in range(n_slab // MM_SLABS):
                slabs = range(part * MM_SLABS, (part + 1) * MM_SLABS)
                stage(project(tile, slabs), buf, slabs)
                conv(1 - buf, slabs)

    @pl.when(j == n_tiles)
    def _():
        conv((n_tiles - 1 - n_z) % 2)


def _c_in_call(x2, mods, mod_row, seq_len, g_pre, w_main, w_dt, dt_b, conv_w, conv_b):
    rows = x2.shape[0]
    tr = ROW_TILE
    n_seq = tr // seq_len
    n_z = SSD_INNER // C_TILE
    n_slab = C_TILE // LANES
    n_tiles = C_MAIN // C_TILE
    mod_map = (lambda i, j: (i, 0, 0)) if mod_row is None else (lambda i, j: (mod_row, 0, 0))
    conv_map = lambda i, j: (0, jnp.clip(j - 1 - n_z, 0, n_tiles - 1 - n_z))
    out_map = lambda i, j: (j - (j >= n_z).astype(jnp.int32), i, 0)
    return pl.pallas_call(
        functools.partial(_c_in_kernel, seq_len=seq_len, n_seq=n_seq),
        grid=(rows // tr, n_tiles + 1),
        in_specs=[pl.BlockSpec((tr, D_MODEL), lambda i, j: (i, 0)),
                  pl.BlockSpec((1, 3, D_MODEL), mod_map),
                  _full(g_pre.shape),
                  pl.BlockSpec((D_MODEL, C_MAIN), lambda i, j: (0, 0), pipeline_mode=pl.Buffered(1)),
                  _full(w_dt.shape), _full(dt_b.shape),
                  pl.BlockSpec((SSD_CONV_K, C_TILE), conv_map),
                  pl.BlockSpec((1, C_TILE), conv_map)],
        out_specs=[pl.BlockSpec((n_slab, tr, LANES), out_map),
                   pl.BlockSpec((tr, 2 * HEAD_SLAB), lambda i, j: (i, 0))],
        out_shape=[jax.ShapeDtypeStruct((C_MAIN // LANES, rows, LANES), ACT),
                   jax.ShapeDtypeStruct((rows, 2 * HEAD_SLAB), F32)],
        scratch_shapes=[pltpu.VMEM((tr, D_MODEL), BF16),
                        pltpu.VMEM((n_slab, SUBLANES + n_seq * (seq_len + SUBLANES), LANES), F32),
                        pltpu.VMEM((n_slab, SUBLANES + n_seq * (seq_len + SUBLANES), LANES), F32),
                        pltpu.VMEM((n_slab, tr, LANES), F32)],
        compiler_params=_params(2),
        name="c_in",
    )(x2, mods, g_pre, w_main, w_dt, dt_b, conv_w, conv_b)


def _split3(a):
    hi = a.astype(BF16)
    r1 = a - hi.astype(F32)
    mid = r1.astype(BF16)
    lo = (r1 - mid.astype(F32)).astype(BF16)
    return hi, mid, lo


def _ssd_direction(d, xs_ref, b_ref, c_ref, dt_ref, aneg_ref, st_scr, xw_scr, xt_scr, xtb_scr, ybuf_scr, slot):
    q = SSD_CHUNK
    for p in range(SSD_INNER // LANES):
        xp_t = xs_ref[p].astype(F32).T
        xt_scr[d, p * LANES:(p + 1) * LANES, :] = xp_t
        xtb_scr[d, p * LANES:(p + 1) * LANES, :] = xp_t.astype(BF16)

    dt = dt_ref[...]
    a = dt * aneg_ref[d]
    ri = lax.broadcasted_iota(jnp.int32, (q, q), 0)
    ci = lax.broadcasted_iota(jnp.int32, (q, q), 1)
    keep_t = (ri <= ci) if d == 0 else (ri >= ci)
    keep_n = (ci <= ri) if d == 0 else (ci >= ri)
    pieces = jnp.concatenate(_split3(a), axis=1)
    c3 = _dot(jnp.where(keep_n, 1.0, 0.0).astype(BF16), pieces)
    cum = c3[:, 0:LANES] + c3[:, LANES:2 * LANES] + c3[:, 2 * LANES:3 * LANES]
    cum_t = cum.T
    dt_t = dt.T
    total_t = jnp.sum(a.T, axis=1, keepdims=True)
    ecum_t = jnp.exp(cum_t)
    wgt_t = jnp.exp(total_t - cum_t) * dt_t
    cdec = jnp.exp(total_t)
    neg_inf = jnp.float32(-jnp.inf)

    def row_bf16(v, e, n):
        return jnp.broadcast_to(v[e:e + 1, :], (n, q)).astype(BF16)

    hpg = SSD_HEADS // SSD_GROUPS
    gw = hpg * SSD_HEAD_DIM
    for g in range(SSD_GROUPS):
        grows = slice(g * gw, (g + 1) * gw)
        bg = b_ref[g].astype(BF16)
        cg = c_ref[g]
        cb_t = lax.dot_general(bg, cg.astype(BF16), NT, preferred_element_type=F32)
        cg_t = cg.astype(F32).T.astype(BF16)
        stg = st_scr[d, grows, :]
        stg_b = stg.astype(BF16)
        decs = []
        for eh in range(hpg):
            e = g * hpg + eh
            rows = slice(e * SSD_HEAD_DIM, (e + 1) * SSD_HEAD_DIM)
            row = jnp.broadcast_to(cum_t[e:e + 1, :], (q, q))
            seg = row - row.T
            m_t = (cb_t * jnp.exp(jnp.where(keep_t, seg, neg_inf))).astype(BF16)
            xe = xtb_scr[d, rows, :]
            lhs = jnp.concatenate([xe * row_bf16(dt_t, e, SSD_HEAD_DIM),
                                   stg_b[eh * SSD_HEAD_DIM:(eh + 1) * SSD_HEAD_DIM, :]], axis=1)
            rhs = jnp.concatenate([m_t, cg_t * row_bf16(ecum_t, e, SSD_STATE)], axis=0)
            ybuf_scr[slot, rows, :] = _dot(lhs, rhs)
            xw_scr[d, rows, :] = xe * row_bf16(wgt_t, e, SSD_HEAD_DIM)
            decs.append(jnp.broadcast_to(cdec[e:e + 1, :], (SSD_HEAD_DIM, SSD_STATE)))
        new = _dot(xw_scr[d, grows, :], bg)
        st_scr[d, grows, :] = stg * jnp.concatenate(decs, axis=0) + new


def _ssd_finish(c, first_slot, last_slot, xt_ref, z_ref, x_ref, dskip_ref, gnorm_ref, wout_ref, gate, gfin_ref,
                ybuf_scr, y_ref):
    q = SSD_CHUNK
    acc = None
    ss = jnp.zeros((1, q), F32)
    slabs_per = FIN_CHUNK // LANES
    for k in range(SSD_INNER // FIN_CHUNK):
        rows = slice(k * FIN_CHUNK, (k + 1) * FIN_CHUNK)
        ytot = ybuf_scr[first_slot, rows, :] + ybuf_scr[last_slot, rows, :] + dskip_ref[rows, :] * xt_ref[rows, :]
        zt = jnp.concatenate([z_ref[p].astype(F32).T for p in range(k * slabs_per, (k + 1) * slabs_per)], axis=0)
        gated = ytot * _silu(zt)
        ss = ss + jnp.sum(gated * gated, axis=0, keepdims=True)
        part = _dot((gated * gnorm_ref[rows, :]).T.astype(BF16), wout_ref[rows, :])
        acc = part if acc is None else acc + part
    inv = lax.rsqrt(ss * (1.0 / SSD_INNER) + EPS)
    inv_col = jnp.broadcast_to(inv, (q, q)).T
    scale = jnp.concatenate([inv_col] * (D_MODEL // LANES), axis=1)
    xn = x_ref[...] + gate * (acc * scale)
    y_ref[pl.ds(pl.multiple_of(c * q, q), q), :] = _rms(xn, gfin_ref[...])


def _ssd_kernel(*refs, n_chunks, has_init, emit_state):
    refs = list(refs)
    xs_refs, b_refs, c_refs, dt_refs, z_refs, x_refs = (refs[0:2], refs[2:4], refs[4:6], refs[6:8], refs[8:10],
                                                        refs[10:12])
    refs = refs[12:]
    st0_ref = refs.pop(0) if has_init else None
    aneg_ref, dskip_ref, gnorm_ref, wout_ref, mod_ref, gfin_ref, y_ref = refs[:7]
    refs = refs[7:]
    stout_ref = refs.pop(0) if emit_state else None
    st_scr, ybuf_scr, xw_scr, xt_scr, xtb_scr = refs

    t = pl.program_id(1)
    second = t >= n_chunks // 2
    chunks = (t, n_chunks - 1 - t)

    @pl.when(t == 0)
    def _():
        if has_init:
            st_scr[...] = st0_ref[0]
        else:
            st_scr[...] = jnp.zeros(st_scr.shape, F32)

    for d in range(2):
        slot = jnp.where(second, n_chunks + d, chunks[d])
        _ssd_direction(d, xs_refs[d], b_refs[d], c_refs[d], dt_refs[d], aneg_ref, st_scr, xw_scr, xt_scr, xtb_scr,
                       ybuf_scr, slot)

    if emit_state:
        @pl.when(t == n_chunks - 1)
        def _():
            stout_ref[0] = st_scr[...]

    @pl.when(second)
    def _():
        for d in range(2):
            _ssd_finish(chunks[d], chunks[d], n_chunks + d, xt_scr.at[d], z_refs[d], x_refs[d], dskip_ref,
                        gnorm_ref, wout_ref, mod_ref[0, 2:3, :], gfin_ref, ybuf_scr, y_ref)


def _ssd_call(u3, dt2, x2, seq_len, st0, a_neg, d_skip, g_norm, w_out, mods, mod_row, g_final, emit_state):
    rows = x2.shape[0]
    q = SSD_CHUNK
    nc = seq_len // q
    half = nc // 2
    nb = rows // seq_len
    has_init = st0 is not None
    n_slab = SSD_INNER // LANES
    bslab = 2 * n_slab // SSD_GROUPS

    fwd = lambda b, t: b * nc + t
    bwd = lambda b, t: b * nc + nc - 1 - t
    fwd_fin = lambda b, t: b * nc + jnp.maximum(t, half)
    bwd_fin = lambda b, t: b * nc + jnp.minimum(nc - 1 - t, half - 1)
    mod_map = (lambda b, t: (b, 0, 0)) if mod_row is None else (lambda b, t: (mod_row, 0, 0))

    def pair(block, make_map):
        return [pl.BlockSpec(block, make_map(fwd)), pl.BlockSpec(block, make_map(bwd))]

    in_specs = (pair((n_slab, q, LANES), lambda ch: (lambda b, t: (1, ch(b, t), 0)))
                + pair((SSD_GROUPS, q, LANES), lambda ch: (lambda b, t: (bslab, ch(b, t), 0)))
                + pair((SSD_GROUPS, q, LANES), lambda ch: (lambda b, t: (bslab + 1, ch(b, t), 0)))
                + [pl.BlockSpec((q, HEAD_SLAB), lambda b, t: (fwd(b, t), 0)),
                   pl.BlockSpec((q, HEAD_SLAB), lambda b, t: (bwd(b, t), 1)),
                   pl.BlockSpec((n_slab, q, LANES), lambda b, t: (0, fwd_fin(b, t), 0)),
                   pl.BlockSpec((n_slab, q, LANES), lambda b, t: (0, bwd_fin(b, t), 0)),
                   pl.BlockSpec((q, D_MODEL), lambda b, t: (fwd_fin(b, t), 0)),
                   pl.BlockSpec((q, D_MODEL), lambda b, t: (bwd_fin(b, t), 0))])
    args = [u3] * 6 + [dt2, dt2, u3, u3, x2, x2]
    if has_init:
        in_specs.append(pl.BlockSpec((1, 2, SSD_INNER, SSD_STATE), lambda b, t: (b, 0, 0, 0)))
        args.append(st0)
    in_specs += [_full(a_neg.shape), _full(d_skip.shape), _full(g_norm.shape), _full(w_out.shape),
                 pl.BlockSpec((1, 3, D_MODEL), mod_map), _full(g_final.shape)]
    args += [a_neg, d_skip, g_norm, w_out, mods, g_final]
    out_specs = [pl.BlockSpec((seq_len, D_MODEL), lambda b, t: (b, 0))]
    out_shape = [jax.ShapeDtypeStruct((rows, D_MODEL), F32)]
    if emit_state:
        out_specs.append(pl.BlockSpec((1, 2, SSD_INNER, SSD_STATE), lambda b, t: (b, 0, 0, 0)))
        out_shape.append(jax.ShapeDtypeStruct((nb, 2, SSD_INNER, SSD_STATE), F32))
    return pl.pallas_call(
        functools.partial(_ssd_kernel, n_chunks=nc, has_init=has_init, emit_state=emit_state),
        grid=(nb, nc),
        in_specs=in_specs, out_specs=out_specs, out_shape=out_shape,
        scratch_shapes=[pltpu.VMEM((2, SSD_INNER, SSD_STATE), F32),
                        pltpu.VMEM((nc + 2, SSD_INNER, q), F32),
                        pltpu.VMEM((2, SSD_INNER, q), BF16),
                        pltpu.VMEM((2, SSD_INNER, q), F32),
                        pltpu.VMEM((2, SSD_INNER, q), BF16)],
        compiler_params=_params(2),
        name="ssd_init" if has_init else "ssd",
    )(*args)


def _rope_tables(length):
    rows = length // GRID_W
    row = jnp.repeat(jnp.arange(rows, dtype=F32), GRID_W)
    col = jnp.tile(jnp.arange(GRID_W, dtype=F32), rows)
    n_freq = QK_ROPE // 4
    inv = jnp.power(ROPE_THETA, -jnp.arange(n_freq, dtype=F32) / n_freq)
    ang = jnp.concatenate([row[:, None] * inv, col[:, None] * inv], axis=-1)
    cos, sin = jnp.cos(ang), jnp.sin(ang)
    cos2 = jnp.repeat(cos, 2, axis=-1)
    sin2 = jnp.stack([-sin, sin], axis=-1).reshape(length, QK_ROPE)
    pad = HEAD_SLAB - QK_ROPE
    return (jnp.concatenate([cos2, jnp.ones((length, pad), F32)], axis=-1),
            jnp.concatenate([sin2, jnp.zeros((length, pad), F32)], axis=-1))


def _prep_layer_a(w_in, w_uq, w_uk, w_uv):
    d = w_in.shape[0]
    swap = jnp.arange(QK_ROPE) ^ 1
    w_in = w_in.astype(BF16)
    kr = w_in[:, 384:416]
    z96 = jnp.zeros((d, HEAD_SLAB - QK_ROPE), BF16)
    w_r = jnp.concatenate([w_in[:, 0:384], kr, z96, kr[:, swap], z96, w_in[:, 416:]], axis=1)
    uq = w_uq.reshape(Q_RANK, MLA_HEADS, QK_NOPE + QK_ROPE)
    nope, rp = uq[..., :QK_NOPE], uq[..., QK_NOPE:]
    z32 = jnp.zeros((Q_RANK, MLA_HEADS, 32), F32)
    w1 = jnp.concatenate([rp, z32, nope], axis=-1).reshape(Q_RANK, MLA_HEADS * HEAD_SLAB).astype(BF16)
    w2 = jnp.concatenate([rp[..., swap], z32, jnp.zeros_like(nope)], axis=-1)
    w2 = w2.reshape(Q_RANK, MLA_HEADS * HEAD_SLAB).astype(BF16)
    uk = w_uk.reshape(KV_RANK, MLA_HEADS, QK_NOPE)
    top = jnp.concatenate([jnp.zeros((KV_RANK, MLA_HEADS, 64), F32), uk], axis=-1)
    eye = jnp.broadcast_to(jnp.eye(QK_ROPE, dtype=F32)[:, None, :], (QK_ROPE, MLA_HEADS, QK_ROPE))
    mid = jnp.concatenate([eye, jnp.zeros((QK_ROPE, MLA_HEADS, HEAD_SLAB - QK_ROPE), F32)], axis=-1)
    bot = jnp.zeros((KV_RANK - QK_ROPE, MLA_HEADS, HEAD_SLAB), F32)
    w_k = jnp.concatenate([top, mid, bot], axis=0).reshape(2 * KV_RANK, MLA_HEADS * HEAD_SLAB).astype(BF16)
    return w_r, w1, w2, w_k, w_uv.T.astype(BF16)


def kernel(x_prompt, x_sample, cache_ckv, cache_krope, state_ssd, c, c_ctx, w_mod, b_mod, g_pre, g_final,
           a_w_in, a_g_q, a_g_kv, a_w_uq, a_w_uk, a_w_uv, a_conv_w, a_conv_b, a_ln_g, a_ln_b, a_w_out,
           c_w_in, c_conv_w, c_conv_b, c_dt_bias, c_a_log, c_d, c_g_norm, c_w_out):
    nbp, lp, d = x_prompt.shape
    nbs, ls, _ = x_sample.shape
    ctx_row = nbs
    cond = jnp.concatenate([c, c_ctx[None, :], jnp.zeros((16 - nbs - 1, d), F32)], axis=0)
    mods_all = _mod_call(cond, w_mod, b_mod).reshape(DEPTH, 16, 3, d)
    row2 = lambda v: v.reshape(1, -1)

    mods = mods_all[0]
    w_r, w1, w2, w_k, w_vt = _prep_layer_a(a_w_in[0], a_w_uq[0], a_w_uk[0], a_w_uv[0])
    cos_t, sin_t = _rope_tables(ls)
    gpre = row2(g_pre[0])
    gq, gkv = row2(a_g_q[0]), row2(a_g_kv[0])
    a_wout = a_w_out[0].astype(BF16)
    conv_args = (a_conv_w[0], row2(a_conv_b[0]), row2(a_ln_g[0]), row2(a_ln_b[0]), a_wout)

    qp, kxp, cvp, sgp, ckv_p, kr_p = _a_in_call(x_prompt.reshape(nbp * lp, d), mods, ctx_row, lp, gpre,
                                                w_r, gq, gkv, w1, None, None, None)
    attn_p = _attn_call(qp.reshape(nbp, lp, -1), kxp.reshape(nbp, lp, -1), None, w_k, w_vt)
    xp1 = _a_out_call(attn_p, cvp.reshape(nbp, lp, -1), sgp.reshape(nbp, lp, -1), x_prompt, mods, ctx_row,
                      *conv_args)

    qs, kxs, cvs, sgs = _a_in_call(x_sample.reshape(nbs * ls, d), mods, None, ls, gpre,
                                   w_r, gq, gkv, w1, w2, cos_t, sin_t)
    n_ctx = cache_ckv.shape[2]
    kx_ctx = jnp.concatenate([cache_ckv[:, 0], cache_krope[:, 0],
                              jnp.zeros((nbs, n_ctx, KV_RANK - QK_ROPE), F32)], axis=-1).astype(BF16)
    attn_s = _attn_call(qs.reshape(nbs, ls, -1), kxs.reshape(nbs, ls, -1), kx_ctx, w_k, w_vt)
    xs1 = _a_out_call(attn_s, cvs.reshape(nbs, ls, -1), sgs.reshape(nbs, ls, -1), x_sample, mods, None,
                      *conv_args)

    mods = mods_all[1]
    gpre = row2(g_pre[1])
    w_in = c_w_in[0]
    w_main = w_in.astype(BF16)
    dtw = w_in[:, C_MAIN:]
    zpad = jnp.zeros((d, HEAD_SLAB - SSD_HEADS), F32)
    w_dt = jnp.concatenate([dtw[:, 0:SSD_HEADS], zpad, dtw[:, SSD_HEADS:], zpad], axis=1).astype(BF16)
    pad32 = jnp.zeros((2, HEAD_SLAB - SSD_HEADS), F32)
    dt_b = jnp.concatenate([c_dt_bias[0], pad32], axis=1).reshape(1, 2 * HEAD_SLAB)
    a_neg = jnp.concatenate([-jnp.exp(c_a_log[0]), pad32], axis=1).reshape(2, 1, HEAD_SLAB)
    lane_rep = lambda v: jnp.broadcast_to(v.reshape(-1, 1), (v.size, LANES))
    d_skip = lane_rep(jnp.repeat(c_d[0], SSD_HEAD_DIM))
    gnorm = lane_rep(c_g_norm[0])
    c_wout = c_w_out[0].astype(BF16)
    gfin = row2(g_final)
    cw, cbias = c_conv_w[0], row2(c_conv_b[0])

    xp1f = xp1.reshape(nbp * lp, d)
    up, dtp = _c_in_call(xp1f, mods, ctx_row, lp, gpre, w_main, w_dt, dt_b, cw, cbias)
    y_prompt, st_p = _ssd_call(up, dtp, xp1f, lp, None, a_neg, d_skip, gnorm, c_wout, mods, ctx_row, gfin, True)
    xs1f = xs1.reshape(nbs * ls, d)
    us, dts = _c_in_call(xs1f, mods, None, ls, gpre, w_main, w_dt, dt_b, cw, cbias)
    st0 = state_ssd[:, 0].reshape(nbs, 2, SSD_INNER, SSD_STATE)
    (y_sample,) = _ssd_call(us, dts, xs1f, ls, st0, a_neg, d_skip, gnorm, c_wout, mods, None, gfin, False)

    new_cache_ckv = ckv_p.reshape(nbp, 1, lp, KV_RANK)
    new_cache_krope = kr_p.reshape(nbp, 1, lp, QK_ROPE)
    new_state_ssd = st_p.reshape(nbp, 1, 2, SSD_HEADS, SSD_HEAD_DIM, SSD_STATE)
    return (y_prompt.reshape(nbp, lp, d), y_sample.reshape(nbs, ls, d), new_cache_ckv, new_cache_krope,
            new_state_ssd)
```
